```python
import jax, jax.numpy as jnp
from jax import lax
import numpy as np

D_MODEL = 1024
BATCH = 32
SEQ = 256
DEPTH = 1
DEC_BATCH = 8
DEC_SEQ = 4096
PAST_LEN = 256

GRID_W = 64
N_DN_HEADS = 4
DN_HEAD_K = 128
DN_HEAD_V = 128
QK_WIDTH = N_DN_HEADS * DN_HEAD_K
DN_WIDTH = N_DN_HEADS * DN_HEAD_V
N_FOURIER_GROUPS = 4
FOURIER_GROUP = 128
FOURIER_WIDTH = N_FOURIER_GROUPS * FOURIER_GROUP
MIX_WIDTH = DN_WIDTH + FOURIER_WIDTH
N_DIR = 2
CONV_K = 3
CHUNK = 64
D_FF = -(-8 * D_MODEL // (3 * 256)) * 256
QKV_WIDTH = 2 * QK_WIDTH + DN_WIDTH
IN_SPLITS = [QKV_WIDTH, QKV_WIDTH + DN_WIDTH, QKV_WIDTH + DN_WIDTH + N_DIR * N_DN_HEADS,
             QKV_WIDTH + DN_WIDTH + 2 * N_DIR * N_DN_HEADS]
IN_COLS = QKV_WIDTH + DN_WIDTH + 2 * N_DIR * N_DN_HEADS + FOURIER_WIDTH
RMS_EPS = 1e-6

kernel_name = "hybrid_deltanet_fourier_diffusion_step"


def rmsnorm(x, g):
    xf = x.astype(jnp.float32)
    y = xf * lax.rsqrt(jnp.mean(xf * xf, axis=-1, keepdims=True) + RMS_EPS)
    return (y * g.astype(jnp.float32)).astype(x.dtype)


def l2norm(x):
    return x * lax.rsqrt(jnp.sum(x * x, axis=-1, keepdims=True) + 1e-6)


def short_conv(x, w):
    C = x.shape[-1]
    return lax.conv_general_dilated(x, w[:, None, :].astype(x.dtype), window_strides=(1,),
                                    padding=[(CONV_K // 2, CONV_K // 2)],
                                    dimension_numbers=('NWC', 'WIO', 'NWC'),
                                    feature_group_count=C)


def to_chunks(t):
    B, N = t.shape[:2]
    t = t.reshape(B, N // CHUNK, CHUNK, *t.shape[2:])
    return jnp.moveaxis(t, 2, 3)


def gated_delta_chunked(q, k, v, beta, log_a, s0):
    B, N, H, _ = q.shape
    DV = v.shape[-1]
    q, k, v, beta, log_a = map(to_chunks, (q, k, v, beta, log_a))
    g = jnp.cumsum(log_a, axis=-1)
    idx = jnp.arange(CHUNK)
    incl = idx[:, None] >= idx[None, :]
    strict = idx[:, None] > idx[None, :]
    decay = jnp.exp(jnp.where(incl, g[..., :, None] - g[..., None, :], -jnp.inf))
    kk = jnp.einsum('bnhik,bnhjk->bnhij', k, k)
    a_mat = jnp.where(strict, beta[..., :, None] * decay * kk, 0.0)
    eye = jnp.eye(CHUNK, dtype=q.dtype)
    rhs = jnp.concatenate([beta[..., None] * v, (beta * jnp.exp(g))[..., None] * k], axis=-1)
    sol = lax.linalg.triangular_solve(eye + a_mat, rhs, left_side=True, lower=True,
                                      unit_diagonal=True)
    u_v, w = sol[..., :DV], sol[..., DV:]
    aqk = jnp.einsum('bnhik,bnhjk->bnhij', q, k) * decay
    qg = q * jnp.exp(g)[..., None]
    g_last = g[..., -1:]
    kd = k * jnp.exp(g_last - g)[..., None]
    gl = jnp.exp(g_last[..., 0])
    xs = tuple(jnp.moveaxis(t, 1, 0) for t in (u_v, w, qg, aqk, kd, gl))

    def step(s, inp):
        u_v_c, w_c, qg_c, aqk_c, kd_c, gl_c = inp
        u = u_v_c - jnp.einsum('bhck,bhkv->bhcv', w_c, s)
        o = jnp.einsum('bhck,bhkv->bhcv', qg_c, s) + jnp.einsum('bhij,bhjv->bhiv', aqk_c, u)
        s = gl_c[..., None, None] * s + jnp.einsum('bhck,bhcv->bhkv', kd_c, u)
        return s, o

    s_end, o = lax.scan(step, s0, xs)
    o = jnp.moveaxis(jnp.moveaxis(o, 0, 1), 3, 2).reshape(B, N, H, DV)
    return o, s_end


def bidir_delta(q, k, v, beta, log_a, s0_f, s0_b):
    o_f, s_f = gated_delta_chunked(q, k, v, beta[:, :, 0], log_a[:, :, 0], s0_f)
    flip = lambda t: jnp.flip(t, axis=1)
    o_b, s_b = gated_delta_chunked(flip(q), flip(k), flip(v), flip(beta[:, :, 1]),
                                   flip(log_a[:, :, 1]), s0_b)
    return o_f + flip(o_b), s_f, s_b


def fourier_mix(f, w_fno, on_grid):
    B, N, _ = f.shape
    ff = f.astype(jnp.float32).reshape(B, N, N_FOURIER_GROUPS, FOURIER_GROUP)
    if on_grid:
        rows = N // GRID_W
        ff = ff.reshape(B, rows, GRID_W, N_FOURIER_GROUPS, FOURIER_GROUP)
        spec = jnp.fft.fftn(ff, axes=(1, 2, 4)).real
    else:
        spec = jnp.fft.fftn(ff, axes=(1, 3)).real
    spec = spec.reshape(B, N, N_FOURIER_GROUPS, FOURIER_GROUP) * (N * FOURIER_GROUP) ** -0.5
    out = jnp.einsum('bngc,gcd->bngd', spec, w_fno.astype(jnp.float32))
    return out.reshape(B, N, FOURIER_WIDTH).astype(f.dtype)


def trunk_layer(x, cond, s0_f, s0_b, on_grid, w_ada, b_ada, g_mix, w_in, w_conv, a_log,
                dt_bias, g_o, w_fno, w_out, g_ffn, w_gu, w_down):
    B, N, _ = x.shape
    mod = (jax.nn.silu(cond) @ w_ada + b_ada)[..., None, :]
    shift1, scale1, gate1, shift2, scale2, gate2 = jnp.split(mod, 6, axis=-1)
    h = rmsnorm(x, g_mix) * (1 + scale1) + shift1
    p = h @ w_in
    qkv, z, a_raw, b_raw, f = jnp.split(p, IN_SPLITS, axis=-1)
    qkv = jax.nn.silu(short_conv(qkv, w_conv)).astype(jnp.float32)
    q, k, v = jnp.split(qkv, [QK_WIDTH, 2 * QK_WIDTH], axis=-1)
    q = l2norm(q.reshape(B, N, N_DN_HEADS, DN_HEAD_K)) * DN_HEAD_K ** -0.5
    k = l2norm(k.reshape(B, N, N_DN_HEADS, DN_HEAD_K))
    v = v.reshape(B, N, N_DN_HEADS, DN_HEAD_V)
    beta = jax.nn.sigmoid(b_raw.astype(jnp.float32)).reshape(B, N, N_DIR, N_DN_HEADS)
    log_a = -jnp.exp(a_log.astype(jnp.float32)) * jax.nn.softplus(
        a_raw.astype(jnp.float32).reshape(B, N, N_DIR, N_DN_HEADS) + dt_bias.astype(jnp.float32))
    o, s_f, s_b = bidir_delta(q, k, v, beta, log_a, s0_f, s0_b)
    o = rmsnorm(o, g_o) * jax.nn.silu(z.astype(jnp.float32).reshape(B, N, N_DN_HEADS, DN_HEAD_V))
    mix = jnp.concatenate([o.reshape(B, N, DN_WIDTH).astype(x.dtype),
                           fourier_mix(f, w_fno, on_grid)], axis=-1)
    x = x + gate1 * (mix @ w_out)
    h2 = rmsnorm(x, g_ffn) * (1 + scale2) + shift2
    gt, up = jnp.split(h2 @ w_gu, 2, axis=-1)
    x = x + gate2 * ((jax.nn.silu(gt) * up) @ w_down)
    return x, s_f, s_b


def setup_inputs(seed: int = 0) -> dict:
    key = jax.random.key(seed)
    ks = jax.random.split(key, 24)
    nrm = lambda k, s, sc: jax.random.normal(k, s, jnp.float32) * sc
    L, D, H = DEPTH, D_MODEL, N_DN_HEADS
    state_shape = (DEC_BATCH, L, H, DN_HEAD_K, DN_HEAD_V)
    dt = jnp.exp(jax.random.uniform(ks[13], (L, N_DIR, H), jnp.float32,
                                    np.log(1e-3), np.log(1e-1)))
    return {
        "x_prompt": nrm(ks[0], (BATCH, SEQ, D), 1.0),
        "x_sample": nrm(ks[1], (DEC_BATCH, DEC_SEQ, D), 1.0),
        "c": nrm(ks[2], (DEC_BATCH, D), 1.0),
        "state_dn_fwd": nrm(ks[3], state_shape, DN_HEAD_K ** -0.5),
        "state_dn_bwd": nrm(ks[4], state_shape, DN_HEAD_K ** -0.5),
        "c_ctx": nrm(ks[5], (D,), 1.0),
        "w_ada": nrm(ks[6], (L, D, 6 * D), D ** -0.5),
        "b_ada": nrm(ks[7], (L, 6 * D), 0.02),
        "g_mix": 1.0 + nrm(ks[8], (L, D), 0.02),
        "w_in": nrm(ks[9], (L, D, IN_COLS), D ** -0.5),
        "w_conv": nrm(ks[10], (L, CONV_K, QKV_WIDTH), CONV_K ** -0.5),
        "a_log": jnp.log(jax.random.uniform(ks[11], (L, N_DIR, H), jnp.float32, 1.0, 16.0)),
        "dt_bias": dt + jnp.log(-jnp.expm1(-dt)),
        "g_o": 1.0 + nrm(ks[12], (L, DN_HEAD_V), 0.02),
        "w_fno": nrm(ks[14], (L, N_FOURIER_GROUPS, FOURIER_GROUP, FOURIER_GROUP), FOURIER_GROUP ** -0.5),
        "w_out": nrm(ks[15], (L, MIX_WIDTH, D), MIX_WIDTH ** -0.5),
        "g_ffn": 1.0 + nrm(ks[16], (L, D), 0.02),
        "w_gu": nrm(ks[17], (L, D, 2 * D_FF), D ** -0.5),
        "w_down": nrm(ks[18], (L, D_FF, D), D_FF ** -0.5),
        "g_final": 1.0 + nrm(ks[19], (D,), 0.02),
    }


def reference(x_prompt, x_sample, c, state_dn_fwd, state_dn_bwd, c_ctx, w_ada, b_ada, g_mix,
              w_in, w_conv, a_log, dt_bias, g_o, w_fno, w_out, g_ffn, w_gu, w_down, g_final):
    bp = x_prompt.shape[0]
    zero = jnp.zeros((bp, N_DN_HEADS, DN_HEAD_K, DN_HEAD_V), jnp.float32)
    hp, hs = x_prompt, x_sample
    new_f, new_b = [], []
    for l in range(DEPTH):
        w = (w_ada[l], b_ada[l], g_mix[l], w_in[l], w_conv[l], a_log[l], dt_bias[l], g_o[l],
             w_fno[l], w_out[l], g_ffn[l], w_gu[l], w_down[l])
        hp, s_f, s_b = trunk_layer(hp, c_ctx, zero, zero, False, *w)
        new_f.append(s_f)
        new_b.append(s_b)
        hs, _, _ = trunk_layer(hs, c, state_dn_fwd[:, l].astype(jnp.float32),
                               state_dn_bwd[:, l].astype(jnp.float32), True, *w)
    y_prompt = rmsnorm(hp, g_final)
    y_sample = rmsnorm(hs, g_final)
    new_state_fwd = jnp.stack(new_f, axis=1)
    new_state_bwd = jnp.stack(new_b, axis=1)
    return (y_prompt, y_sample, new_state_fwd, new_state_bwd)
```

```python
import functools

import numpy as np
import jax
import jax.numpy as jnp
from jax import lax
from jax.experimental import pallas as pl
from jax.experimental.pallas import tpu as pltpu

D_MODEL = 1024
N_HEADS = 4
HEAD_DIM = 128
QK_WIDTH = N_HEADS * HEAD_DIM
QKV_WIDTH = 3 * QK_WIDTH
N_GROUPS = 4
FOURIER_WIDTH = N_GROUPS * HEAD_DIM
N_DIR = 2
GRID_W = 64
CHUNK = 64
D_FF = 2816
RMS_EPS = 1e-6
LANES = 128
W_CAT_COLS = QKV_WIDTH + QK_WIDTH + FOURIER_WIDTH + LANES
VMEM_LIMIT = 56 * 1024 * 1024

F32 = jnp.float32
BF16 = jnp.bfloat16
HIGHEST = lax.Precision.HIGHEST


def _dot(a, b):
    return jnp.dot(a, b, preferred_element_type=F32)


def _silu(x):
    return x * jax.nn.sigmoid(x)


def _softplus(x):
    return jnp.maximum(x, 0.0) + jnp.log1p(jnp.exp(-jnp.abs(x)))


def _params(**kw):
    return pltpu.CompilerParams(vmem_limit_bytes=VMEM_LIMIT, **kw)


def _mod_kernel(cond_ref, w_ref, b_ref, o_ref):
    s = _silu(cond_ref[...]).astype(BF16)
    o_ref[...] = _dot(s, w_ref[...].astype(BF16)) + b_ref[...]


def _mod_call(cond, w_ada, b_ada):
    rows, d = cond.shape
    cols = w_ada.shape[1]
    tn = 1536
    return pl.pallas_call(
        _mod_kernel,
        grid=(cols // tn,),
        in_specs=[pl.BlockSpec((rows, d), lambda j: (0, 0)),
                  pl.BlockSpec((d, tn), lambda j: (0, j)),
                  pl.BlockSpec((1, tn), lambda j: (0, j))],
        out_specs=pl.BlockSpec((rows, tn), lambda j: (0, j)),
        out_shape=jax.ShapeDtypeStruct((rows, cols), F32),
        compiler_params=_params(),
    )(cond, w_ada, b_ada.reshape(1, cols))


def _fno_w_kernel(w_ref, cc_ref, sc_ref, wp_ref, wg_ref):
    w = w_ref[...]
    cw = jnp.dot(cc_ref[...], w, precision=HIGHEST, preferred_element_type=F32)
    sw = jnp.dot(sc_ref[...], w, precision=HIGHEST, preferred_element_type=F32)
    wp_ref[...] = jnp.concatenate([cw, sw], axis=1).astype(BF16)
    wg_ref[...] = jnp.concatenate([cw, -sw, -sw, -cw], axis=1).astype(BF16)


def _fno_w_call(w_fno, cc, sc):
    g, c, _ = w_fno.shape
    return pl.pallas_call(
        _fno_w_kernel,
        grid=(g,),
        in_specs=[pl.BlockSpec((None, c, c), lambda i: (i, 0, 0)),
                  pl.BlockSpec((c, c), lambda i: (0, 0)),
                  pl.BlockSpec((c, c), lambda i: (0, 0))],
        out_specs=[pl.BlockSpec((None, c, 2 * c), lambda i: (i, 0, 0)),
                   pl.BlockSpec((None, c, 4 * c), lambda i: (i, 0, 0))],
        out_shape=[jax.ShapeDtypeStruct((g, c, 2 * c), BF16),
                   jax.ShapeDtypeStruct((g, c, 4 * c), BF16)],
        compiler_params=_params(),
    )(w_fno, cc, sc)


def _inproj_kernel(x_ref, mod_ref, g_ref, w_ref, qkv_ref, z_ref, f_ref, ab_ref):
    x = x_ref[...]
    y = x * lax.rsqrt(jnp.mean(x * x, axis=-1, keepdims=True) + RMS_EPS) * g_ref[...]
    shift1 = mod_ref[:, 0:D_MODEL]
    scale1 = mod_ref[:, D_MODEL:2 * D_MODEL]
    h = (y * (1.0 + scale1) + shift1).astype(BF16)
    c0, c1, c2 = QKV_WIDTH, QKV_WIDTH + QK_WIDTH, QKV_WIDTH + QK_WIDTH + FOURIER_WIDTH
    qkv_ref[...] = _dot(h, w_ref[:, 0:c0])
    z_ref[...] = _dot(h, w_ref[:, c0:c1]).astype(BF16)
    f_ref[...] = _dot(h, w_ref[:, c1:c2]).astype(BF16)
    ab_ref[...] = _dot(h, w_ref[:, c2:W_CAT_COLS])


def _inproj_call(x, mod3, mod_row0, g_mix, w_cat, tm):
    b, n, d = x.shape
    tok = lambda w: pl.BlockSpec((None, tm, w), lambda i, t: (i, t, 0))
    return pl.pallas_call(
        _inproj_kernel,
        grid=(b, n // tm),
        in_specs=[tok(d),
                  pl.BlockSpec((None, 1, 6 * d), lambda i, t: (mod_row0 + i, 0, 0)),
                  pl.BlockSpec((1, d), lambda i, t: (0, 0)),
                  pl.BlockSpec((d, W_CAT_COLS), lambda i, t: (0, 0))],
        out_specs=[tok(QKV_WIDTH), tok(QK_WIDTH), tok(FOURIER_WIDTH), tok(LANES)],
        out_shape=[jax.ShapeDtypeStruct((b, n, QKV_WIDTH), F32),
                   jax.ShapeDtypeStruct((b, n, QK_WIDTH), BF16),
                   jax.ShapeDtypeStruct((b, n, FOURIER_WIDTH), BF16),
                   jax.ShapeDtypeStruct((b, n, LANES), F32)],
        compiler_params=_params(),
    )(x, mod3, g_mix.reshape(1, d), w_cat)


def _tri_inverse(a):
    c = a.shape[0]
    eye = (lax.broadcasted_iota(jnp.int32, (c, c), 0) ==
           lax.broadcasted_iota(jnp.int32, (c, c), 1)).astype(F32)
    p = eye - a
    ak = a
    steps = int(np.log2(c)) - 1
    for _ in range(steps):
        ak = jnp.dot(ak, ak, precision=HIGHEST, preferred_element_type=F32)
        p = p + jnp.dot(p, ak, precision=HIGHEST, preferred_element_type=F32)
    return p


def _delta_kernel(*refs, n, has_state):
    if has_state:
        (alog_ref, dtb_ref, q_ref, k_ref, v_ref, z_ref, ab_ref, abt_ref, wq_ref, wk_ref, wv_ref,
         go_ref, sf0_ref, sb0_ref, og_ref, sf_ref, sb_ref, qs, ks, vs, osc) = refs
    else:
        (alog_ref, dtb_ref, q_ref, k_ref, v_ref, z_ref, ab_ref, abt_ref, wq_ref, wk_ref, wv_ref,
         go_ref, og_ref, sf_ref, sb_ref, qs, ks, vs, osc) = refs
    h = pl.program_id(1)
    nc = n // CHUNK
    rblk = 256
    nblk = n // rblk

    rows = lax.broadcasted_iota(jnp.int32, (rblk, LANES), 0)

    def conv_block(src_ref, w_ref, r0):
        x = src_ref[pl.ds(r0, rblk), :]
        prev8 = src_ref[pl.ds(jnp.maximum(r0 - 8, 0), 8), :]
        next8 = src_ref[pl.ds(jnp.minimum(r0 + rblk, n - 8), 8), :]
        prow = jnp.where(r0 > 0, prev8[7:8, :], 0.0)
        nrow = jnp.where(r0 + rblk < n, next8[0:1, :], 0.0)
        xp = jnp.where(rows == 0, prow, pltpu.roll(x, 1, 0))
        xn = jnp.where(rows == rblk - 1, nrow, pltpu.roll(x, rblk - 1, 0))
        y = w_ref[0:1, :] * xp + w_ref[1:2, :] * x + w_ref[2:3, :] * xn
        return _silu(y)

    def l2n(x):
        return x * lax.rsqrt(jnp.sum(x * x, axis=-1, keepdims=True) + 1e-6)

    def pre_body(i, carry):
        r0 = pl.multiple_of(i * rblk, rblk)
        qs[pl.ds(r0, rblk), :] = l2n(conv_block(q_ref, wq_ref, r0)) * (HEAD_DIM ** -0.5)
        ks[pl.ds(r0, rblk), :] = l2n(conv_block(k_ref, wk_ref, r0))
        vs[pl.ds(r0, rblk), :] = conv_block(v_ref, wv_ref, r0)
        osc[pl.ds(r0, rblk), :] = jnp.zeros((rblk, LANES), F32)
        return carry

    lax.fori_loop(0, nblk, pre_body, 0)

    ci = lax.broadcasted_iota(jnp.int32, (CHUNK, CHUNK), 0)
    cj = lax.broadcasted_iota(jnp.int32, (CHUNK, CHUNK), 1)
    lane = lax.broadcasted_iota(jnp.int32, (CHUNK, LANES), 1)
    incl = (ci >= cj, ci <= cj)
    strict = (ci > cj, ci < cj)
    tri = tuple(m.astype(F32) for m in incl)

    def chunk_step(c, d, s):
        r0 = pl.multiple_of(c * CHUNK, CHUNK)
        q = qs[pl.ds(r0, CHUNK), :]
        k = ks[pl.ds(r0, CHUNK), :]
        v = vs[pl.ds(r0, CHUNK), :]
        ab = ab_ref[pl.ds(r0, CHUNK), :]
        col = d * N_HEADS + h
        a_col = jnp.sum(jnp.where(lane == col, ab, 0.0), axis=-1, keepdims=True)
        b_col = jnp.sum(jnp.where(lane == 2 * N_HEADS + col, ab, 0.0), axis=-1, keepdims=True)
        a_row = abt_ref[c, pl.ds(col, 1), :]
        neg_a = -jnp.exp(jnp.full((1, 1), alog_ref[d, h], F32))
        dtb = dtb_ref[d, h]
        la_col = neg_a * _softplus(a_col + dtb)
        la_row = neg_a * _softplus(a_row + dtb)
        beta = jax.nn.sigmoid(b_col)
        g_col = jnp.dot(tri[d], jnp.broadcast_to(la_col, (CHUNK, LANES)),
                        precision=HIGHEST, preferred_element_type=F32)
        g_row = jnp.dot(jnp.broadcast_to(la_row, (8, CHUNK)), tri[1 - d],
                        precision=HIGHEST, preferred_element_type=F32)[0:1, :]
        g_tot = g_col[CHUNK - 1:CHUNK, :] if d == 0 else g_col[0:1, :]
        decay = jnp.exp(jnp.where(incl[d], g_col[:, 0:CHUNK] - g_row, -jnp.inf))
        kb = k.astype(BF16)
        kq = lax.dot_general(jnp.concatenate([k, q], axis=0).astype(BF16), kb,
                             (((1,), (1,)), ((), ())), preferred_element_type=F32)
        kk = kq[0:CHUNK, :]
        qk = kq[CHUNK:2 * CHUNK, :]
        a_mat = jnp.where(strict[d], beta * decay * kk, 0.0)
        t_inv = _tri_inverse(a_mat)
        eg = jnp.exp(g_col)
        rhs = jnp.concatenate([beta * v, (beta * eg) * k], axis=1).astype(BF16)
        sol = _dot(t_inv.astype(BF16), rhs)
        u_v = sol[:, 0:HEAD_DIM]
        w = sol[:, HEAD_DIM:2 * HEAD_DIM]
        aqk = qk * decay
        qg = q * eg
        kd = k * jnp.exp(g_tot - g_col)
        gl = jnp.exp(g_tot)
        ws = _dot(jnp.concatenate([w, qg], axis=0).astype(BF16), s.astype(BF16))
        u = u_v - ws[0:CHUNK, :]
        ub = u.astype(BF16)
        o = ws[CHUNK:2 * CHUNK, :] + _dot(aqk.astype(BF16), ub)
        s_new = gl * s + _dot(kd.T.astype(BF16), ub)
        osc[pl.ds(r0, CHUNK), :] += o
        return s_new

    def scan_body(i, carry):
        s_f, s_b = carry
        s_f = chunk_step(i, 0, s_f)
        s_b = chunk_step(nc - 1 - i, 1, s_b)
        return s_f, s_b

    if has_state:
        init = (sf0_ref[...], sb0_ref[...])
    else:
        init = (jnp.zeros((HEAD_DIM, HEAD_DIM), F32), jnp.zeros((HEAD_DIM, HEAD_DIM), F32))
    s_f, s_b = lax.fori_loop(0, nc, scan_body, init)
    sf_ref[...] = s_f
    sb_ref[...] = s_b

    def post_body(i, carry):
        r0 = pl.multiple_of(i * rblk, rblk)
        o = osc[pl.ds(r0, rblk), :]
        y = o * lax.rsqrt(jnp.mean(o * o, axis=-1, keepdims=True) + RMS_EPS) * go_ref[...]
        zz = z_ref[pl.ds(r0, rblk), :].astype(F32)
        og_ref[pl.ds(r0, rblk), :] = (y * _silu(zz)).astype(BF16)
        return carry

    lax.fori_loop(0, nblk, post_body, 0)


def _delta_call(qkv, z, ab, abt, w_conv, a_log, dt_bias, g_o, s0_f, s0_b):
    b, n, _ = qkv.shape
    nc = n // CHUNK
    has_state = s0_f is not None
    col = lambda off: pl.BlockSpec((None, n, HEAD_DIM), lambda i, h: (i, 0, off + h))
    wsp = lambda off: pl.BlockSpec((3, HEAD_DIM), lambda i, h: (0, off + h))
    st = pl.BlockSpec((None, None, None, HEAD_DIM, HEAD_DIM), lambda i, h: (i, 0, h, 0, 0))
    smem = pl.BlockSpec(memory_space=pltpu.SMEM)
    in_specs = [smem, smem, col(0), col(N_HEADS), col(2 * N_HEADS), col(0),
                pl.BlockSpec((None, n, LANES), lambda i, h: (i, 0, 0)),
                pl.BlockSpec((None, nc, 2 * N_DIR * N_HEADS, CHUNK), lambda i, h: (i, 0, 0, 0)),
                wsp(0), wsp(N_HEADS), wsp(2 * N_HEADS),
                pl.BlockSpec((1, HEAD_DIM), lambda i, h: (0, 0))]
    args = [a_log, dt_bias, qkv, qkv, qkv, z, ab, abt, w_conv, w_conv, w_conv, g_o.reshape(1, HEAD_DIM)]
    if has_state:
        in_specs += [st, st]
        args += [s0_f, s0_b]
    state_shape = jax.ShapeDtypeStruct((b, 1, N_HEADS, HEAD_DIM, HEAD_DIM), F32)
    return pl.pallas_call(
        functools.partial(_delta_kernel, n=n, has_state=has_state),
        grid=(b, N_HEADS),
        in_specs=in_specs,
        out_specs=[col(0), st, st],
        out_shape=[jax.ShapeDtypeStruct((b, n, QK_WIDTH), BF16), state_shape, state_shape],
        scratch_shapes=[pltpu.VMEM((n, HEAD_DIM), F32)] * 4,
        compiler_params=_params(),
    )(*args)


def _dft_tables(n, scale=1.0):
    idx = np.arange(n)
    ang = 2.0 * np.pi * ((idx[:, None] * idx[None, :]) % n) / n
    return (np.cos(ang) * scale).astype(np.float32), (np.sin(ang) * scale).astype(np.float32)


def _fno_prompt_kernel(f_ref, wy_ref, cn_ref, sn_ref, o_ref):
    cn = cn_ref[...]
    sn = sn_ref[...]
    for g in range(N_GROUPS):
        y = _dot(f_ref[:, g * HEAD_DIM:(g + 1) * HEAD_DIM], wy_ref[g])
        o = _dot(cn, y[:, 0:HEAD_DIM].astype(BF16)) + _dot(sn, y[:, HEAD_DIM:].astype(BF16))
        o_ref[:, g * HEAD_DIM:(g + 1) * HEAD_DIM] = o.astype(BF16)


def _fno_prompt_call(f, wy):
    b, n, w = f.shape
    cn, sn = _dft_tables(n, (n * HEAD_DIM) ** -0.5)
    cn = jnp.asarray(cn).astype(BF16)
    sn_neg = jnp.asarray(-sn).astype(BF16)
    return pl.pallas_call(
        _fno_prompt_kernel,
        grid=(b,),
        in_specs=[pl.BlockSpec((None, n, w), lambda i: (i, 0, 0)),
                  pl.BlockSpec((N_GROUPS, HEAD_DIM, 2 * HEAD_DIM), lambda i: (0, 0, 0)),
                  pl.BlockSpec((n, n), lambda i: (0, 0)),
                  pl.BlockSpec((n, n), lambda i: (0, 0))],
        out_specs=pl.BlockSpec((None, n, w), lambda i: (i, 0, 0)),
        out_shape=jax.ShapeDtypeStruct((b, n, w), BF16),
        compiler_params=_params(),
    )(f, wy, cn, sn_neg)


def _fno_grid_kernel(f_ref, wy_ref, bdc_ref, bds_ref, cr_ref, sr_ref, o_ref, zr_sc, zi_sc, o_sc, *, n):
    tb = bdc_ref.shape[0]
    two = 2 * HEAD_DIM

    def col_body(i, carry):
        r0 = pl.multiple_of(i * tb, tb)
        y = _dot(f_ref[pl.ds(r0, tb), :], wy_ref[...]).astype(BF16)
        z = _dot(bdc_ref[...], y[:, 0:two]) + _dot(bds_ref[...], y[:, two:])
        zr_sc[pl.ds(r0, tb), :] = z[:, 0:HEAD_DIM]
        zi_sc[pl.ds(r0, tb), :] = z[:, HEAD_DIM:]
        return carry

    lax.fori_loop(0, n // tb, col_body, 0)

    def row_body(wp, carry):
        col = pl.ds(wp, n // GRID_W, stride=GRID_W)
        o = (_dot(cr_ref[...], zr_sc[col, :].astype(BF16)) +
             _dot(sr_ref[...], zi_sc[col, :].astype(BF16)))
        o_sc[col, :] = o
        return carry

    lax.fori_loop(0, GRID_W, row_body, 0)
    o_ref[...] = o_sc[...].astype(BF16)


def _fno_grid_call(f, wy4):
    b, n, w = f.shape
    rows = n // GRID_W
    tb = 256
    cw, sw = _dft_tables(GRID_W)
    rep = np.eye(tb // GRID_W, dtype=np.float32)
    bdc = jnp.asarray(np.kron(rep, cw)).astype(BF16)
    bds = jnp.asarray(np.kron(rep, sw)).astype(BF16)
    cr, sr = _dft_tables(rows, (n * HEAD_DIM) ** -0.5)
    cr = jnp.asarray(cr).astype(BF16)
    sr = jnp.asarray(sr).astype(BF16)
    const = lambda s: pl.BlockSpec(s, lambda i, g: (0,) * len(s))
    return pl.pallas_call(
        functools.partial(_fno_grid_kernel, n=n),
        grid=(b, N_GROUPS),
        in_specs=[pl.BlockSpec((None, n, HEAD_DIM), lambda i, g: (i, 0, g)),
                  pl.BlockSpec((None, HEAD_DIM, 4 * HEAD_DIM), lambda i, g: (g, 0, 0)),
                  const((tb, tb)), const((tb, tb)), const((rows, rows)), const((rows, rows))],
        out_specs=pl.BlockSpec((None, n, HEAD_DIM), lambda i, g: (i, 0, g)),
        out_shape=jax.ShapeDtypeStruct((b, n, w), BF16),
        scratch_shapes=[pltpu.VMEM((n, HEAD_DIM), F32)] * 3,
        compiler_params=_params(),
    )(f, wy4, bdc, bds, cr, sr)


FF_BLOCK = 256


def _ffn_kernel(x_ref, og_ref, fo_ref, mod_ref, woa_ref, wob_ref, gffn_ref, wg_ref, wu_ref, wd_ref,
                gfin_ref, y_ref):
    d = D_MODEL
    gate1 = mod_ref[:, 2 * d:3 * d]
    shift2 = mod_ref[:, 3 * d:4 * d]
    scale2 = mod_ref[:, 4 * d:5 * d]
    gate2 = mod_ref[:, 5 * d:6 * d]
    mo = _dot(og_ref[...], woa_ref[...]) + _dot(fo_ref[...], wob_ref[...])
    x1 = x_ref[...] + gate1 * mo
    hn = x1 * lax.rsqrt(jnp.mean(x1 * x1, axis=-1, keepdims=True) + RMS_EPS) * gffn_ref[...]
    h2 = (hn * (1.0 + scale2) + shift2).astype(BF16)
    acc = jnp.zeros(x1.shape, F32)
    for j in range(D_FF // FF_BLOCK):
        sl = slice(j * FF_BLOCK, (j + 1) * FF_BLOCK)
        gt = _dot(h2, wg_ref[:, sl])
        up = _dot(h2, wu_ref[:, sl])
        acc = acc + _dot((_silu(gt) * up).astype(BF16), wd_ref[sl, :])
    x2 = x1 + gate2 * acc
    y_ref[...] = x2 * lax.rsqrt(jnp.mean(x2 * x2, axis=-1, keepdims=True) + RMS_EPS) * gfin_ref[...]


def _ffn_call(x, og, fo, mod3, mod_row0, w_out_a, w_out_b, g_ffn, w_g, w_u, w_down, g_final, tm):
    b, n, d = x.shape
    tok = lambda w: pl.BlockSpec((None, tm, w), lambda i, t: (i, t, 0))
    const = lambda s: pl.BlockSpec(s, lambda i, t: (0,) * len(s), pipeline_mode=pl.Buffered(1))
    return pl.pallas_call(
        _ffn_kernel,
        grid=(b, n // tm),
        in_specs=[tok(d), tok(QK_WIDTH), tok(FOURIER_WIDTH),
                  pl.BlockSpec((None, 1, 6 * d), lambda i, t: (mod_row0 + i, 0, 0)),
                  const((QK_WIDTH, d)), const((FOURIER_WIDTH, d)), const((1, d)),
                  const((d, D_FF)), const((d, D_FF)), const((D_FF, d)), const((1, d))],
        out_specs=tok(d),
        out_shape=jax.ShapeDtypeStruct((b, n, d), F32),
        compiler_params=_params(),
    )(x, og, fo, mod3, w_out_a, w_out_b, g_ffn.reshape(1, d), w_g, w_u, w_down, g_final.reshape(1, d))


def _chunk_transposed(ab):
    b, n, _ = ab.shape
    g = ab[:, :, 0:2 * N_DIR * N_HEADS].reshape(b, n // CHUNK, CHUNK, 2 * N_DIR * N_HEADS)
    return jnp.swapaxes(g, 2, 3)


def kernel(x_prompt, x_sample, c, state_dn_fwd, state_dn_bwd, c_ctx, w_ada, b_ada, g_mix, w_in, w_conv,
           a_log, dt_bias, g_o, w_fno, w_out, g_ffn, w_gu, w_down, g_final):
    d = D_MODEL
    bp, np_, _ = x_prompt.shape
    bs, ns, _ = x_sample.shape
    l = 0

    wi = w_in[l]
    n_gate = 2 * N_DIR * N_HEADS
    g0 = QKV_WIDTH + QK_WIDTH
    w_cat = jnp.concatenate([wi[:, 0:g0], wi[:, g0 + n_gate:], wi[:, g0:g0 + n_gate],
                             jnp.zeros((d, LANES - n_gate), F32)], axis=1).astype(BF16)
    w_out_b16 = w_out[l].astype(BF16)
    w_out_a, w_out_b = w_out_b16[0:QK_WIDTH], w_out_b16[QK_WIDTH:]
    w_g = w_gu[l][:, 0:D_FF].astype(BF16)
    w_u = w_gu[l][:, D_FF:].astype(BF16)
    w_dn = w_down[l].astype(BF16)

    cond = jnp.concatenate([c_ctx[None, :], c, jnp.zeros((16 - 1 - bs, d), F32)], axis=0)
    mod = _mod_call(cond, w_ada[l], b_ada[l])
    mod3 = mod.reshape(16, 1, 6 * d)

    cc, sc = _dft_tables(HEAD_DIM)
    wy_p, wy_g = _fno_w_call(w_fno[l], jnp.asarray(cc), jnp.asarray(sc))

    xp = x_prompt.reshape(1, bp * np_, d)
    qkv, z, f, ab = _inproj_call(xp, mod3, 0, g_mix[l], w_cat, 512)
    qkv = qkv.reshape(bp, np_, QKV_WIDTH)
    z = z.reshape(bp, np_, QK_WIDTH)
    f = f.reshape(bp, np_, FOURIER_WIDTH)
    ab = ab.reshape(bp, np_, LANES)
    og, new_f, new_b = _delta_call(qkv, z, ab, _chunk_transposed(ab), w_conv[l], a_log[l], dt_bias[l],
                                   g_o[l], None, None)
    fo = _fno_prompt_call(f, wy_p)
    y_prompt = _ffn_call(xp, og.reshape(1, bp * np_, QK_WIDTH), fo.reshape(1, bp * np_, FOURIER_WIDTH),
                         mod3, 0, w_out_a, w_out_b, g_ffn[l], w_g, w_u, w_dn, g_final, 512)
    y_prompt = y_prompt.reshape(bp, np_, d)

    qkv, z, f, ab = _inproj_call(x_sample, mod3, 1, g_mix[l], w_cat, 512)
    og, _, _ = _delta_call(qkv, z, ab, _chunk_transposed(ab), w_conv[l], a_log[l], dt_bias[l], g_o[l],
                           state_dn_fwd[:, l:l + 1], state_dn_bwd[:, l:l + 1])
    fo = _fno_grid_call(f, wy_g)
    y_sample = _ffn_call(x_sample, og, fo, mod3, 1, w_out_a, w_out_b, g_ffn[l], w_g, w_u, w_dn, g_final, 512)

    return (y_prompt, y_sample, new_f, new_b)
```

```python
import functools

import numpy as np
import jax
import jax.numpy as jnp
from jax import lax
from jax.experimental import pallas as pl
from jax.experimental.pallas import tpu as pltpu

D_MODEL = 1024
N_HEADS = 4
HEAD_DIM = 128
QK_WIDTH = N_HEADS * HEAD_DIM
QKV_WIDTH = 3 * QK_WIDTH
N_GROUPS = 4
FOURIER_WIDTH = N_GROUPS * HEAD_DIM
N_DIR = 2
GRID_W = 64
CHUNK = 64
D_FF = 2816
RMS_EPS = 1e-6
LANES = 128
W_CAT_COLS = QKV_WIDTH + QK_WIDTH + FOURIER_WIDTH + LANES
VMEM_LIMIT = 56 * 1024 * 1024

F32 = jnp.float32
BF16 = jnp.bfloat16
HIGHEST = lax.Precision.HIGHEST


def _dot(a, b):
    return jnp.dot(a, b, preferred_element_type=F32)


def _silu(x):
    return x * jax.nn.sigmoid(x)


def _softplus(x):
    return jnp.maximum(x, 0.0) + jnp.log1p(jnp.exp(-jnp.abs(x)))


def _params(**kw):
    return pltpu.CompilerParams(vmem_limit_bytes=VMEM_LIMIT, **kw)


def _mod_kernel(cond_ref, w_ref, b_ref, o_ref):
    s = _silu(cond_ref[...]).astype(BF16)
    o_ref[...] = _dot(s, w_ref[...].astype(BF16)) + b_ref[...]


def _mod_call(cond, w_ada, b_ada):
    rows, d = cond.shape
    cols = w_ada.shape[1]
    tn = 1536
    return pl.pallas_call(
        _mod_kernel,
        grid=(cols // tn,),
        in_specs=[pl.BlockSpec((rows, d), lambda j: (0, 0)),
                  pl.BlockSpec((d, tn), lambda j: (0, j)),
                  pl.BlockSpec((1, tn), lambda j: (0, j))],
        out_specs=pl.BlockSpec((rows, tn), lambda j: (0, j)),
        out_shape=jax.ShapeDtypeStruct((rows, cols), F32),
        compiler_params=_params(),
    )(cond, w_ada, b_ada.reshape(1, cols))


def _fno_w_kernel(w_ref, cc_ref, sc_ref, wp_ref, wg_ref):
    w = w_ref[...]
    cw = jnp.dot(cc_ref[...], w, precision=HIGHEST, preferred_element_type=F32)
    sw = jnp.dot(sc_ref[...], w, precision=HIGHEST, preferred_element_type=F32)
    wp_ref[...] = jnp.concatenate([cw, sw], axis=1).astype(BF16)
    wg_ref[...] = jnp.concatenate([cw, -sw, -sw, -cw], axis=1).astype(BF16)


def _fno_w_call(w_fno, cc, sc):
    g, c, _ = w_fno.shape
    return pl.pallas_call(
        _fno_w_kernel,
        grid=(g,),
        in_specs=[pl.BlockSpec((None, c, c), lambda i: (i, 0, 0)),
                  pl.BlockSpec((c, c), lambda i: (0, 0)),
                  pl.BlockSpec((c, c), lambda i: (0, 0))],
        out_specs=[pl.BlockSpec((None, c, 2 * c), lambda i: (i, 0, 0)),
                   pl.BlockSpec((None, c, 4 * c), lambda i: (i, 0, 0))],
        out_shape=[jax.ShapeDtypeStruct((g, c, 2 * c), BF16),
                   jax.ShapeDtypeStruct((g, c, 4 * c), BF16)],
        compiler_params=_params(),
    )(w_fno, cc, sc)


def _inproj_kernel(x_ref, mod_ref, g_ref, w_ref, qkv_ref, z_ref, f_ref, ab_ref):
    x = x_ref[...]
    y = x * lax.rsqrt(jnp.mean(x * x, axis=-1, keepdims=True) + RMS_EPS) * g_ref[...]
    shift1 = mod_ref[:, 0:D_MODEL]
    scale1 = mod_ref[:, D_MODEL:2 * D_MODEL]
    h = (y * (1.0 + scale1) + shift1).astype(BF16)
    c0, c1, c2 = QKV_WIDTH, QKV_WIDTH + QK_WIDTH, QKV_WIDTH + QK_WIDTH + FOURIER_WIDTH
    qkv_ref[...] = _dot(h, w_ref[:, 0:c0])
    z_ref[...] = _dot(h, w_ref[:, c0:c1]).astype(BF16)
    f_ref[...] = _dot(h, w_ref[:, c1:c2]).astype(BF16)
    ab_ref[...] = _dot(h, w_ref[:, c2:W_CAT_COLS])


def _inproj_call(x, mod3, mod_row0, g_mix, w_cat, tm):
    b, n, d = x.shape
    tok = lambda w: pl.BlockSpec((None, tm, w), lambda i, t: (i, t, 0))
    return pl.pallas_call(
        _inproj_kernel,
        grid=(b, n // tm),
        in_specs=[tok(d),
                  pl.BlockSpec((None, 1, 6 * d), lambda i, t: (mod_row0 + i, 0, 0)),
                  pl.BlockSpec((1, d), lambda i, t: (0, 0)),
                  pl.BlockSpec((d, W_CAT_COLS), lambda i, t: (0, 0))],
        out_specs=[tok(QKV_WIDTH), tok(QK_WIDTH), tok(FOURIER_WIDTH), tok(LANES)],
        out_shape=[jax.ShapeDtypeStruct((b, n, QKV_WIDTH), F32),
                   jax.ShapeDtypeStruct((b, n, QK_WIDTH), BF16),
                   jax.ShapeDtypeStruct((b, n, FOURIER_WIDTH), BF16),
                   jax.ShapeDtypeStruct((b, n, LANES), F32)],
        compiler_params=_params(),
    )(x, mod3, g_mix.reshape(1, d), w_cat)


DUP = 4
PREP_UNROLL = 2
WIDE = DUP * CHUNK


def _split_lhs(x4, odd_quarter):
    hi = x4.astype(BF16).astype(F32)
    return jnp.where(odd_quarter, x4 - hi, x4).astype(BF16)


def _split_rhs(x4):
    hi = x4.astype(BF16)
    lo = (x4 - hi.astype(F32)).astype(BF16)
    return jnp.concatenate([hi, hi, lo, lo], axis=0)


def _tri_inverse_x4(a4s, eye4, odd_quarter):
    steps = int(np.log2(CHUNK))
    p4s = [eye4 - a4 for a4 in a4s]
    bk4s = [_dot(_split_lhs(a4, odd_quarter), _split_rhs(a4)) for a4 in a4s]
    for step in range(1, steps):
        rhss = [_split_rhs(bk4) for bk4 in bk4s]
        if step + 1 < steps:
            ress = [_dot(jnp.concatenate([_split_lhs(bk4, odd_quarter), _split_lhs(p4, odd_quarter)], axis=0), rhs)
                    for bk4, p4, rhs in zip(bk4s, p4s, rhss)]
            bk4s = [res[0:CHUNK, :] for res in ress]
            p4s = [p4 + res[CHUNK:, :] for p4, res in zip(p4s, ress)]
        else:
            p4s = [p4 + _dot(_split_lhs(p4, odd_quarter), rhs) for p4, rhs in zip(p4s, rhss)]
    return p4s


def _delta_kernel(*refs, n, has_state):
    if has_state:
        (alog_ref, dtb_ref, q_ref, k_ref, v_ref, z_ref, ab_ref, abt_ref, wq_ref, wk_ref, wv_ref,
         go_ref, tric_ref, trir_ref, sf0_ref, sb0_ref, og_ref, sf_ref, sb_ref,
         wqg, uvs, akd, gls, osc) = refs
    else:
        (alog_ref, dtb_ref, q_ref, k_ref, v_ref, z_ref, ab_ref, abt_ref, wq_ref, wk_ref, wv_ref,
         go_ref, tric_ref, trir_ref, og_ref, sf_ref, sb_ref,
         wqg, uvs, akd, gls, osc) = refs
    h = pl.program_id(1)
    nc = n // CHUNK
    rblk = 256
    nblk = n // rblk

    rows = lax.broadcasted_iota(jnp.int32, (CHUNK, LANES), 0)
    lane = lax.broadcasted_iota(jnp.int32, (CHUNK, LANES), 1)
    ci4 = lax.broadcasted_iota(jnp.int32, (CHUNK, WIDE), 0)
    lane4 = lax.broadcasted_iota(jnp.int32, (CHUNK, WIDE), 1)
    cj4 = lane4 & (CHUNK - 1)
    odd_quarter = ((lane4 // CHUNK) & 1) == 1
    eye4 = (ci4 == cj4).astype(F32)
    incl4 = (ci4 >= cj4, ci4 <= cj4)
    strict4 = (ci4 > cj4, ci4 < cj4)
    q8 = lax.broadcasted_iota(jnp.int32, (8, WIDE), 1) // CHUNK

    def conv_chunk(src_ref, w_ref, r0):
        x = src_ref[pl.ds(r0, CHUNK), :]
        prev8 = src_ref[pl.ds(jnp.maximum(r0 - 8, 0), 8), :]
        next8 = src_ref[pl.ds(jnp.minimum(r0 + CHUNK, n - 8), 8), :]
        prow = jnp.where(r0 > 0, prev8[7:8, :], 0.0)
        nrow = jnp.where(r0 + CHUNK < n, next8[0:1, :], 0.0)
        xp = jnp.where(rows == 0, prow, pltpu.roll(x, 1, 0))
        xn = jnp.where(rows == CHUNK - 1, nrow, pltpu.roll(x, CHUNK - 1, 0))
        return _silu(w_ref[0:1, :] * xp + w_ref[1:2, :] * x + w_ref[2:3, :] * xn)

    def l2n(x):
        return x * lax.rsqrt(jnp.sum(x * x, axis=-1, keepdims=True) + 1e-6)

    def split3_rows(x):
        hi = x.astype(BF16)
        r1 = x - hi.astype(F32)
        lo = r1.astype(BF16)
        lo2 = (r1 - lo.astype(F32)).astype(BF16)
        return jnp.concatenate([hi, lo, lo2, jnp.zeros_like(hi)], axis=0)

    def split3_lanes(x4):
        hi = x4.astype(BF16).astype(F32)
        r1 = x4 - hi
        lo = r1.astype(BF16).astype(F32)
        r2 = r1 - lo
        return jnp.where(q8 == 0, x4, jnp.where(q8 == 1, r1, jnp.where(q8 == 2, r2, 0.0))).astype(BF16)

    def chain_front(c, d, q, k, v, kk4, qk, ab):
        col = d * N_HEADS + h
        a_col = jnp.sum(jnp.where(lane == col, ab, 0.0), axis=-1, keepdims=True)
        b_col = jnp.sum(jnp.where(lane == 2 * N_HEADS + col, ab, 0.0), axis=-1, keepdims=True)
        a_row4 = abt_ref[c, pl.ds(col, 1), :]
        neg_a = -jnp.exp(jnp.full((1, 1), alog_ref[d, h], F32))
        dtb = dtb_ref[d, h]
        la_col = neg_a * _softplus(a_col + dtb)
        la_row4 = neg_a * _softplus(a_row4 + dtb)
        beta = jax.nn.sigmoid(b_col)
        g_col = _dot(tric_ref[d], split3_rows(jnp.broadcast_to(la_col, (CHUNK, LANES))))
        g_row4 = _dot(split3_lanes(jnp.broadcast_to(la_row4, (8, WIDE))), trir_ref[d])[0:1, :]
        g_tot = g_col[CHUNK - 1:CHUNK, :] if d == 0 else g_col[0:1, :]
        g_col4 = jnp.concatenate([g_col, g_col], axis=1)
        decay4 = jnp.exp(jnp.where(incl4[d], g_col4 - g_row4, -jnp.inf))
        eg = jnp.exp(g_col)
        wqg[d, c, CHUNK:, :] = (q * eg).astype(BF16)
        akd[d, c, 0:CHUNK, :] = (qk * decay4[:, 0:CHUNK]).astype(BF16)
        akd[d, c, CHUNK:, :] = (k * jnp.exp(g_tot - g_col)).T.astype(BF16)
        gls[d, c] = jnp.broadcast_to(jnp.exp(g_tot), (8, LANES))
        a4 = jnp.where(strict4[d], beta * decay4 * kk4, 0.0)
        rhs = jnp.concatenate([beta * v, (beta * eg) * k], axis=1).astype(BF16)
        return a4, rhs

    def prepare(i, carry):
        chains = []
        for j in range(PREP_UNROLL):
            c = i * PREP_UNROLL + j
            r0 = pl.multiple_of(c * CHUNK, CHUNK)
            q = l2n(conv_chunk(q_ref, wq_ref, r0)) * (HEAD_DIM ** -0.5)
            k = l2n(conv_chunk(k_ref, wk_ref, r0))
            v = conv_chunk(v_ref, wv_ref, r0)
            kb = k.astype(BF16)
            kq4 = lax.dot_general(jnp.concatenate([kb, q.astype(BF16)], axis=0),
                                  jnp.concatenate([kb] * DUP, axis=0),
                                  (((1,), (1,)), ((), ())), preferred_element_type=F32)
            ab = ab_ref[pl.ds(r0, CHUNK), :]
            osc[pl.ds(r0, CHUNK), :] = jnp.zeros((CHUNK, LANES), F32)
            for d in range(N_DIR):
                chains.append((c, d) + chain_front(c, d, q, k, v, kq4[0:CHUNK, :], kq4[CHUNK:, 0:CHUNK], ab))
        t_invs = _tri_inverse_x4([ch[2] for ch in chains], eye4, odd_quarter)
        for (c, d, _, rhs), t4 in zip(chains, t_invs):
            sol = _dot(t4[:, 0:CHUNK].astype(BF16), rhs)
            uvs[d, c] = sol[:, 0:HEAD_DIM]
            wqg[d, c, 0:CHUNK, :] = sol[:, HEAD_DIM:].astype(BF16)
        return carry

    lax.fori_loop(0, nc // PREP_UNROLL, prepare, 0)

    def scan_step(c, d, s):
        r0 = pl.multiple_of(c * CHUNK, CHUNK)
        ws = _dot(wqg[d, c], s.astype(BF16))
        ub = (uvs[d, c] - ws[0:CHUNK, :]).astype(BF16)
        r = _dot(akd[d, c], ub)
        osc[pl.ds(r0, CHUNK), :] += ws[CHUNK:, :] + r[0:CHUNK, :]
        return gls[d, c][0:1, :] * s + r[CHUNK:, :]

    def scan_body(i, carry):
        s_f, s_b = carry
        return scan_step(i, 0, s_f), scan_step(nc - 1 - i, 1, s_b)

    if has_state:
        init = (sf0_ref[...], sb0_ref[...])
    else:
        init = (jnp.zeros((HEAD_DIM, HEAD_DIM), F32), jnp.zeros((HEAD_DIM, HEAD_DIM), F32))
    s_f, s_b = lax.fori_loop(0, nc, scan_body, init)
    sf_ref[...] = s_f
    sb_ref[...] = s_b

    def post_body(i, carry):
        r0 = pl.multiple_of(i * rblk, rblk)
        o = osc[pl.ds(r0, rblk), :]
        y = o * lax.rsqrt(jnp.mean(o * o, axis=-1, keepdims=True) + RMS_EPS) * go_ref[...]
        zz = z_ref[pl.ds(r0, rblk), :].astype(F32)
        og_ref[pl.ds(r0, rblk), :] = (y * _silu(zz)).astype(BF16)
        return carry

    lax.fori_loop(0, nblk, post_body, 0)


def _tri_tables():
    i = np.arange(CHUNK)
    low = (i[:, None] >= i[None, :]).astype(np.float32)
    up = low.T
    zc = np.zeros((CHUNK, CHUNK), np.float32)
    tric = np.stack([np.concatenate([m, m, m, zc], axis=1) for m in (low, up)])
    zw = np.zeros((CHUNK, WIDE), np.float32)
    trir = np.stack([np.concatenate([np.tile(m, (1, DUP))] * 3 + [zw], axis=0) for m in (up, low)])
    return jnp.asarray(tric).astype(BF16), jnp.asarray(trir).astype(BF16)


def _delta_call(qkv, z, ab, abt4, w_conv, a_log, dt_bias, g_o, s0_f, s0_b):
    b, n, _ = qkv.shape
    nc = n // CHUNK
    has_state = s0_f is not None
    tric, trir = _tri_tables()
    col = lambda off: pl.BlockSpec((None, n, HEAD_DIM), lambda i, h: (i, 0, off + h))
    wsp = lambda off: pl.BlockSpec((3, HEAD_DIM), lambda i, h: (0, off + h))
    st = pl.BlockSpec((None, None, None, HEAD_DIM, HEAD_DIM), lambda i, h: (i, 0, h, 0, 0))
    smem = pl.BlockSpec(memory_space=pltpu.SMEM)
    n_gate = 2 * N_DIR * N_HEADS
    in_specs = [smem, smem, col(0), col(N_HEADS), col(2 * N_HEADS), col(0),
                pl.BlockSpec((None, n, LANES), lambda i, h: (i, 0, 0)),
                pl.BlockSpec((None, nc, n_gate, WIDE), lambda i, h: (i, 0, 0, 0)),
                wsp(0), wsp(N_HEADS), wsp(2 * N_HEADS),
                pl.BlockSpec((1, HEAD_DIM), lambda i, h: (0, 0)),
                pl.BlockSpec((N_DIR, CHUNK, WIDE), lambda i, h: (0, 0, 0)),
                pl.BlockSpec((N_DIR, WIDE, WIDE), lambda i, h: (0, 0, 0))]
    args = [a_log, dt_bias, qkv, qkv, qkv, z, ab, abt4, w_conv, w_conv, w_conv, g_o.reshape(1, HEAD_DIM),
            tric, trir]
    if has_state:
        in_specs += [st, st]
        args += [s0_f, s0_b]
    state_shape = jax.ShapeDtypeStruct((b, 1, N_HEADS, HEAD_DIM, HEAD_DIM), F32)
    return pl.pallas_call(
        functools.partial(_delta_kernel, n=n, has_state=has_state),
        grid=(b, N_HEADS),
        in_specs=in_specs,
        out_specs=[col(0), st, st],
        out_shape=[jax.ShapeDtypeStruct((b, n, QK_WIDTH), BF16), state_shape, state_shape],
        scratch_shapes=[pltpu.VMEM((N_DIR, nc, 2 * CHUNK, HEAD_DIM), BF16),
                        pltpu.VMEM((N_DIR, nc, CHUNK, HEAD_DIM), F32),
                        pltpu.VMEM((N_DIR, nc, CHUNK + HEAD_DIM, CHUNK), BF16),
                        pltpu.VMEM((N_DIR, nc, 8, LANES), F32),
                        pltpu.VMEM((n, HEAD_DIM), F32)],
        compiler_params=_params(),
    )(*args)


def _dft_tables(n, scale=1.0):
    idx = np.arange(n)
    ang = 2.0 * np.pi * ((idx[:, None] * idx[None, :]) % n) / n
    return (np.cos(ang) * scale).astype(np.float32), (np.sin(ang) * scale).astype(np.float32)


def _fno_prompt_kernel(f_ref, wy_ref, cn_ref, sn_ref, o_ref):
    cn = cn_ref[...]
    sn = sn_ref[...]
    for g in range(N_GROUPS):
        y = _dot(f_ref[:, g * HEAD_DIM:(g + 1) * HEAD_DIM], wy_ref[g])
        o = _dot(cn, y[:, 0:HEAD_DIM].astype(BF16)) + _dot(sn, y[:, HEAD_DIM:].astype(BF16))
        o_ref[:, g * HEAD_DIM:(g + 1) * HEAD_DIM] = o.astype(BF16)


def _fno_prompt_call(f, wy):
    b, n, w = f.shape
    cn, sn = _dft_tables(n, (n * HEAD_DIM) ** -0.5)
    cn = jnp.asarray(cn).astype(BF16)
    sn_neg = jnp.asarray(-sn).astype(BF16)
    return pl.pallas_call(
        _fno_prompt_kernel,
        grid=(b,),
        in_specs=[pl.BlockSpec((None, n, w), lambda i: (i, 0, 0)),
                  pl.BlockSpec((N_GROUPS, HEAD_DIM, 2 * HEAD_DIM), lambda i: (0, 0, 0)),
                  pl.BlockSpec((n, n), lambda i: (0, 0)),
                  pl.BlockSpec((n, n), lambda i: (0, 0))],
        out_specs=pl.BlockSpec((None, n, w), lambda i: (i, 0, 0)),
        out_shape=jax.ShapeDtypeStruct((b, n, w), BF16),
        compiler_params=_params(),
    )(f, wy, cn, sn_neg)


def _fno_grid_kernel(f_ref, wy_ref, bdc_ref, bds_ref, cr_ref, sr_ref, o_ref, zr_sc, zi_sc, o_sc, *, n):
    tb = bdc_ref.shape[0]
    two = 2 * HEAD_DIM

    def col_body(i, carry):
        r0 = pl.multiple_of(i * tb, tb)
        y = _dot(f_ref[pl.ds(r0, tb), :], wy_ref[...]).astype(BF16)
        z = _dot(bdc_ref[...], y[:, 0:two]) + _dot(bds_ref[...], y[:, two:])
        zr_sc[pl.ds(r0, tb), :] = z[:, 0:HEAD_DIM]
        zi_sc[pl.ds(r0, tb), :] = z[:, HEAD_DIM:]
        return carry

    lax.fori_loop(0, n // tb, col_body, 0)

    def row_body(wp, carry):
        col = pl.ds(wp, n // GRID_W, stride=GRID_W)
        o = (_dot(cr_ref[...], zr_sc[col, :].astype(BF16)) +
             _dot(sr_ref[...], zi_sc[col, :].astype(BF16)))
        o_sc[col, :] = o
        return carry

    lax.fori_loop(0, GRID_W, row_body, 0)
    o_ref[...] = o_sc[...].astype(BF16)


def _fno_grid_call(f, wy4):
    b, n, w = f.shape
    rows = n // GRID_W
    tb = 256
    cw, sw = _dft_tables(GRID_W)
    rep = np.eye(tb // GRID_W, dtype=np.float32)
    bdc = jnp.asarray(np.kron(rep, cw)).astype(BF16)
    bds = jnp.asarray(np.kron(rep, sw)).astype(BF16)
    cr, sr = _dft_tables(rows, (n * HEAD_DIM) ** -0.5)
    cr = jnp.asarray(cr).astype(BF16)
    sr = jnp.asarray(sr).astype(BF16)
    const = lambda s: pl.BlockSpec(s, lambda i, g: (0,) * len(s))
    return pl.pallas_call(
        functools.partial(_fno_grid_kernel, n=n),
        grid=(b, N_GROUPS),
        in_specs=[pl.BlockSpec((None, n, HEAD_DIM), lambda i, g: (i, 0, g)),
                  pl.BlockSpec((None, HEAD_DIM, 4 * HEAD_DIM), lambda i, g: (g, 0, 0)),
                  const((tb, tb)), const((tb, tb)), const((rows, rows)), const((rows, rows))],
        out_specs=pl.BlockSpec((None, n, HEAD_DIM), lambda i, g: (i, 0, g)),
        out_shape=jax.ShapeDtypeStruct((b, n, w), BF16),
        scratch_shapes=[pltpu.VMEM((n, HEAD_DIM), F32)] * 3,
        compiler_params=_params(),
    )(f, wy4, bdc, bds, cr, sr)


FF_BLOCK = 256


def _ffn_kernel(x_ref, og_ref, fo_ref, mod_ref, woa_ref, wob_ref, gffn_ref, wg_ref, wu_ref, wd_ref,
                gfin_ref, y_ref):
    d = D_MODEL
    gate1 = mod_ref[:, 2 * d:3 * d]
    shift2 = mod_ref[:, 3 * d:4 * d]
    scale2 = mod_ref[:, 4 * d:5 * d]
    gate2 = mod_ref[:, 5 * d:6 * d]
    mo = _dot(og_ref[...], woa_ref[...]) + _dot(fo_ref[...], wob_ref[...])
    x1 = x_ref[...] + gate1 * mo
    hn = x1 * lax.rsqrt(jnp.mean(x1 * x1, axis=-1, keepdims=True) + RMS_EPS) * gffn_ref[...]
    h2 = (hn * (1.0 + scale2) + shift2).astype(BF16)
    acc = jnp.zeros(x1.shape, F32)
    for j in range(D_FF // FF_BLOCK):
        sl = slice(j * FF_BLOCK, (j + 1) * FF_BLOCK)
        gt = _dot(h2, wg_ref[:, sl])
        up = _dot(h2, wu_ref[:, sl])
        acc = acc + _dot((_silu(gt) * up).astype(BF16), wd_ref[sl, :])
    x2 = x1 + gate2 * acc
    y_ref[...] = x2 * lax.rsqrt(jnp.mean(x2 * x2, axis=-1, keepdims=True) + RMS_EPS) * gfin_ref[...]


def _ffn_call(x, og, fo, mod3, mod_row0, w_out_a, w_out_b, g_ffn, w_g, w_u, w_down, g_final, tm):
    b, n, d = x.shape
    tok = lambda w: pl.BlockSpec((None, tm, w), lambda i, t: (i, t, 0))
    const = lambda s: pl.BlockSpec(s, lambda i, t: (0,) * len(s), pipeline_mode=pl.Buffered(1))
    return pl.pallas_call(
        _ffn_kernel,
        grid=(b, n // tm),
        in_specs=[tok(d), tok(QK_WIDTH), tok(FOURIER_WIDTH),
                  pl.BlockSpec((None, 1, 6 * d), lambda i, t: (mod_row0 + i, 0, 0)),
                  const((QK_WIDTH, d)), const((FOURIER_WIDTH, d)), const((1, d)),
                  const((d, D_FF)), const((d, D_FF)), const((D_FF, d)), const((1, d))],
        out_specs=tok(d),
        out_shape=jax.ShapeDtypeStruct((b, n, d), F32),
        compiler_params=_params(),
    )(x, og, fo, mod3, w_out_a, w_out_b, g_ffn.reshape(1, d), w_g, w_u, w_down, g_final.reshape(1, d))


def _chunk_transposed(ab):
    b, n, _ = ab.shape
    g = ab[:, :, 0:2 * N_DIR * N_HEADS].reshape(b, n // CHUNK, CHUNK, 2 * N_DIR * N_HEADS)
    return jnp.tile(jnp.swapaxes(g, 2, 3), (1, 1, 1, DUP))


def kernel(x_prompt, x_sample, c, state_dn_fwd, state_dn_bwd, c_ctx, w_ada, b_ada, g_mix, w_in, w_conv,
           a_log, dt_bias, g_o, w_fno, w_out, g_ffn, w_gu, w_down, g_final):
    d = D_MODEL
    bp, np_, _ = x_prompt.shape
    bs, ns, _ = x_sample.shape
    l = 0

    wi = w_in[l]
    n_gate = 2 * N_DIR * N_HEADS
    g0 = QKV_WIDTH + QK_WIDTH
    w_cat = jnp.concatenate([wi[:, 0:g0], wi[:, g0 + n_gate:], wi[:, g0:g0 + n_gate],
                             jnp.zeros((d, LANES - n_gate), F32)], axis=1).astype(BF16)
    w_out_b16 = w_out[l].astype(BF16)
    w_out_a, w_out_b = w_out_b16[0:QK_WIDTH], w_out_b16[QK_WIDTH:]
    w_g = w_gu[l][:, 0:D_FF].astype(BF16)
    w_u = w_gu[l][:, D_FF:].astype(BF16)
    w_dn = w_down[l].astype(BF16)

    cond = jnp.concatenate([c_ctx[None, :], c, jnp.zeros((16 - 1 - bs, d), F32)], axis=0)
    mod = _mod_call(cond, w_ada[l], b_ada[l])
    mod3 = mod.reshape(16, 1, 6 * d)

    cc, sc = _dft_tables(HEAD_DIM)
    wy_p, wy_g = _fno_w_call(w_fno[l], jnp.asarray(cc), jnp.asarray(sc))

    xp = x_prompt.reshape(1, bp * np_, d)
    qkv, z, f, ab = _inproj_call(xp, mod3, 0, g_mix[l], w_cat, 512)
    qkv = qkv.reshape(bp, np_, QKV_WIDTH)
    z = z.reshape(bp, np_, QK_WIDTH)
    f = f.reshape(bp, np_, FOURIER_WIDTH)
    ab = ab.reshape(bp, np_, LANES)
    og, new_f, new_b = _delta_call(qkv, z, ab, _chunk_transposed(ab), w_conv[l], a_log[l], dt_bias[l],
                                   g_o[l], None, None)
    fo = _fno_prompt_call(f, wy_p)
    y_prompt = _ffn_call(xp, og.reshape(1, bp * np_, QK_WIDTH), fo.reshape(1, bp * np_, FOURIER_WIDTH),
                         mod3, 0, w_out_a, w_out_b, g_ffn[l], w_g, w_u, w_dn, g_final, 512)
    y_prompt = y_prompt.reshape(bp, np_, d)

    qkv, z, f, ab = _inproj_call(x_sample, mod3, 1, g_mix[l], w_cat, 512)
    og, _, _ = _delta_call(qkv, z, ab, _chunk_transposed(ab), w_conv[l], a_log[l], dt_bias[l], g_o[l],
                           state_dn_fwd[:, l:l + 1], state_dn_bwd[:, l:l + 1])
    fo = _fno_grid_call(f, wy_g)
    y_sample = _ffn_call(x_sample, og, fo, mod3, 1, w_out_a, w_out_b, g_ffn[l], w_g, w_u, w_dn, g_final, 512)

    return (y_prompt, y_sample, new_f, new_b)
```

```python
import functools

import numpy as np
import jax
import jax.numpy as jnp
from jax import lax
from jax.experimental import pallas as pl
from jax.experimental.pallas import tpu as pltpu

D_MODEL = 1024
N_HEADS = 4
HEAD_DIM = 128
QK_WIDTH = N_HEADS * HEAD_DIM
QKV_WIDTH = 3 * QK_WIDTH
N_GROUPS = 4
FOURIER_WIDTH = N_GROUPS * HEAD_DIM
N_DIR = 2
GRID_W = 64
CHUNK = 64
D_FF = 2816
RMS_EPS = 1e-6
LANES = 128
W_CAT_COLS = QKV_WIDTH + QK_WIDTH + FOURIER_WIDTH + LANES
VMEM_LIMIT = 56 * 1024 * 1024

F32 = jnp.float32
BF16 = jnp.bfloat16
HIGHEST = lax.Precision.HIGHEST


def _dot(a, b):
    return jnp.dot(a, b, preferred_element_type=F32)


def _silu(x):
    return x * jax.nn.sigmoid(x)


def _softplus(x):
    return jnp.maximum(x, 0.0) + jnp.log1p(jnp.exp(-jnp.abs(x)))


def _params(**kw):
    return pltpu.CompilerParams(vmem_limit_bytes=VMEM_LIMIT, **kw)


def _mod_kernel(cond_ref, w_ref, b_ref, o_ref):
    s = _silu(cond_ref[...]).astype(BF16)
    o_ref[...] = _dot(s, w_ref[...].astype(BF16)) + b_ref[...]


def _mod_call(cond, w_ada, b_ada):
    rows, d = cond.shape
    cols = w_ada.shape[1]
    tn = 1536
    return pl.pallas_call(
        _mod_kernel,
        grid=(cols // tn,),
        in_specs=[pl.BlockSpec((rows, d), lambda j: (0, 0)),
                  pl.BlockSpec((d, tn), lambda j: (0, j)),
                  pl.BlockSpec((1, tn), lambda j: (0, j))],
        out_specs=pl.BlockSpec((rows, tn), lambda j: (0, j)),
        out_shape=jax.ShapeDtypeStruct((rows, cols), F32),
        compiler_params=_params(),
    )(cond, w_ada, b_ada.reshape(1, cols))


def _fno_w_kernel(w_ref, cc_ref, sc_ref, wp_ref, wg_ref):
    w = w_ref[...]
    cw = jnp.dot(cc_ref[...], w, precision=HIGHEST, preferred_element_type=F32)
    sw = jnp.dot(sc_ref[...], w, precision=HIGHEST, preferred_element_type=F32)
    wp_ref[...] = jnp.concatenate([cw, sw], axis=1).astype(BF16)
    wg_ref[...] = jnp.concatenate([cw, -sw, -sw, -cw], axis=1).astype(BF16)


def _fno_w_call(w_fno, cc, sc):
    g, c, _ = w_fno.shape
    return pl.pallas_call(
        _fno_w_kernel,
        grid=(g,),
        in_specs=[pl.BlockSpec((None, c, c), lambda i: (i, 0, 0)),
                  pl.BlockSpec((c, c), lambda i: (0, 0)),
                  pl.BlockSpec((c, c), lambda i: (0, 0))],
        out_specs=[pl.BlockSpec((None, c, 2 * c), lambda i: (i, 0, 0)),
                   pl.BlockSpec((None, c, 4 * c), lambda i: (i, 0, 0))],
        out_shape=[jax.ShapeDtypeStruct((g, c, 2 * c), BF16),
                   jax.ShapeDtypeStruct((g, c, 4 * c), BF16)],
        compiler_params=_params(),
    )(w_fno, cc, sc)


def _inproj_kernel(x_ref, mod_ref, g_ref, w_ref, qkv_ref, z_ref, f_ref, ab_ref):
    x = x_ref[...]
    y = x * lax.rsqrt(jnp.mean(x * x, axis=-1, keepdims=True) + RMS_EPS) * g_ref[...]
    shift1 = mod_ref[:, 0:D_MODEL]
    scale1 = mod_ref[:, D_MODEL:2 * D_MODEL]
    h = (y * (1.0 + scale1) + shift1).astype(BF16)
    c0, c1, c2 = QKV_WIDTH, QKV_WIDTH + QK_WIDTH, QKV_WIDTH + QK_WIDTH + FOURIER_WIDTH
    qkv_ref[...] = _dot(h, w_ref[:, 0:c0])
    z_ref[...] = _dot(h, w_ref[:, c0:c1]).astype(BF16)
    f_ref[...] = _dot(h, w_ref[:, c1:c2]).astype(BF16)
    ab_ref[...] = _dot(h, w_ref[:, c2:W_CAT_COLS])


def _inproj_call(x, mod3, mod_row0, g_mix, w_cat, tm):
    b, n, d = x.shape
    tok = lambda w: pl.BlockSpec((None, tm, w), lambda i, t: (i, t, 0))
    return pl.pallas_call(
        _inproj_kernel,
        grid=(b, n // tm),
        in_specs=[tok(d),
                  pl.BlockSpec((None, 1, 6 * d), lambda i, t: (mod_row0 + i, 0, 0)),
                  pl.BlockSpec((1, d), lambda i, t: (0, 0)),
                  pl.BlockSpec((d, W_CAT_COLS), lambda i, t: (0, 0))],
        out_specs=[tok(QKV_WIDTH), tok(QK_WIDTH), tok(FOURIER_WIDTH), tok(LANES)],
        out_shape=[jax.ShapeDtypeStruct((b, n, QKV_WIDTH), F32),
                   jax.ShapeDtypeStruct((b, n, QK_WIDTH), BF16),
                   jax.ShapeDtypeStruct((b, n, FOURIER_WIDTH), BF16),
                   jax.ShapeDtypeStruct((b, n, LANES), F32)],
        compiler_params=_params(),
    )(x, mod3, g_mix.reshape(1, d), w_cat)


DUP = 2
WIDE = DUP * CHUNK


def _tri_inverse(a_list, eye, level_masks):
    ds = [eye - jnp.where(level_masks[0], a, 0.0) for a in a_list]
    for mask in level_masks[1:]:
        ls = [jnp.where(mask, a, 0.0).astype(BF16) for a in a_list]
        dbs = [d.astype(BF16) for d in ds]
        ms = [_dot(db, l).astype(BF16) for db, l in zip(dbs, ls)]
        ds = [d - _dot(m, db) for d, m, db in zip(ds, ms, dbs)]
    return ds


def _delta_kernel(*refs, n, hp_n, unroll, has_state):
    if has_state:
        (gp_ref, q_ref, k_ref, v_ref, z_ref, ab_ref, abt_ref, wq_ref, wk_ref, wv_ref,
         go_ref, tric_ref, trir_ref, sf0_ref, sb0_ref, og_ref, sf_ref, sb_ref,
         lhs_sc, nst_sc, gls, osc) = refs
    else:
        (gp_ref, q_ref, k_ref, v_ref, z_ref, ab_ref, abt_ref, wq_ref, wk_ref, wv_ref,
         go_ref, tric_ref, trir_ref, og_ref, sf_ref, sb_ref,
         lhs_sc, nst_sc, gls, osc) = refs
    hg = pl.program_id(1)
    nc = n // CHUNK
    rblk = 256
    nblk = n // rblk

    rows = lax.broadcasted_iota(jnp.int32, (CHUNK, LANES), 0)
    ci = lax.broadcasted_iota(jnp.int32, (CHUNK, CHUNK), 0)
    cj = lax.broadcasted_iota(jnp.int32, (CHUNK, CHUNK), 1)
    eye = (ci == cj).astype(F32)
    level_masks = [((ci // (2 * b)) == (cj // (2 * b))) & ((ci // b) != (cj // b))
                   for b in (1, 2, 4, 8, 16, 32)]
    incl = (ci >= cj, ci <= cj)
    strict = (ci > cj, ci < cj)
    upper8 = lax.broadcasted_iota(jnp.int32, (8, LANES), 1) >= CHUNK
    neg_a_vec = -jnp.exp(gp_ref[0:1, :])
    dtb_vec = gp_ref[1:2, :]

    def head_cols(hp):
        return slice(hp * HEAD_DIM, (hp + 1) * HEAD_DIM)

    def conv_chunk(src_ref, w_ref, r0, hp):
        cols = head_cols(hp)
        x = src_ref[pl.ds(r0, CHUNK), cols]
        prev8 = src_ref[pl.ds(pl.multiple_of(jnp.maximum(r0 - 8, 0), 8), 8), cols]
        next8 = src_ref[pl.ds(pl.multiple_of(jnp.minimum(r0 + CHUNK, n - 8), 8), 8), cols]
        prow = jnp.where(r0 > 0, prev8[7:8, :], 0.0)
        nrow = jnp.where(r0 + CHUNK < n, next8[0:1, :], 0.0)
        xp = jnp.where(rows == 0, prow, pltpu.roll(x, 1, 0))
        xn = jnp.where(rows == CHUNK - 1, nrow, pltpu.roll(x, CHUNK - 1, 0))
        return _silu(w_ref[0:1, cols] * xp + w_ref[1:2, cols] * x + w_ref[2:3, cols] * xn)

    def l2n(x):
        return x * lax.rsqrt(jnp.sum(x * x, axis=-1, keepdims=True) + 1e-6)

    def split3_rows(x):
        hi = x.astype(BF16)
        r1 = x - hi.astype(F32)
        lo = r1.astype(BF16)
        lo2 = (r1 - lo.astype(F32)).astype(BF16)
        return jnp.concatenate([hi, lo, lo2, jnp.zeros_like(hi)], axis=0)

    def split3_lanes(x2):
        hi = x2.astype(BF16).astype(F32)
        r1 = x2 - hi
        r2 = r1 - r1.astype(BF16).astype(F32)
        return jnp.concatenate([jnp.where(upper8, r1, x2), jnp.where(upper8, 0.0, r2)], axis=1).astype(BF16)

    def select_col(x, col):
        lanes = lax.broadcasted_iota(jnp.int32, x.shape, 1)
        return jnp.sum(jnp.where(lanes == col, x, 0.0), axis=-1, keepdims=True)

    def chain_front(hp, c, d, q, k, v, kk, qk, cum, beta_all):
        col = d * N_HEADS + hg * hp_n + hp
        g_col = jnp.broadcast_to(select_col(cum[d], col), (CHUNK, LANES))
        beta = select_col(beta_all, 2 * N_HEADS + col)
        a_row2 = abt_ref[c, pl.ds(col, 1), :]
        neg_a = select_col(neg_a_vec, col)
        dtb = select_col(dtb_vec, col)
        la_row2 = neg_a * _softplus(a_row2 + dtb)
        g_row = _dot(split3_lanes(jnp.broadcast_to(la_row2, (8, WIDE))), trir_ref[d])[0:1, :]
        g_tot = g_col[CHUNK - 1:CHUNK, :] if d == 0 else g_col[0:1, :]
        decay = jnp.exp(jnp.where(incl[d], g_col[:, 0:CHUNK] - g_row, -jnp.inf))
        eg = jnp.exp(g_col)
        gls[hp, d, c] = jnp.broadcast_to(jnp.exp(g_tot), (8, LANES))
        a = jnp.where(strict[d], beta * decay * kk, 0.0)
        rhs = jnp.concatenate([beta * v, (beta * eg) * k], axis=1).astype(BF16)
        lhs2 = jnp.concatenate([(k * jnp.exp(g_tot - g_col)).T.astype(BF16), (qk * decay).astype(BF16)], axis=0)
        return a, rhs, lhs2, q * eg

    def prepare(i, carry):
        chains = []
        for hp in range(hp_n):
            for j in range(unroll):
                c = i * unroll + j
                r0 = pl.multiple_of(c * CHUNK, CHUNK)
                q = l2n(conv_chunk(q_ref, wq_ref, r0, hp)) * (HEAD_DIM ** -0.5)
                k = l2n(conv_chunk(k_ref, wk_ref, r0, hp))
                v = conv_chunk(v_ref, wv_ref, r0, hp)
                kb = k.astype(BF16)
                kq = lax.dot_general(jnp.concatenate([kb, q.astype(BF16)], axis=0), kb,
                                     (((1,), (1,)), ((), ())), preferred_element_type=F32)
                ab = ab_ref[pl.ds(r0, CHUNK), :]
                log_a = neg_a_vec * _softplus(ab + dtb_vec)
                beta_all = jax.nn.sigmoid(ab)
                cums = _dot(tric_ref[...], split3_rows(log_a))
                cum = (cums[0:CHUNK, :], cums[CHUNK:, :])
                for d in range(N_DIR):
                    chains.append((hp, j, c, d) + chain_front(hp, c, d, q, k, v, kq[0:CHUNK, :], kq[CHUNK:, :],
                                                              cum, beta_all))
        t_invs = _tri_inverse([ch[4] for ch in chains], eye, level_masks)
        o_local = {}
        for (hp, j, c, d, _, rhs, lhs2, qg), t_inv in zip(chains, t_invs):
            sol = _dot(t_inv.astype(BF16), rhs).astype(BF16)
            x = _dot(lhs2, sol)
            nst_sc[hp, d, c] = x[0:HEAD_DIM, 0:HEAD_DIM]
            lhs_sc[hp, d, c, 0:HEAD_DIM, :] = (-x[0:HEAD_DIM, HEAD_DIM:]).astype(BF16)
            lhs_sc[hp, d, c, HEAD_DIM:, :] = (qg - x[HEAD_DIM:, HEAD_DIM:]).astype(BF16)
            o_local[(hp, j, d)] = x[HEAD_DIM:, 0:HEAD_DIM]
        for hp in range(hp_n):
            for j in range(unroll):
                c = i * unroll + j
                r0 = pl.multiple_of(c * CHUNK, CHUNK)
                osc[hp, pl.ds(r0, CHUNK), :] = o_local[(hp, j, 0)] + o_local[(hp, j, 1)]
        return carry

    lax.fori_loop(0, nc // unroll, prepare, 0)

    def scan_step(hp, c, d, s):
        r0 = pl.multiple_of(c * CHUNK, CHUNK)
        r = _dot(lhs_sc[hp, d, c], s.astype(BF16))
        osc[hp, pl.ds(r0, CHUNK), :] += r[HEAD_DIM:, :]
        return gls[hp, d, c][0:1, :] * s + r[0:HEAD_DIM, :] + nst_sc[hp, d, c]

    def scan_body(i, carry):
        out = []
        for hp in range(hp_n):
            out.append(scan_step(hp, i, 0, carry[2 * hp]))
            out.append(scan_step(hp, nc - 1 - i, 1, carry[2 * hp + 1]))
        return tuple(out)

    init = []
    for hp in range(hp_n):
        if has_state:
            init += [sf0_ref[hp], sb0_ref[hp]]
        else:
            init += [jnp.zeros((HEAD_DIM, HEAD_DIM), F32), jnp.zeros((HEAD_DIM, HEAD_DIM), F32)]
    fin = lax.fori_loop(0, nc, scan_body, tuple(init))
    for hp in range(hp_n):
        sf_ref[hp] = fin[2 * hp]
        sb_ref[hp] = fin[2 * hp + 1]

    def post_body(i, carry):
        r0 = pl.multiple_of(i * rblk, rblk)
        for hp in range(hp_n):
            o = osc[hp, pl.ds(r0, rblk), :]
            y = o * lax.rsqrt(jnp.mean(o * o, axis=-1, keepdims=True) + RMS_EPS) * go_ref[...]
            zz = z_ref[pl.ds(r0, rblk), head_cols(hp)].astype(F32)
            og_ref[pl.ds(r0, rblk), head_cols(hp)] = (y * _silu(zz)).astype(BF16)
        return carry

    lax.fori_loop(0, nblk, post_body, 0)


def _tri_tables():
    i = np.arange(CHUNK)
    low = (i[:, None] >= i[None, :]).astype(np.float32)
    up = low.T
    zc = np.zeros((CHUNK, CHUNK), np.float32)
    tric = np.concatenate([np.concatenate([m, m, m, zc], axis=1) for m in (low, up)], axis=0)
    trir = np.stack([np.concatenate([m, m, m, zc], axis=0) for m in (up, low)])
    return jnp.asarray(tric).astype(BF16), jnp.asarray(trir).astype(BF16)


def _delta_call(qkv, z, ab, abt2, w_conv, a_log, dt_bias, g_o, s0_f, s0_b, *, heads_per_step, unroll):
    b, n, _ = qkv.shape
    nc = n // CHUNK
    hp_n = heads_per_step
    groups = N_HEADS // hp_n
    wide = hp_n * HEAD_DIM
    has_state = s0_f is not None
    tric, trir = _tri_tables()
    col = lambda off: pl.BlockSpec((None, n, wide), lambda i, g: (i, 0, off + g))
    wsp = lambda off: pl.BlockSpec((3, wide), lambda i, g: (0, off + g))
    st = pl.BlockSpec((None, None, hp_n, HEAD_DIM, HEAD_DIM), lambda i, g: (i, 0, g, 0, 0))
    n_gate = 2 * N_DIR * N_HEADS
    pad = jnp.zeros((LANES - N_DIR * N_HEADS,), F32)
    gate_params = jnp.stack([jnp.concatenate([a_log.reshape(-1), pad]),
                             jnp.concatenate([dt_bias.reshape(-1), pad])])
    in_specs = [pl.BlockSpec((2, LANES), lambda i, g: (0, 0)),
                col(0), col(groups), col(2 * groups), col(0),
                pl.BlockSpec((None, n, LANES), lambda i, g: (i, 0, 0)),
                pl.BlockSpec((None, nc, n_gate, WIDE), lambda i, g: (i, 0, 0, 0)),
                wsp(0), wsp(groups), wsp(2 * groups),
                pl.BlockSpec((1, HEAD_DIM), lambda i, g: (0, 0)),
                pl.BlockSpec((N_DIR * CHUNK, 4 * CHUNK), lambda i, g: (0, 0)),
                pl.BlockSpec((N_DIR, 4 * CHUNK, CHUNK), lambda i, g: (0, 0, 0))]
    args = [gate_params, qkv, qkv, qkv, z, ab, abt2, w_conv, w_conv, w_conv, g_o.reshape(1, HEAD_DIM),
            tric, trir]
    if has_state:
        in_specs += [st, st]
        args += [s0_f, s0_b]
    state_shape = jax.ShapeDtypeStruct((b, 1, N_HEADS, HEAD_DIM, HEAD_DIM), F32)
    return pl.pallas_call(
        functools.partial(_delta_kernel, n=n, hp_n=hp_n, unroll=unroll, has_state=has_state),
        grid=(b, groups),
        in_specs=in_specs,
        out_specs=[col(0), st, st],
        out_shape=[jax.ShapeDtypeStruct((b, n, QK_WIDTH), BF16), state_shape, state_shape],
        scratch_shapes=[pltpu.VMEM((hp_n, N_DIR, nc, HEAD_DIM + CHUNK, HEAD_DIM), BF16),
                        pltpu.VMEM((hp_n, N_DIR, nc, HEAD_DIM, HEAD_DIM), F32),
                        pltpu.VMEM((hp_n, N_DIR, nc, 8, LANES), F32),
                        pltpu.VMEM((hp_n, n, HEAD_DIM), F32)],
        compiler_params=_params(),
    )(*args)


def _dft_tables(n, scale=1.0):
    idx = np.arange(n)
    ang = 2.0 * np.pi * ((idx[:, None] * idx[None, :]) % n) / n
    return (np.cos(ang) * scale).astype(np.float32), (np.sin(ang) * scale).astype(np.float32)


def _fno_prompt_kernel(f_ref, wy_ref, cn_ref, sn_ref, o_ref):
    cn = cn_ref[...]
    sn = sn_ref[...]
    for g in range(N_GROUPS):
        y = _dot(f_ref[:, g * HEAD_DIM:(g + 1) * HEAD_DIM], wy_ref[g])
        o = _dot(cn, y[:, 0:HEAD_DIM].astype(BF16)) + _dot(sn, y[:, HEAD_DIM:].astype(BF16))
        o_ref[:, g * HEAD_DIM:(g + 1) * HEAD_DIM] = o.astype(BF16)


def _fno_prompt_call(f, wy):
    b, n, w = f.shape
    cn, sn = _dft_tables(n, (n * HEAD_DIM) ** -0.5)
    cn = jnp.asarray(cn).astype(BF16)
    sn_neg = jnp.asarray(-sn).astype(BF16)
    return pl.pallas_call(
        _fno_prompt_kernel,
        grid=(b,),
        in_specs=[pl.BlockSpec((None, n, w), lambda i: (i, 0, 0)),
                  pl.BlockSpec((N_GROUPS, HEAD_DIM, 2 * HEAD_DIM), lambda i: (0, 0, 0)),
                  pl.BlockSpec((n, n), lambda i: (0, 0)),
                  pl.BlockSpec((n, n), lambda i: (0, 0))],
        out_specs=pl.BlockSpec((None, n, w), lambda i: (i, 0, 0)),
        out_shape=jax.ShapeDtypeStruct((b, n, w), BF16),
        compiler_params=_params(),
    )(f, wy, cn, sn_neg)


COL_UNROLL = 2
ROW_UNROLL = 8


def _fno_grid_kernel(f_ref, wy_ref, bdc_ref, bds_ref, crs_ref, o_ref, zr_sc, zi_sc, o_sc, *, n):
    tb = bdc_ref.shape[0]
    two = 2 * HEAD_DIM
    rows = n // GRID_W

    def col_body(i, carry):
        r0s = [pl.multiple_of((i * COL_UNROLL + j) * tb, tb) for j in range(COL_UNROLL)]
        ys = [_dot(f_ref[pl.ds(r0, tb), :], wy_ref[...]).astype(BF16) for r0 in r0s]
        zs = [_dot(bdc_ref[...], y[:, 0:two]) + _dot(bds_ref[...], y[:, two:]) for y in ys]
        for r0, z in zip(r0s, zs):
            zr_sc[pl.ds(r0, tb), :] = z[:, 0:HEAD_DIM]
            zi_sc[pl.ds(r0, tb), :] = z[:, HEAD_DIM:]
        return carry

    lax.fori_loop(0, n // (tb * COL_UNROLL), col_body, 0)

    def row_body(i, carry):
        cols = [pl.ds(i * ROW_UNROLL + j, rows, stride=GRID_W) for j in range(ROW_UNROLL)]
        zs = [jnp.concatenate([zr_sc[col, :], zi_sc[col, :]], axis=0).astype(BF16) for col in cols]
        outs = [_dot(crs_ref[...], z) for z in zs]
        for col, o in zip(cols, outs):
            o_sc[col, :] = o
        return carry

    lax.fori_loop(0, GRID_W // ROW_UNROLL, row_body, 0)
    o_ref[...] = o_sc[...].astype(BF16)


def _fno_grid_call(f, wy4):
    b, n, w = f.shape
    rows = n // GRID_W
    tb = 256
    cw, sw = _dft_tables(GRID_W)
    rep = np.eye(tb // GRID_W, dtype=np.float32)
    bdc = jnp.asarray(np.kron(rep, cw)).astype(BF16)
    bds = jnp.asarray(np.kron(rep, sw)).astype(BF16)
    cr, sr = _dft_tables(rows, (n * HEAD_DIM) ** -0.5)
    crs = jnp.asarray(np.concatenate([cr, sr], axis=1)).astype(BF16)
    const = lambda s: pl.BlockSpec(s, lambda i, g: (0,) * len(s))
    return pl.pallas_call(
        functools.partial(_fno_grid_kernel, n=n),
        grid=(b, N_GROUPS),
        in_specs=[pl.BlockSpec((None, n, HEAD_DIM), lambda i, g: (i, 0, g)),
                  pl.BlockSpec((None, HEAD_DIM, 4 * HEAD_DIM), lambda i, g: (g, 0, 0)),
                  const((tb, tb)), const((tb, tb)), const((rows, 2 * rows))],
        out_specs=pl.BlockSpec((None, n, HEAD_DIM), lambda i, g: (i, 0, g)),
        out_shape=jax.ShapeDtypeStruct((b, n, w), BF16),
        scratch_shapes=[pltpu.VMEM((n, HEAD_DIM), F32)] * 3,
        compiler_params=_params(),
    )(f, wy4, bdc, bds, crs)


FF_BLOCK = 256


def _ffn_kernel(x_ref, og_ref, fo_ref, mod_ref, woa_ref, wob_ref, gffn_ref, wg_ref, wu_ref, wd_ref,
                gfin_ref, y_ref):
    d = D_MODEL
    gate1 = mod_ref[:, 2 * d:3 * d]
    shift2 = mod_ref[:, 3 * d:4 * d]
    scale2 = mod_ref[:, 4 * d:5 * d]
    gate2 = mod_ref[:, 5 * d:6 * d]
    mo = _dot(og_ref[...], woa_ref[...]) + _dot(fo_ref[...], wob_ref[...])
    x1 = x_ref[...] + gate1 * mo
    hn = x1 * lax.rsqrt(jnp.mean(x1 * x1, axis=-1, keepdims=True) + RMS_EPS) * gffn_ref[...]
    h2 = (hn * (1.0 + scale2) + shift2).astype(BF16)
    acc = jnp.zeros(x1.shape, F32)
    for j in range(D_FF // FF_BLOCK):
        sl = slice(j * FF_BLOCK, (j + 1) * FF_BLOCK)
        gt = _dot(h2, wg_ref[:, sl])
        up = _dot(h2, wu_ref[:, sl])
        acc = acc + _dot((_silu(gt) * up).astype(BF16), wd_ref[sl, :])
    x2 = x1 + gate2 * acc
    y_ref[...] = x2 * lax.rsqrt(jnp.mean(x2 * x2, axis=-1, keepdims=True) + RMS_EPS) * gfin_ref[...]


def _ffn_call(x, og, fo, mod3, mod_row0, w_out_a, w_out_b, g_ffn, w_g, w_u, w_down, g_final, tm):
    b, n, d = x.shape
    tok = lambda w: pl.BlockSpec((None, tm, w), lambda i, t: (i, t, 0))
    const = lambda s: pl.BlockSpec(s, lambda i, t: (0,) * len(s), pipeline_mode=pl.Buffered(1))
    return pl.pallas_call(
        _ffn_kernel,
        grid=(b, n // tm),
        in_specs=[tok(d), tok(QK_WIDTH), tok(FOURIER_WIDTH),
                  pl.BlockSpec((None, 1, 6 * d), lambda i, t: (mod_row0 + i, 0, 0)),
                  const((QK_WIDTH, d)), const((FOURIER_WIDTH, d)), const((1, d)),
                  const((d, D_FF)), const((d, D_FF)), const((D_FF, d)), const((1, d))],
        out_specs=tok(d),
        out_shape=jax.ShapeDtypeStruct((b, n, d), F32),
        compiler_params=_params(),
    )(x, og, fo, mod3, w_out_a, w_out_b, g_ffn.reshape(1, d), w_g, w_u, w_down, g_final.reshape(1, d))


def _chunk_transposed(ab):
    b, n, _ = ab.shape
    g = ab[:, :, 0:2 * N_DIR * N_HEADS].reshape(b, n // CHUNK, CHUNK, 2 * N_DIR * N_HEADS)
    return jnp.tile(jnp.swapaxes(g, 2, 3), (1, 1, 1, DUP))


def kernel(x_prompt, x_sample, c, state_dn_fwd, state_dn_bwd, c_ctx, w_ada, b_ada, g_mix, w_in, w_conv,
           a_log, dt_bias, g_o, w_fno, w_out, g_ffn, w_gu, w_down, g_final):
    d = D_MODEL
    bp, np_, _ = x_prompt.shape
    bs, ns, _ = x_sample.shape
    l = 0

    wi = w_in[l]
    n_gate = 2 * N_DIR * N_HEADS
    g0 = QKV_WIDTH + QK_WIDTH
    w_cat = jnp.concatenate([wi[:, 0:g0], wi[:, g0 + n_gate:], wi[:, g0:g0 + n_gate],
                             jnp.zeros((d, LANES - n_gate), F32)], axis=1).astype(BF16)
    w_out_b16 = w_out[l].astype(BF16)
    w_out_a, w_out_b = w_out_b16[0:QK_WIDTH], w_out_b16[QK_WIDTH:]
    w_g = w_gu[l][:, 0:D_FF].astype(BF16)
    w_u = w_gu[l][:, D_FF:].astype(BF16)
    w_dn = w_down[l].astype(BF16)

    cond = jnp.concatenate([c_ctx[None, :], c, jnp.zeros((16 - 1 - bs, d), F32)], axis=0)
    mod = _mod_call(cond, w_ada[l], b_ada[l])
    mod3 = mod.reshape(16, 1, 6 * d)

    cc, sc = _dft_tables(HEAD_DIM)
    wy_p, wy_g = _fno_w_call(w_fno[l], jnp.asarray(cc), jnp.asarray(sc))

    xp = x_prompt.reshape(1, bp * np_, d)
    qkv, z, f, ab = _inproj_call(xp, mod3, 0, g_mix[l], w_cat, 512)
    qkv = qkv.reshape(bp, np_, QKV_WIDTH)
    z = z.reshape(bp, np_, QK_WIDTH)
    f = f.reshape(bp, np_, FOURIER_WIDTH)
    ab = ab.reshape(bp, np_, LANES)
    og, new_f, new_b = _delta_call(qkv, z, ab, _chunk_transposed(ab), w_conv[l], a_log[l], dt_bias[l],
                                   g_o[l], None, None, heads_per_step=2, unroll=4)
    fo = _fno_prompt_call(f, wy_p)
    y_prompt = _ffn_call(xp, og.reshape(1, bp * np_, QK_WIDTH), fo.reshape(1, bp * np_, FOURIER_WIDTH),
                         mod3, 0, w_out_a, w_out_b, g_ffn[l], w_g, w_u, w_dn, g_final, 512)
    y_prompt = y_prompt.reshape(bp, np_, d)

    qkv, z, f, ab = _inproj_call(x_sample, mod3, 1, g_mix[l], w_cat, 512)
    og, _, _ = _delta_call(qkv, z, ab, _chunk_transposed(ab), w_conv[l], a_log[l], dt_bias[l], g_o[l],
                           state_dn_fwd[:, l:l + 1], state_dn_bwd[:, l:l + 1], heads_per_step=1, unroll=8)
    fo = _fno_grid_call(f, wy_g)
    y_sample = _ffn_call(x_sample, og, fo, mod3, 1, w_out_a, w_out_b, g_ffn[l], w_g, w_u, w_dn, g_final, 512)

    return (y_prompt, y_sample, new_f, new_b)
```

```python
import functools

import numpy as np
import jax
import jax.numpy as jnp
from jax import lax
from jax.experimental import pallas as pl
from jax.experimental.pallas import tpu as pltpu

D_MODEL = 1024
N_HEADS = 4
HEAD_DIM = 128
QK_WIDTH = N_HEADS * HEAD_DIM
QKV_WIDTH = 3 * QK_WIDTH
N_GROUPS = 4
FOURIER_WIDTH = N_GROUPS * HEAD_DIM
N_DIR = 2
GRID_W = 64
CHUNK = 128
D_FF = 2816
RMS_EPS = 1e-6
LANES = 128
W_CAT_COLS = QKV_WIDTH + QK_WIDTH + FOURIER_WIDTH + LANES
VMEM_LIMIT = 56 * 1024 * 1024

F32 = jnp.float32
BF16 = jnp.bfloat16
HIGHEST = lax.Precision.HIGHEST


def _dot(a, b):
    return jnp.dot(a, b, preferred_element_type=F32)


def _silu(x):
    return x * jax.nn.sigmoid(x)


def _softplus(x):
    return jnp.maximum(x, 0.0) + jnp.log1p(jnp.exp(-jnp.abs(x)))


def _params(**kw):
    return pltpu.CompilerParams(vmem_limit_bytes=VMEM_LIMIT, **kw)


def _mod_kernel(cond_ref, w_ref, b_ref, o_ref):
    s = _silu(cond_ref[...]).astype(BF16)
    o_ref[...] = _dot(s, w_ref[...].astype(BF16)) + b_ref[...]


def _mod_call(cond, w_ada, b_ada):
    rows, d = cond.shape
    cols = w_ada.shape[1]
    tn = 1536
    return pl.pallas_call(
        _mod_kernel,
        grid=(cols // tn,),
        in_specs=[pl.BlockSpec((rows, d), lambda j: (0, 0)),
                  pl.BlockSpec((d, tn), lambda j: (0, j)),
                  pl.BlockSpec((1, tn), lambda j: (0, j))],
        out_specs=pl.BlockSpec((rows, tn), lambda j: (0, j)),
        out_shape=jax.ShapeDtypeStruct((rows, cols), F32),
        compiler_params=_params(),
    )(cond, w_ada, b_ada.reshape(1, cols))


def _fno_w_kernel(w_ref, cc_ref, sc_ref, wp_ref, wg_ref):
    w = w_ref[...]
    cw = jnp.dot(cc_ref[...], w, precision=HIGHEST, preferred_element_type=F32)
    sw = jnp.dot(sc_ref[...], w, precision=HIGHEST, preferred_element_type=F32)
    wp_ref[...] = jnp.concatenate([cw, sw], axis=1).astype(BF16)
    wg_ref[...] = jnp.concatenate([cw, -sw, -sw, -cw], axis=1).astype(BF16)


def _fno_w_call(w_fno, cc, sc):
    g, c, _ = w_fno.shape
    return pl.pallas_call(
        _fno_w_kernel,
        grid=(g,),
        in_specs=[pl.BlockSpec((None, c, c), lambda i: (i, 0, 0)),
                  pl.BlockSpec((c, c), lambda i: (0, 0)),
                  pl.BlockSpec((c, c), lambda i: (0, 0))],
        out_specs=[pl.BlockSpec((None, c, 2 * c), lambda i: (i, 0, 0)),
                   pl.BlockSpec((None, c, 4 * c), lambda i: (i, 0, 0))],
        out_shape=[jax.ShapeDtypeStruct((g, c, 2 * c), BF16),
                   jax.ShapeDtypeStruct((g, c, 4 * c), BF16)],
        compiler_params=_params(),
    )(w_fno, cc, sc)


def _inproj_kernel(x_ref, mod_ref, g_ref, w_ref, qkv_ref, z_ref, f_ref, ab_ref):
    x = x_ref[...]
    y = x * lax.rsqrt(jnp.mean(x * x, axis=-1, keepdims=True) + RMS_EPS) * g_ref[...]
    shift1 = mod_ref[:, 0:D_MODEL]
    scale1 = mod_ref[:, D_MODEL:2 * D_MODEL]
    h = (y * (1.0 + scale1) + shift1).astype(BF16)
    c0, c1, c2 = QKV_WIDTH, QKV_WIDTH + QK_WIDTH, QKV_WIDTH + QK_WIDTH + FOURIER_WIDTH
    qkv_ref[...] = _dot(h, w_ref[:, 0:c0])
    z_ref[...] = _dot(h, w_ref[:, c0:c1]).astype(BF16)
    f_ref[...] = _dot(h, w_ref[:, c1:c2]).astype(BF16)
    ab_ref[...] = _dot(h, w_ref[:, c2:W_CAT_COLS])


def _inproj_call(x, mod3, mod_row0, g_mix, w_cat, tm):
    b, n, d = x.shape
    tok = lambda w: pl.BlockSpec((None, tm, w), lambda i, t: (i, t, 0))
    return pl.pallas_call(
        _inproj_kernel,
        grid=(b, n // tm),
        in_specs=[tok(d),
                  pl.BlockSpec((None, 1, 6 * d), lambda i, t: (mod_row0 + i, 0, 0)),
                  pl.BlockSpec((1, d), lambda i, t: (0, 0)),
                  pl.BlockSpec((d, W_CAT_COLS), lambda i, t: (0, 0))],
        out_specs=[tok(QKV_WIDTH), tok(QK_WIDTH), tok(FOURIER_WIDTH), tok(LANES)],
        out_shape=[jax.ShapeDtypeStruct((b, n, QKV_WIDTH), F32),
                   jax.ShapeDtypeStruct((b, n, QK_WIDTH), BF16),
                   jax.ShapeDtypeStruct((b, n, FOURIER_WIDTH), BF16),
                   jax.ShapeDtypeStruct((b, n, LANES), F32)],
        compiler_params=_params(),
    )(x, mod3, g_mix.reshape(1, d), w_cat)


LEVEL_BLOCKS = tuple(2 ** e for e in range(int(np.log2(CHUNK))))


def _tri_inverse(a_list, eye, level_masks):
    ds = [eye - jnp.where(level_masks[0], a, 0.0) for a in a_list]
    for mask in level_masks[1:]:
        ls = [jnp.where(mask, a, 0.0).astype(BF16) for a in a_list]
        dbs = [d.astype(BF16) for d in ds]
        ms = [_dot(db, l).astype(BF16) for db, l in zip(dbs, ls)]
        ds = [d - _dot(m, db) for d, m, db in zip(ds, ms, dbs)]
    return ds


def _delta_kernel(*refs, n, hp_n, unroll, has_state):
    if has_state:
        (gp_ref, q_ref, k_ref, v_ref, z_ref, ab_ref, abt_ref, wq_ref, wk_ref, wv_ref,
         go_ref, tric_ref, trir_ref, sf0_ref, sb0_ref, og_ref, sf_ref, sb_ref,
         lhs_sc, nst_sc, gls, osc) = refs
    else:
        (gp_ref, q_ref, k_ref, v_ref, z_ref, ab_ref, abt_ref, wq_ref, wk_ref, wv_ref,
         go_ref, tric_ref, trir_ref, og_ref, sf_ref, sb_ref,
         lhs_sc, nst_sc, gls, osc) = refs
    hg = pl.program_id(1)
    nc = n // CHUNK
    rblk = 256
    nblk = n // rblk

    rows = lax.broadcasted_iota(jnp.int32, (CHUNK, LANES), 0)
    ci = lax.broadcasted_iota(jnp.int32, (CHUNK, CHUNK), 0)
    cj = lax.broadcasted_iota(jnp.int32, (CHUNK, CHUNK), 1)
    eye = (ci == cj).astype(F32)
    level_masks = [((ci // (2 * b)) == (cj // (2 * b))) & ((ci // b) != (cj // b))
                   for b in LEVEL_BLOCKS]
    incl = (ci >= cj, ci <= cj)
    strict = (ci > cj, ci < cj)
    neg_a_vec = -jnp.exp(gp_ref[0:1, :])
    dtb_vec = gp_ref[1:2, :]

    def head_cols(hp):
        return slice(hp * HEAD_DIM, (hp + 1) * HEAD_DIM)

    def conv_chunk(src_ref, w_ref, r0, hp):
        cols = head_cols(hp)
        x = src_ref[pl.ds(r0, CHUNK), cols]
        prev8 = src_ref[pl.ds(pl.multiple_of(jnp.maximum(r0 - 8, 0), 8), 8), cols]
        next8 = src_ref[pl.ds(pl.multiple_of(jnp.minimum(r0 + CHUNK, n - 8), 8), 8), cols]
        prow = jnp.where(r0 > 0, prev8[7:8, :], 0.0)
        nrow = jnp.where(r0 + CHUNK < n, next8[0:1, :], 0.0)
        xp = jnp.where(rows == 0, prow, pltpu.roll(x, 1, 0))
        xn = jnp.where(rows == CHUNK - 1, nrow, pltpu.roll(x, CHUNK - 1, 0))
        return _silu(w_ref[0:1, cols] * xp + w_ref[1:2, cols] * x + w_ref[2:3, cols] * xn)

    def l2n(x):
        return x * lax.rsqrt(jnp.sum(x * x, axis=-1, keepdims=True) + 1e-6)

    def split3_rows(x):
        hi = x.astype(BF16)
        r1 = x - hi.astype(F32)
        lo = r1.astype(BF16)
        lo2 = (r1 - lo.astype(F32)).astype(BF16)
        return jnp.concatenate([hi, lo, lo2], axis=0)

    def split3_lanes(x):
        hi = x.astype(BF16)
        r1 = x - hi.astype(F32)
        lo = r1.astype(BF16)
        lo2 = (r1 - lo.astype(F32)).astype(BF16)
        return jnp.concatenate([hi, lo, lo2], axis=1)

    def select_col(x, col):
        lanes = lax.broadcasted_iota(jnp.int32, x.shape, 1)
        return jnp.sum(jnp.where(lanes == col, x, 0.0), axis=-1, keepdims=True)

    def chain_front(hp, c, d, q, k, v, kk, qk, cum, beta_all):
        col = d * N_HEADS + hg * hp_n + hp
        g_col = jnp.broadcast_to(select_col(cum[d], col), (CHUNK, LANES))
        beta = select_col(beta_all, 2 * N_HEADS + col)
        a_row = abt_ref[c, pl.ds(col, 1), :]
        neg_a = select_col(neg_a_vec, col)
        dtb = select_col(dtb_vec, col)
        la_row = neg_a * _softplus(a_row + dtb)
        g_row = _dot(split3_lanes(jnp.broadcast_to(la_row, (8, CHUNK))), trir_ref[d])[0:1, :]
        g_tot = g_col[CHUNK - 1:CHUNK, :] if d == 0 else g_col[0:1, :]
        decay = jnp.exp(jnp.where(incl[d], g_col - g_row, -jnp.inf))
        eg = jnp.exp(g_col)
        gls[hp, d, c] = jnp.broadcast_to(jnp.exp(g_tot), (8, LANES))
        a = jnp.where(strict[d], beta * decay * kk, 0.0)
        rhs = jnp.concatenate([beta * v, (beta * eg) * k], axis=1).astype(BF16)
        lhs2 = jnp.concatenate([(k * jnp.exp(g_tot - g_col)).T.astype(BF16), (qk * decay).astype(BF16)], axis=0)
        return a, rhs, lhs2, q * eg

    def prepare(i, carry):
        chains = []
        for hp in range(hp_n):
            for j in range(unroll):
                c = i * unroll + j
                r0 = pl.multiple_of(c * CHUNK, CHUNK)
                q = l2n(conv_chunk(q_ref, wq_ref, r0, hp)) * (HEAD_DIM ** -0.5)
                k = l2n(conv_chunk(k_ref, wk_ref, r0, hp))
                v = conv_chunk(v_ref, wv_ref, r0, hp)
                kb = k.astype(BF16)
                kq = lax.dot_general(jnp.concatenate([kb, q.astype(BF16)], axis=0), kb,
                                     (((1,), (1,)), ((), ())), preferred_element_type=F32)
                ab = ab_ref[pl.ds(r0, CHUNK), :]
                log_a = neg_a_vec * _softplus(ab + dtb_vec)
                beta_all = jax.nn.sigmoid(ab)
                cums = _dot(tric_ref[...], split3_rows(log_a))
                cum = (cums[0:CHUNK, :], cums[CHUNK:, :])
                for d in range(N_DIR):
                    chains.append((hp, j, c, d) + chain_front(hp, c, d, q, k, v, kq[0:CHUNK, :], kq[CHUNK:, :],
                                                              cum, beta_all))
        t_invs = _tri_inverse([ch[4] for ch in chains], eye, level_masks)
        o_local = {}
        for (hp, j, c, d, _, rhs, lhs2, qg), t_inv in zip(chains, t_invs):
            sol = _dot(t_inv.astype(BF16), rhs).astype(BF16)
            x = _dot(lhs2, sol)
            nst_sc[hp, d, c] = x[0:HEAD_DIM, 0:HEAD_DIM]
            lhs_sc[hp, d, c, 0:HEAD_DIM, :] = (-x[0:HEAD_DIM, HEAD_DIM:]).astype(BF16)
            lhs_sc[hp, d, c, HEAD_DIM:, :] = (qg - x[HEAD_DIM:, HEAD_DIM:]).astype(BF16)
            o_local[(hp, j, d)] = x[HEAD_DIM:, 0:HEAD_DIM]
        for hp in range(hp_n):
            for j in range(unroll):
                c = i * unroll + j
                r0 = pl.multiple_of(c * CHUNK, CHUNK)
                osc[hp, pl.ds(r0, CHUNK), :] = o_local[(hp, j, 0)] + o_local[(hp, j, 1)]
        return carry

    lax.fori_loop(0, nc // unroll, prepare, 0)

    def scan_step(hp, c, d, s):
        r0 = pl.multiple_of(c * CHUNK, CHUNK)
        r = _dot(lhs_sc[hp, d, c], s.astype(BF16))
        osc[hp, pl.ds(r0, CHUNK), :] += r[HEAD_DIM:, :]
        return gls[hp, d, c][0:1, :] * s + r[0:HEAD_DIM, :] + nst_sc[hp, d, c]

    def scan_body(i, carry):
        out = []
        for hp in range(hp_n):
            out.append(scan_step(hp, i, 0, carry[2 * hp]))
            out.append(scan_step(hp, nc - 1 - i, 1, carry[2 * hp + 1]))
        return tuple(out)

    init = []
    for hp in range(hp_n):
        if has_state:
            init += [sf0_ref[hp], sb0_ref[hp]]
        else:
            init += [jnp.zeros((HEAD_DIM, HEAD_DIM), F32), jnp.zeros((HEAD_DIM, HEAD_DIM), F32)]
    fin = lax.fori_loop(0, nc, scan_body, tuple(init))
    for hp in range(hp_n):
        sf_ref[hp] = fin[2 * hp]
        sb_ref[hp] = fin[2 * hp + 1]

    def post_body(i, carry):
        r0 = pl.multiple_of(i * rblk, rblk)
        for hp in range(hp_n):
            o = osc[hp, pl.ds(r0, rblk), :]
            y = o * lax.rsqrt(jnp.mean(o * o, axis=-1, keepdims=True) + RMS_EPS) * go_ref[...]
            zz = z_ref[pl.ds(r0, rblk), head_cols(hp)].astype(F32)
            og_ref[pl.ds(r0, rblk), head_cols(hp)] = (y * _silu(zz)).astype(BF16)
        return carry

    lax.fori_loop(0, nblk, post_body, 0)


def _tri_tables():
    i = np.arange(CHUNK)
    low = (i[:, None] >= i[None, :]).astype(np.float32)
    up = low.T
    tric = np.concatenate([np.concatenate([m, m, m], axis=1) for m in (low, up)], axis=0)
    trir = np.stack([np.concatenate([m, m, m], axis=0) for m in (up, low)])
    return jnp.asarray(tric).astype(BF16), jnp.asarray(trir).astype(BF16)


def _delta_call(qkv, z, ab, abt, w_conv, a_log, dt_bias, g_o, s0_f, s0_b, *, heads_per_step, unroll):
    b, n, _ = qkv.shape
    nc = n // CHUNK
    hp_n = heads_per_step
    groups = N_HEADS // hp_n
    wide = hp_n * HEAD_DIM
    has_state = s0_f is not None
    tric, trir = _tri_tables()
    col = lambda off: pl.BlockSpec((None, n, wide), lambda i, g: (i, 0, off + g))
    wsp = lambda off: pl.BlockSpec((3, wide), lambda i, g: (0, off + g))
    st = pl.BlockSpec((None, None, hp_n, HEAD_DIM, HEAD_DIM), lambda i, g: (i, 0, g, 0, 0))
    n_gate = 2 * N_DIR * N_HEADS
    pad = jnp.zeros((LANES - N_DIR * N_HEADS,), F32)
    gate_params = jnp.stack([jnp.concatenate([a_log.reshape(-1), pad]),
                             jnp.concatenate([dt_bias.reshape(-1), pad])])
    in_specs = [pl.BlockSpec((2, LANES), lambda i, g: (0, 0)),
                col(0), col(groups), col(2 * groups), col(0),
                pl.BlockSpec((None, n, LANES), lambda i, g: (i, 0, 0)),
                pl.BlockSpec((None, nc, n_gate, CHUNK), lambda i, g: (i, 0, 0, 0)),
                wsp(0), wsp(groups), wsp(2 * groups),
                pl.BlockSpec((1, HEAD_DIM), lambda i, g: (0, 0)),
                pl.BlockSpec((N_DIR * CHUNK, 3 * CHUNK), lambda i, g: (0, 0)),
                pl.BlockSpec((N_DIR, 3 * CHUNK, CHUNK), lambda i, g: (0, 0, 0))]
    args = [gate_params, qkv, qkv, qkv, z, ab, abt, w_conv, w_conv, w_conv, g_o.reshape(1, HEAD_DIM),
            tric, trir]
    if has_state:
        in_specs += [st, st]
        args += [s0_f, s0_b]
    state_shape = jax.ShapeDtypeStruct((b, 1, N_HEADS, HEAD_DIM, HEAD_DIM), F32)
    return pl.pallas_call(
        functools.partial(_delta_kernel, n=n, hp_n=hp_n, unroll=unroll, has_state=has_state),
        grid=(b, groups),
        in_specs=in_specs,
        out_specs=[col(0), st, st],
        out_shape=[jax.ShapeDtypeStruct((b, n, QK_WIDTH), BF16), state_shape, state_shape],
        scratch_shapes=[pltpu.VMEM((hp_n, N_DIR, nc, HEAD_DIM + CHUNK, HEAD_DIM), BF16),
                        pltpu.VMEM((hp_n, N_DIR, nc, HEAD_DIM, HEAD_DIM), F32),
                        pltpu.VMEM((hp_n, N_DIR, nc, 8, LANES), F32),
                        pltpu.VMEM((hp_n, n, HEAD_DIM), F32)],
        compiler_params=_params(),
    )(*args)


def _dft_tables(n, scale=1.0):
    idx = np.arange(n)
    ang = 2.0 * np.pi * ((idx[:, None] * idx[None, :]) % n) / n
    return (np.cos(ang) * scale).astype(np.float32), (np.sin(ang) * scale).astype(np.float32)


def _fno_prompt_kernel(f_ref, wy_ref, cn_ref, sn_ref, o_ref):
    cn = cn_ref[...]
    sn = sn_ref[...]
    for g in range(N_GROUPS):
        y = _dot(f_ref[:, g * HEAD_DIM:(g + 1) * HEAD_DIM], wy_ref[g])
        o = _dot(cn, y[:, 0:HEAD_DIM].astype(BF16)) + _dot(sn, y[:, HEAD_DIM:].astype(BF16))
        o_ref[:, g * HEAD_DIM:(g + 1) * HEAD_DIM] = o.astype(BF16)


def _fno_prompt_call(f, wy):
    b, n, w = f.shape
    cn, sn = _dft_tables(n, (n * HEAD_DIM) ** -0.5)
    cn = jnp.asarray(cn).astype(BF16)
    sn_neg = jnp.asarray(-sn).astype(BF16)
    return pl.pallas_call(
        _fno_prompt_kernel,
        grid=(b,),
        in_specs=[pl.BlockSpec((None, n, w), lambda i: (i, 0, 0)),
                  pl.BlockSpec((N_GROUPS, HEAD_DIM, 2 * HEAD_DIM), lambda i: (0, 0, 0)),
                  pl.BlockSpec((n, n), lambda i: (0, 0)),
                  pl.BlockSpec((n, n), lambda i: (0, 0))],
        out_specs=pl.BlockSpec((None, n, w), lambda i: (i, 0, 0)),
        out_shape=jax.ShapeDtypeStruct((b, n, w), BF16),
        compiler_params=_params(),
    )(f, wy, cn, sn_neg)


COL_UNROLL = 2
ROW_UNROLL = 8


def _fno_grid_kernel(f_ref, wy_ref, bdc_ref, bds_ref, crs_ref, o_ref, zr_sc, zi_sc, o_sc, *, n):
    tb = bdc_ref.shape[0]
    two = 2 * HEAD_DIM
    rows = n // GRID_W

    def col_body(i, carry):
        r0s = [pl.multiple_of((i * COL_UNROLL + j) * tb, tb) for j in range(COL_UNROLL)]
        ys = [_dot(f_ref[pl.ds(r0, tb), :], wy_ref[...]).astype(BF16) for r0 in r0s]
        zs = [_dot(bdc_ref[...], y[:, 0:two]) + _dot(bds_ref[...], y[:, two:]) for y in ys]
        for r0, z in zip(r0s, zs):
            zr_sc[pl.ds(r0, tb), :] = z[:, 0:HEAD_DIM]
            zi_sc[pl.ds(r0, tb), :] = z[:, HEAD_DIM:]
        return carry

    lax.fori_loop(0, n // (tb * COL_UNROLL), col_body, 0)

    def row_body(i, carry):
        cols = [pl.ds(i * ROW_UNROLL + j, rows, stride=GRID_W) for j in range(ROW_UNROLL)]
        zs = [jnp.concatenate([zr_sc[col, :], zi_sc[col, :]], axis=0).astype(BF16) for col in cols]
        outs = [_dot(crs_ref[...], z) for z in zs]
        for col, o in zip(cols, outs):
            o_sc[col, :] = o
        return carry

    lax.fori_loop(0, GRID_W // ROW_UNROLL, row_body, 0)
    o_ref[...] = o_sc[...].astype(BF16)


def _fno_grid_call(f, wy4):
    b, n, w = f.shape
    rows = n // GRID_W
    tb = 256
    cw, sw = _dft_tables(GRID_W)
    rep = np.eye(tb // GRID_W, dtype=np.float32)
    bdc = jnp.asarray(np.kron(rep, cw)).astype(BF16)
    bds = jnp.asarray(np.kron(rep, sw)).astype(BF16)
    cr, sr = _dft_tables(rows, (n * HEAD_DIM) ** -0.5)
    crs = jnp.asarray(np.concatenate([cr, sr], axis=1)).astype(BF16)
    const = lambda s: pl.BlockSpec(s, lambda i, g: (0,) * len(s))
    return pl.pallas_call(
        functools.partial(_fno_grid_kernel, n=n),
        grid=(b, N_GROUPS),
        in_specs=[pl.BlockSpec((None, n, HEAD_DIM), lambda i, g: (i, 0, g)),
                  pl.BlockSpec((None, HEAD_DIM, 4 * HEAD_DIM), lambda i, g: (g, 0, 0)),
                  const((tb, tb)), const((tb, tb)), const((rows, 2 * rows))],
        out_specs=pl.BlockSpec((None, n, HEAD_DIM), lambda i, g: (i, 0, g)),
        out_shape=jax.ShapeDtypeStruct((b, n, w), BF16),
        scratch_shapes=[pltpu.VMEM((n, HEAD_DIM), F32)] * 3,
        compiler_params=_params(),
    )(f, wy4, bdc, bds, crs)


FF_BLOCK = 256


def _ffn_kernel(x_ref, og_ref, fo_ref, mod_ref, woa_ref, wob_ref, gffn_ref, wg_ref, wu_ref, wd_ref,
                gfin_ref, y_ref):
    d = D_MODEL
    gate1 = mod_ref[:, 2 * d:3 * d]
    shift2 = mod_ref[:, 3 * d:4 * d]
    scale2 = mod_ref[:, 4 * d:5 * d]
    gate2 = mod_ref[:, 5 * d:6 * d]
    mo = _dot(og_ref[...], woa_ref[...]) + _dot(fo_ref[...], wob_ref[...])
    x1 = x_ref[...] + gate1 * mo
    hn = x1 * lax.rsqrt(jnp.mean(x1 * x1, axis=-1, keepdims=True) + RMS_EPS) * gffn_ref[...]
    h2 = (hn * (1.0 + scale2) + shift2).astype(BF16)
    acc = jnp.zeros(x1.shape, F32)
    for j in range(D_FF // FF_BLOCK):
        sl = slice(j * FF_BLOCK, (j + 1) * FF_BLOCK)
        gt = _dot(h2, wg_ref[:, sl])
        up = _dot(h2, wu_ref[:, sl])
        acc = acc + _dot((_silu(gt) * up).astype(BF16), wd_ref[sl, :])
    x2 = x1 + gate2 * acc
    y_ref[...] = x2 * lax.rsqrt(jnp.mean(x2 * x2, axis=-1, keepdims=True) + RMS_EPS) * gfin_ref[...]


def _ffn_call(x, og, fo, mod3, mod_row0, w_out_a, w_out_b, g_ffn, w_g, w_u, w_down, g_final, tm):
    b, n, d = x.shape
    tok = lambda w: pl.BlockSpec((None, tm, w), lambda i, t: (i, t, 0))
    const = lambda s: pl.BlockSpec(s, lambda i, t: (0,) * len(s), pipeline_mode=pl.Buffered(1))
    return pl.pallas_call(
        _ffn_kernel,
        grid=(b, n // tm),
        in_specs=[tok(d), tok(QK_WIDTH), tok(FOURIER_WIDTH),
                  pl.BlockSpec((None, 1, 6 * d), lambda i, t: (mod_row0 + i, 0, 0)),
                  const((QK_WIDTH, d)), const((FOURIER_WIDTH, d)), const((1, d)),
                  const((d, D_FF)), const((d, D_FF)), const((D_FF, d)), const((1, d))],
        out_specs=tok(d),
        out_shape=jax.ShapeDtypeStruct((b, n, d), F32),
        compiler_params=_params(),
    )(x, og, fo, mod3, w_out_a, w_out_b, g_ffn.reshape(1, d), w_g, w_u, w_down, g_final.reshape(1, d))


def _chunk_transposed(ab):
    b, n, _ = ab.shape
    g = ab[:, :, 0:2 * N_DIR * N_HEADS].reshape(b, n // CHUNK, CHUNK, 2 * N_DIR * N_HEADS)
    return jnp.swapaxes(g, 2, 3)


def kernel(x_prompt, x_sample, c, state_dn_fwd, state_dn_bwd, c_ctx, w_ada, b_ada, g_mix, w_in, w_conv,
           a_log, dt_bias, g_o, w_fno, w_out, g_ffn, w_gu, w_down, g_final):
    d = D_MODEL
    bp, np_, _ = x_prompt.shape
    bs, ns, _ = x_sample.shape
    l = 0

    wi = w_in[l]
    n_gate = 2 * N_DIR * N_HEADS
    g0 = QKV_WIDTH + QK_WIDTH
    w_cat = jnp.concatenate([wi[:, 0:g0], wi[:, g0 + n_gate:], wi[:, g0:g0 + n_gate],
                             jnp.zeros((d, LANES - n_gate), F32)], axis=1).astype(BF16)
    w_out_b16 = w_out[l].astype(BF16)
    w_out_a, w_out_b = w_out_b16[0:QK_WIDTH], w_out_b16[QK_WIDTH:]
    w_g = w_gu[l][:, 0:D_FF].astype(BF16)
    w_u = w_gu[l][:, D_FF:].astype(BF16)
    w_dn = w_down[l].astype(BF16)

    cond = jnp.concatenate([c_ctx[None, :], c, jnp.zeros((16 - 1 - bs, d), F32)], axis=0)
    mod = _mod_call(cond, w_ada[l], b_ada[l])
    mod3 = mod.reshape(16, 1, 6 * d)

    cc, sc = _dft_tables(HEAD_DIM)
    wy_p, wy_g = _fno_w_call(w_fno[l], jnp.asarray(cc), jnp.asarray(sc))

    xp = x_prompt.reshape(1, bp * np_, d)
    qkv, z, f, ab = _inproj_call(xp, mod3, 0, g_mix[l], w_cat, 512)
    qkv = qkv.reshape(bp, np_, QKV_WIDTH)
    z = z.reshape(bp, np_, QK_WIDTH)
    f = f.reshape(bp, np_, FOURIER_WIDTH)
    ab = ab.reshape(bp, np_, LANES)
    og, new_f, new_b = _delta_call(qkv, z, ab, _chunk_transposed(ab), w_conv[l], a_log[l], dt_bias[l],
                                   g_o[l], None, None, heads_per_step=2, unroll=2)
    fo = _fno_prompt_call(f, wy_p)
    y_prompt = _ffn_call(xp, og.reshape(1, bp * np_, QK_WIDTH), fo.reshape(1, bp * np_, FOURIER_WIDTH),
                         mod3, 0, w_out_a, w_out_b, g_ffn[l], w_g, w_u, w_dn, g_final, 512)
    y_prompt = y_prompt.reshape(bp, np_, d)

    qkv, z, f, ab = _inproj_call(x_sample, mod3, 1, g_mix[l], w_cat, 512)
    og, _, _ = _delta_call(qkv, z, ab, _chunk_transposed(ab), w_conv[l], a_log[l], dt_bias[l], g_o[l],
                           state_dn_fwd[:, l:l + 1], state_dn_bwd[:, l:l + 1], heads_per_step=1, unroll=4)
    fo = _fno_grid_call(f, wy_g)
    y_sample = _ffn_call(x_sample, og, fo, mod3, 1, w_out_a, w_out_b, g_ffn[l], w_g, w_u, w_dn, g_final, 512)

    return (y_prompt, y_sample, new_f, new_b)
```

```python
import functools

import numpy as np
import jax
import jax.numpy as jnp
from jax import lax
from jax.experimental import pallas as pl
from jax.experimental.pallas import tpu as pltpu

D_MODEL = 1024
N_HEADS = 4
HEAD_DIM = 128
QK_WIDTH = N_HEADS * HEAD_DIM
QKV_WIDTH = 3 * QK_WIDTH
N_GROUPS = 4
FOURIER_WIDTH = N_GROUPS * HEAD_DIM
N_DIR = 2
GRID_W = 64
CHUNK = 128
D_FF = 2816
RMS_EPS = 1e-6
LANES = 128
W_CAT_COLS = QKV_WIDTH + QK_WIDTH + FOURIER_WIDTH + LANES
VMEM_LIMIT = 56 * 1024 * 1024

F32 = jnp.float32
BF16 = jnp.bfloat16
HIGHEST = lax.Precision.HIGHEST


def _dot(a, b):
    return jnp.dot(a, b, preferred_element_type=F32)


def _silu(x):
    return x * jax.nn.sigmoid(x)


def _softplus(x):
    return jnp.maximum(x, 0.0) + jnp.log1p(jnp.exp(-jnp.abs(x)))


def _params(**kw):
    return pltpu.CompilerParams(vmem_limit_bytes=VMEM_LIMIT, **kw)


def _mod_kernel(cond_ref, w_ref, b_ref, o_ref):
    s = _silu(cond_ref[...]).astype(BF16)
    o_ref[...] = _dot(s, w_ref[...].astype(BF16)) + b_ref[...]


def _mod_call(cond, w_ada, b_ada):
    rows, d = cond.shape
    cols = w_ada.shape[1]
    tn = 1536
    return pl.pallas_call(
        _mod_kernel,
        grid=(cols // tn,),
        in_specs=[pl.BlockSpec((rows, d), lambda j: (0, 0)),
                  pl.BlockSpec((d, tn), lambda j: (0, j)),
                  pl.BlockSpec((1, tn), lambda j: (0, j))],
        out_specs=pl.BlockSpec((rows, tn), lambda j: (0, j)),
        out_shape=jax.ShapeDtypeStruct((rows, cols), F32),
        compiler_params=_params(),
    )(cond, w_ada, b_ada.reshape(1, cols))


def _fno_w_kernel(w_ref, cc_ref, sc_ref, wp_ref, wg_ref):
    w = w_ref[...]
    cw = jnp.dot(cc_ref[...], w, precision=HIGHEST, preferred_element_type=F32)
    sw = jnp.dot(sc_ref[...], w, precision=HIGHEST, preferred_element_type=F32)
    wp_ref[...] = jnp.concatenate([cw, sw], axis=1).astype(BF16)
    wg_ref[...] = jnp.concatenate([cw, -sw, -sw, -cw], axis=1).astype(BF16)


def _fno_w_call(w_fno, cc, sc):
    g, c, _ = w_fno.shape
    return pl.pallas_call(
        _fno_w_kernel,
        grid=(g,),
        in_specs=[pl.BlockSpec((None, c, c), lambda i: (i, 0, 0)),
                  pl.BlockSpec((c, c), lambda i: (0, 0)),
                  pl.BlockSpec((c, c), lambda i: (0, 0))],
        out_specs=[pl.BlockSpec((None, c, 2 * c), lambda i: (i, 0, 0)),
                   pl.BlockSpec((None, c, 4 * c), lambda i: (i, 0, 0))],
        out_shape=[jax.ShapeDtypeStruct((g, c, 2 * c), BF16),
                   jax.ShapeDtypeStruct((g, c, 4 * c), BF16)],
        compiler_params=_params(),
    )(w_fno, cc, sc)


HALO = 8
CONV_COLS = 2 * HEAD_DIM


def _inproj_kernel(xp_ref, x_ref, xn_ref, mod_ref, g_ref, w_ref, wc_ref, qkv_ref, z_ref, f_ref, ab_ref, *,
                   tm, seq_len):
    t = pl.program_id(1)
    x = jnp.concatenate([xp_ref[...], x_ref[...], xn_ref[...]], axis=0)
    y = x * lax.rsqrt(jnp.mean(x * x, axis=-1, keepdims=True) + RMS_EPS) * g_ref[...]
    shift1 = mod_ref[:, 0:D_MODEL]
    scale1 = mod_ref[:, D_MODEL:2 * D_MODEL]
    h_ext = (y * (1.0 + scale1) + shift1).astype(BF16)
    h = h_ext[HALO:HALO + tm, :]

    rows = lax.broadcasted_iota(jnp.int32, (tm, CONV_COLS), 0)
    pos = (t * tm + rows) & (seq_len - 1)
    first = pos == 0
    last = pos == seq_len - 1
    for j in range(QKV_WIDTH // CONV_COLS):
        cols = slice(j * CONV_COLS, (j + 1) * CONV_COLS)
        p_ext = _dot(h_ext, w_ref[:, cols])
        p = p_ext[HALO:HALO + tm, :]
        p_prev = jnp.where(rows == 0, p_ext[HALO - 1:HALO, :], pltpu.roll(p, 1, 0))
        p_next = jnp.where(rows == tm - 1, p_ext[HALO + tm:HALO + tm + 1, :], pltpu.roll(p, tm - 1, 0))
        p_prev = jnp.where(first, 0.0, p_prev)
        p_next = jnp.where(last, 0.0, p_next)
        c = _silu(wc_ref[0:1, cols] * p_prev + wc_ref[1:2, cols] * p + wc_ref[2:3, cols] * p_next)
        for hh in range(CONV_COLS // HEAD_DIM):
            head = j * (CONV_COLS // HEAD_DIM) + hh
            ch = c[:, hh * HEAD_DIM:(hh + 1) * HEAD_DIM]
            if head < 2 * N_HEADS:
                ch = ch * lax.rsqrt(jnp.sum(ch * ch, axis=-1, keepdims=True) + 1e-6)
            if head < N_HEADS:
                ch = ch * (HEAD_DIM ** -0.5)
            qkv_ref[:, head * HEAD_DIM:(head + 1) * HEAD_DIM] = ch.astype(BF16)
    c0, c1, c2 = QKV_WIDTH, QKV_WIDTH + QK_WIDTH, QKV_WIDTH + QK_WIDTH + FOURIER_WIDTH
    z_ref[...] = _dot(h, w_ref[:, c0:c1]).astype(BF16)
    f_ref[...] = _dot(h, w_ref[:, c1:c2]).astype(BF16)
    ab_ref[...] = _dot(h, w_ref[:, c2:W_CAT_COLS])


def _inproj_call(x, mod3, mod_row0, mod_per_batch, g_mix, w_cat, w_conv, tm, seq_len):
    b, n, d = x.shape
    assert seq_len & (seq_len - 1) == 0 and n % tm == 0 and tm % HALO == 0
    tok = lambda w: pl.BlockSpec((None, tm, w), lambda i, t: (i, t, 0))
    per = tm // HALO
    return pl.pallas_call(
        functools.partial(_inproj_kernel, tm=tm, seq_len=seq_len),
        grid=(b, n // tm),
        in_specs=[pl.BlockSpec((None, HALO, d), lambda i, t: (i, jnp.maximum(t * per - 1, 0), 0)),
                  tok(d),
                  pl.BlockSpec((None, HALO, d), lambda i, t: (i, jnp.minimum((t + 1) * per, n // HALO - 1), 0)),
                  pl.BlockSpec((None, 1, 6 * d), lambda i, t: (mod_row0 + (i if mod_per_batch else 0), 0, 0)),
                  pl.BlockSpec((1, d), lambda i, t: (0, 0)),
                  pl.BlockSpec((d, W_CAT_COLS), lambda i, t: (0, 0)),
                  pl.BlockSpec((3, QKV_WIDTH), lambda i, t: (0, 0))],
        out_specs=[tok(QKV_WIDTH), tok(QK_WIDTH), tok(FOURIER_WIDTH), tok(LANES)],
        out_shape=[jax.ShapeDtypeStruct((b, n, QKV_WIDTH), BF16),
                   jax.ShapeDtypeStruct((b, n, QK_WIDTH), BF16),
                   jax.ShapeDtypeStruct((b, n, FOURIER_WIDTH), BF16),
                   jax.ShapeDtypeStruct((b, n, LANES), F32)],
        compiler_params=_params(),
    )(x, x, x, mod3, g_mix.reshape(1, d), w_cat, w_conv)


LEVEL_BLOCKS = tuple(2 ** e for e in range(int(np.log2(CHUNK))))


def _tri_inverse(a_list, eye, level_masks):
    ds = [eye - jnp.where(level_masks[0], a, 0.0) for a in a_list]
    for mask in level_masks[1:]:
        ls = [jnp.where(mask, a, 0.0).astype(BF16) for a in a_list]
        dbs = [d.astype(BF16) for d in ds]
        ms = [_dot(db, l).astype(BF16) for db, l in zip(dbs, ls)]
        ds = [d - _dot(m, db) for d, m, db in zip(ds, ms, dbs)]
    return ds


def _delta_kernel(*refs, n, hp_n, unroll, has_state):
    if has_state:
        (gp_ref, q_ref, k_ref, v_ref, z_ref, abt_ref, go_ref, tric_ref, trir_ref, sf0_ref, sb0_ref,
         og_ref, sf_ref, sb_ref, lhs_sc, nst_sc, gls, osc) = refs
    else:
        (gp_ref, q_ref, k_ref, v_ref, z_ref, abt_ref, go_ref, tric_ref, trir_ref,
         og_ref, sf_ref, sb_ref, lhs_sc, nst_sc, gls, osc) = refs
    hg = pl.program_id(1)
    nc = n // CHUNK
    rblk = 256
    nblk = n // rblk

    ci = lax.broadcasted_iota(jnp.int32, (CHUNK, CHUNK), 0)
    cj = lax.broadcasted_iota(jnp.int32, (CHUNK, CHUNK), 1)
    eye = (ci == cj).astype(F32)
    level_masks = [((ci // (2 * b)) == (cj // (2 * b))) & ((ci // b) != (cj // b))
                   for b in LEVEL_BLOCKS]
    incl = (ci >= cj, ci <= cj)
    strict = (ci > cj, ci < cj)
    n_half = N_DIR * N_HEADS

    def head_cols(hp):
        return slice(hp * HEAD_DIM, (hp + 1) * HEAD_DIM)

    def split3_lanes(x):
        hi = x.astype(BF16)
        r1 = x - hi.astype(F32)
        lo = r1.astype(BF16)
        lo2 = (r1 - lo.astype(F32)).astype(BF16)
        return jnp.concatenate([hi, lo, lo2], axis=1)

    def select_col(x, col):
        lanes = lax.broadcasted_iota(jnp.int32, x.shape, 1)
        return jnp.sum(jnp.where(lanes == col, x, 0.0), axis=-1, keepdims=True)

    def chain_front(hp, c, d, q, k, v, kk, qk, cum, beta_all):
        col = d * N_HEADS + hg * hp_n + hp
        g_col = jnp.broadcast_to(select_col(cum[d], col), (CHUNK, LANES))
        beta = select_col(beta_all, n_half + col)
        a_row = abt_ref[c, pl.ds(col, 1), :]
        la_row = -jnp.exp(gp_ref[pl.ds(col, 1), :]) * _softplus(a_row + gp_ref[pl.ds(n_half + col, 1), :])
        g_row = _dot(split3_lanes(jnp.broadcast_to(la_row, (8, CHUNK))), trir_ref[d])[0:1, :]
        g_tot = g_col[CHUNK - 1:CHUNK, :] if d == 0 else g_col[0:1, :]
        decay = jnp.exp(jnp.where(incl[d], g_col - g_row, -jnp.inf))
        eg = jnp.exp(g_col)
        gls[hp, d, c] = jnp.broadcast_to(jnp.exp(g_tot), (8, LANES))
        a = jnp.where(strict[d], beta * decay * kk, 0.0)
        rhs = jnp.concatenate([beta * v, (beta * eg) * k], axis=1).astype(BF16)
        lhs2 = jnp.concatenate([(k * jnp.exp(g_tot - g_col)).T.astype(BF16), (qk * decay).astype(BF16)], axis=0)
        return a, rhs, lhs2, q * eg

    def prepare(i, carry):
        chains = []
        for hp in range(hp_n):
            for j in range(unroll):
                c = i * unroll + j
                r0 = pl.multiple_of(c * CHUNK, CHUNK)
                qb = q_ref[pl.ds(r0, CHUNK), head_cols(hp)]
                kb = k_ref[pl.ds(r0, CHUNK), head_cols(hp)]
                q = qb.astype(F32)
                k = kb.astype(F32)
                v = v_ref[pl.ds(r0, CHUNK), head_cols(hp)].astype(F32)
                kq = lax.dot_general(jnp.concatenate([kb, qb], axis=0), kb,
                                     (((1,), (1,)), ((), ())), preferred_element_type=F32)
                raw = abt_ref[c]
                log_a = -jnp.exp(gp_ref[0:n_half, :]) * _softplus(raw[0:n_half, :] + gp_ref[n_half:, :])
                gates = split3_lanes(jnp.concatenate([log_a, jax.nn.sigmoid(raw[n_half:, :])], axis=0))
                gates = jnp.concatenate([gates, jnp.zeros((LANES - 2 * n_half, 3 * CHUNK), BF16)], axis=0)
                cols3 = lax.dot_general(tric_ref[...], gates, (((1,), (1,)), ((), ())),
                                        preferred_element_type=F32)
                cum_f = cols3[0:CHUNK, :]
                beta_all = cols3[CHUNK:, :]
                cum = (cum_f, cum_f[CHUNK - 1:CHUNK, :] - cum_f + beta_all)
                for d in range(N_DIR):
                    chains.append((hp, j, c, d) + chain_front(hp, c, d, q, k, v, kq[0:CHUNK, :], kq[CHUNK:, :],
                                                              cum, beta_all))
        t_invs = _tri_inverse([ch[4] for ch in chains], eye, level_masks)
        o_local = {}
        for (hp, j, c, d, _, rhs, lhs2, qg), t_inv in zip(chains, t_invs):
            sol = _dot(t_inv.astype(BF16), rhs).astype(BF16)
            x = _dot(lhs2, sol)
            nst_sc[hp, d, c] = x[0:HEAD_DIM, 0:HEAD_DIM]
            lhs_sc[hp, d, c, 0:HEAD_DIM, :] = (-x[0:HEAD_DIM, HEAD_DIM:]).astype(BF16)
            lhs_sc[hp, d, c, HEAD_DIM:, :] = (qg - x[HEAD_DIM:, HEAD_DIM:]).astype(BF16)
            o_local[(hp, j, d)] = x[HEAD_DIM:, 0:HEAD_DIM]
        for hp in range(hp_n):
            for j in range(unroll):
                c = i * unroll + j
                r0 = pl.multiple_of(c * CHUNK, CHUNK)
                osc[hp, pl.ds(r0, CHUNK), :] = o_local[(hp, j, 0)] + o_local[(hp, j, 1)]
        return carry

    lax.fori_loop(0, nc // unroll, prepare, 0)

    def scan_step(hp, c, d, s):
        r0 = pl.multiple_of(c * CHUNK, CHUNK)
        r = _dot(lhs_sc[hp, d, c], s.astype(BF16))
        osc[hp, pl.ds(r0, CHUNK), :] += r[HEAD_DIM:, :]
        return gls[hp, d, c][0:1, :] * s + r[0:HEAD_DIM, :] + nst_sc[hp, d, c]

    def scan_body(i, carry):
        out = []
        for hp in range(hp_n):
            out.append(scan_step(hp, i, 0, carry[2 * hp]))
            out.append(scan_step(hp, nc - 1 - i, 1, carry[2 * hp + 1]))
        return tuple(out)

    init = []
    for hp in range(hp_n):
        if has_state:
            init += [sf0_ref[hp], sb0_ref[hp]]
        else:
            init += [jnp.zeros((HEAD_DIM, HEAD_DIM), F32), jnp.zeros((HEAD_DIM, HEAD_DIM), F32)]
    fin = lax.fori_loop(0, nc, scan_body, tuple(init))
    for hp in range(hp_n):
        sf_ref[hp] = fin[2 * hp]
        sb_ref[hp] = fin[2 * hp + 1]

    def post_body(i, carry):
        r0 = pl.multiple_of(i * rblk, rblk)
        for hp in range(hp_n):
            o = osc[hp, pl.ds(r0, rblk), :]
            y = o * lax.rsqrt(jnp.mean(o * o, axis=-1, keepdims=True) + RMS_EPS) * go_ref[...]
            zz = z_ref[pl.ds(r0, rblk), head_cols(hp)].astype(F32)
            og_ref[pl.ds(r0, rblk), head_cols(hp)] = (y * _silu(zz)).astype(BF16)
        return carry

    lax.fori_loop(0, nblk, post_body, 0)


def _tri_tables():
    i = np.arange(CHUNK)
    low = (i[:, None] >= i[None, :]).astype(np.float32)
    up = low.T
    tric = np.concatenate([np.concatenate([m, m, m], axis=1) for m in (low, np.eye(CHUNK))], axis=0)
    trir = np.stack([np.concatenate([m, m, m], axis=0) for m in (up, low)])
    return jnp.asarray(tric).astype(BF16), jnp.asarray(trir).astype(BF16)


def _delta_call(qkv, z, abt, a_log, dt_bias, g_o, s0_f, s0_b, *, heads_per_step, unroll):
    b, n, _ = qkv.shape
    nc = n // CHUNK
    hp_n = heads_per_step
    groups = N_HEADS // hp_n
    wide = hp_n * HEAD_DIM
    has_state = s0_f is not None
    tric, trir = _tri_tables()
    col = lambda off: pl.BlockSpec((None, n, wide), lambda i, g: (i, 0, off + g))
    st = pl.BlockSpec((None, None, hp_n, HEAD_DIM, HEAD_DIM), lambda i, g: (i, 0, g, 0, 0))
    n_gate = 2 * N_DIR * N_HEADS
    gate_params = jnp.broadcast_to(jnp.concatenate([a_log.reshape(-1), dt_bias.reshape(-1)])[:, None],
                                   (n_gate, CHUNK))
    in_specs = [pl.BlockSpec((n_gate, CHUNK), lambda i, g: (0, 0)),
                col(0), col(groups), col(2 * groups), col(0),
                pl.BlockSpec((None, nc, n_gate, CHUNK), lambda i, g: (i, 0, 0, 0)),
                pl.BlockSpec((1, HEAD_DIM), lambda i, g: (0, 0)),
                pl.BlockSpec((2 * CHUNK, 3 * CHUNK), lambda i, g: (0, 0)),
                pl.BlockSpec((N_DIR, 3 * CHUNK, CHUNK), lambda i, g: (0, 0, 0))]
    args = [gate_params, qkv, qkv, qkv, z, abt, g_o.reshape(1, HEAD_DIM), tric, trir]
    if has_state:
        in_specs += [st, st]
        args += [s0_f, s0_b]
    state_shape = jax.ShapeDtypeStruct((b, 1, N_HEADS, HEAD_DIM, HEAD_DIM), F32)
    return pl.pallas_call(
        functools.partial(_delta_kernel, n=n, hp_n=hp_n, unroll=unroll, has_state=has_state),
        grid=(b, groups),
        in_specs=in_specs,
        out_specs=[col(0), st, st],
        out_shape=[jax.ShapeDtypeStruct((b, n, QK_WIDTH), BF16), state_shape, state_shape],
        scratch_shapes=[pltpu.VMEM((hp_n, N_DIR, nc, HEAD_DIM + CHUNK, HEAD_DIM), BF16),
                        pltpu.VMEM((hp_n, N_DIR, nc, HEAD_DIM, HEAD_DIM), F32),
                        pltpu.VMEM((hp_n, N_DIR, nc, 8, LANES), F32),
                        pltpu.VMEM((hp_n, n, HEAD_DIM), F32)],
        compiler_params=_params(),
    )(*args)


def _dft_tables(n, scale=1.0):
    idx = np.arange(n)
    ang = 2.0 * np.pi * ((idx[:, None] * idx[None, :]) % n) / n
    return (np.cos(ang) * scale).astype(np.float32), (np.sin(ang) * scale).astype(np.float32)


def _fno_prompt_kernel(f_ref, wy_ref, cn_ref, sn_ref, o_ref):
    cn = cn_ref[...]
    sn = sn_ref[...]
    for g in range(N_GROUPS):
        y = _dot(f_ref[:, g * HEAD_DIM:(g + 1) * HEAD_DIM], wy_ref[g])
        o = _dot(cn, y[:, 0:HEAD_DIM].astype(BF16)) + _dot(sn, y[:, HEAD_DIM:].astype(BF16))
        o_ref[:, g * HEAD_DIM:(g + 1) * HEAD_DIM] = o.astype(BF16)


def _fno_prompt_call(f, wy):
    b, n, w = f.shape
    cn, sn = _dft_tables(n, (n * HEAD_DIM) ** -0.5)
    cn = jnp.asarray(cn).astype(BF16)
    sn_neg = jnp.asarray(-sn).astype(BF16)
    return pl.pallas_call(
        _fno_prompt_kernel,
        grid=(b,),
        in_specs=[pl.BlockSpec((None, n, w), lambda i: (i, 0, 0)),
                  pl.BlockSpec((N_GROUPS, HEAD_DIM, 2 * HEAD_DIM), lambda i: (0, 0, 0)),
                  pl.BlockSpec((n, n), lambda i: (0, 0)),
                  pl.BlockSpec((n, n), lambda i: (0, 0))],
        out_specs=pl.BlockSpec((None, n, w), lambda i: (i, 0, 0)),
        out_shape=jax.ShapeDtypeStruct((b, n, w), BF16),
        compiler_params=_params(),
    )(f, wy, cn, sn_neg)


COL_UNROLL = 2
ROW_UNROLL = 8
ROW_PITCH = GRID_W + 8


def _fno_grid_kernel(f_ref, wy_ref, bdc_ref, bds_ref, crs_ref, o_ref, zr_sc, zi_sc, o_sc, *, n):
    tb = bdc_ref.shape[0]
    two = 2 * HEAD_DIM
    rows = n // GRID_W
    rows_per_block = tb // GRID_W

    def col_body(i, carry):
        blocks = [i * COL_UNROLL + j for j in range(COL_UNROLL)]
        ys = [_dot(f_ref[pl.ds(pl.multiple_of(blk * tb, tb), tb), :], wy_ref[...]).astype(BF16)
              for blk in blocks]
        zs = [_dot(bdc_ref[...], y[:, 0:two]) + _dot(bds_ref[...], y[:, two:]) for y in ys]
        for blk, z in zip(blocks, zs):
            for rr in range(rows_per_block):
                dst = pl.ds(pl.multiple_of((blk * rows_per_block + rr) * ROW_PITCH, 8), GRID_W)
                zr_sc[dst, :] = z[rr * GRID_W:(rr + 1) * GRID_W, 0:HEAD_DIM]
                zi_sc[dst, :] = z[rr * GRID_W:(rr + 1) * GRID_W, HEAD_DIM:]
        return carry

    lax.fori_loop(0, n // (tb * COL_UNROLL), col_body, 0)

    def row_body(i, carry):
        cols = [pl.ds(i * ROW_UNROLL + j, rows, stride=ROW_PITCH) for j in range(ROW_UNROLL)]
        zs = [jnp.concatenate([zr_sc[col, :], zi_sc[col, :]], axis=0).astype(BF16) for col in cols]
        outs = [_dot(crs_ref[...], z) for z in zs]
        for col, o in zip(cols, outs):
            o_sc[col, :] = o
        return carry

    lax.fori_loop(0, GRID_W // ROW_UNROLL, row_body, 0)

    def out_body(r, carry):
        src = pl.ds(pl.multiple_of(r * ROW_PITCH, 8), GRID_W)
        o_ref[pl.ds(pl.multiple_of(r * GRID_W, GRID_W), GRID_W), :] = o_sc[src, :].astype(BF16)
        return carry

    lax.fori_loop(0, rows, out_body, 0)


def _fno_grid_call(f, wy4):
    b, n, w = f.shape
    rows = n // GRID_W
    tb = 256
    cw, sw = _dft_tables(GRID_W)
    rep = np.eye(tb // GRID_W, dtype=np.float32)
    bdc = jnp.asarray(np.kron(rep, cw)).astype(BF16)
    bds = jnp.asarray(np.kron(rep, sw)).astype(BF16)
    cr, sr = _dft_tables(rows, (n * HEAD_DIM) ** -0.5)
    crs = jnp.asarray(np.concatenate([cr, sr], axis=1)).astype(BF16)
    const = lambda s: pl.BlockSpec(s, lambda i, g: (0,) * len(s))
    return pl.pallas_call(
        functools.partial(_fno_grid_kernel, n=n),
        grid=(b, N_GROUPS),
        in_specs=[pl.BlockSpec((None, n, HEAD_DIM), lambda i, g: (i, 0, g)),
                  pl.BlockSpec((None, HEAD_DIM, 4 * HEAD_DIM), lambda i, g: (g, 0, 0)),
                  const((tb, tb)), const((tb, tb)), const((rows, 2 * rows))],
        out_specs=pl.BlockSpec((None, n, HEAD_DIM), lambda i, g: (i, 0, g)),
        out_shape=jax.ShapeDtypeStruct((b, n, w), BF16),
        scratch_shapes=[pltpu.VMEM((rows * ROW_PITCH, HEAD_DIM), F32)] * 3,
        compiler_params=_params(),
    )(f, wy4, bdc, bds, crs)


FF_BLOCK = 256


def _ffn_kernel(x_ref, og_ref, fo_ref, mod_ref, woa_ref, wob_ref, gffn_ref, wg_ref, wu_ref, wd_ref,
                gfin_ref, y_ref):
    d = D_MODEL
    gate1 = mod_ref[:, 2 * d:3 * d]
    shift2 = mod_ref[:, 3 * d:4 * d]
    scale2 = mod_ref[:, 4 * d:5 * d]
    gate2 = mod_ref[:, 5 * d:6 * d]
    mo = _dot(og_ref[...], woa_ref[...]) + _dot(fo_ref[...], wob_ref[...])
    x1 = x_ref[...] + gate1 * mo
    hn = x1 * lax.rsqrt(jnp.mean(x1 * x1, axis=-1, keepdims=True) + RMS_EPS) * gffn_ref[...]
    h2 = (hn * (1.0 + scale2) + shift2).astype(BF16)
    acc = jnp.zeros(x1.shape, F32)
    for j in range(D_FF // FF_BLOCK):
        sl = slice(j * FF_BLOCK, (j + 1) * FF_BLOCK)
        gt = _dot(h2, wg_ref[:, sl])
        up = _dot(h2, wu_ref[:, sl])
        acc = acc + _dot((_silu(gt) * up).astype(BF16), wd_ref[sl, :])
    x2 = x1 + gate2 * acc
    y_ref[...] = x2 * lax.rsqrt(jnp.mean(x2 * x2, axis=-1, keepdims=True) + RMS_EPS) * gfin_ref[...]


def _ffn_call(x, og, fo, mod3, mod_row0, w_out_a, w_out_b, g_ffn, w_g, w_u, w_down, g_final, tm):
    b, n, d = x.shape
    tok = lambda w: pl.BlockSpec((None, tm, w), lambda i, t: (i, t, 0))
    const = lambda s: pl.BlockSpec(s, lambda i, t: (0,) * len(s), pipeline_mode=pl.Buffered(1))
    return pl.pallas_call(
        _ffn_kernel,
        grid=(b, n // tm),
        in_specs=[tok(d), tok(QK_WIDTH), tok(FOURIER_WIDTH),
                  pl.BlockSpec((None, 1, 6 * d), lambda i, t: (mod_row0 + i, 0, 0)),
                  const((QK_WIDTH, d)), const((FOURIER_WIDTH, d)), const((1, d)),
                  const((d, D_FF)), const((d, D_FF)), const((D_FF, d)), const((1, d))],
        out_specs=tok(d),
        out_shape=jax.ShapeDtypeStruct((b, n, d), F32),
        compiler_params=_params(),
    )(x, og, fo, mod3, w_out_a, w_out_b, g_ffn.reshape(1, d), w_g, w_u, w_down, g_final.reshape(1, d))


def _chunk_transposed(ab):
    b, n, _ = ab.shape
    g = ab[:, :, 0:2 * N_DIR * N_HEADS].reshape(b, n // CHUNK, CHUNK, 2 * N_DIR * N_HEADS)
    return jnp.swapaxes(g, 2, 3)


def kernel(x_prompt, x_sample, c, state_dn_fwd, state_dn_bwd, c_ctx, w_ada, b_ada, g_mix, w_in, w_conv,
           a_log, dt_bias, g_o, w_fno, w_out, g_ffn, w_gu, w_down, g_final):
    d = D_MODEL
    bp, np_, _ = x_prompt.shape
    bs, ns, _ = x_sample.shape
    l = 0

    wi = w_in[l]
    n_gate = 2 * N_DIR * N_HEADS
    g0 = QKV_WIDTH + QK_WIDTH
    w_cat = jnp.concatenate([wi[:, 0:g0], wi[:, g0 + n_gate:], wi[:, g0:g0 + n_gate],
                             jnp.zeros((d, LANES - n_gate), F32)], axis=1).astype(BF16)
    w_out_b16 = w_out[l].astype(BF16)
    w_out_a, w_out_b = w_out_b16[0:QK_WIDTH], w_out_b16[QK_WIDTH:]
    w_g = w_gu[l][:, 0:D_FF].astype(BF16)
    w_u = w_gu[l][:, D_FF:].astype(BF16)
    w_dn = w_down[l].astype(BF16)

    cond = jnp.concatenate([c_ctx[None, :], c, jnp.zeros((16 - 1 - bs, d), F32)], axis=0)
    mod = _mod_call(cond, w_ada[l], b_ada[l])
    mod3 = mod.reshape(16, 1, 6 * d)

    cc, sc = _dft_tables(HEAD_DIM)
    wy_p, wy_g = _fno_w_call(w_fno[l], jnp.asarray(cc), jnp.asarray(sc))

    xp = x_prompt.reshape(1, bp * np_, d)
    qkv, z, f, ab = _inproj_call(xp, mod3, 0, False, g_mix[l], w_cat, w_conv[l], 512, np_)
    qkv = qkv.reshape(bp, np_, QKV_WIDTH)
    z = z.reshape(bp, np_, QK_WIDTH)
    f = f.reshape(bp, np_, FOURIER_WIDTH)
    ab = ab.reshape(bp, np_, LANES)
    og, new_f, new_b = _delta_call(qkv, z, _chunk_transposed(ab), a_log[l], dt_bias[l], g_o[l], None, None,
                                   heads_per_step=2, unroll=2)
    fo = _fno_prompt_call(f, wy_p)
    y_prompt = _ffn_call(xp, og.reshape(1, bp * np_, QK_WIDTH), fo.reshape(1, bp * np_, FOURIER_WIDTH),
                         mod3, 0, w_out_a, w_out_b, g_ffn[l], w_g, w_u, w_dn, g_final, 512)
    y_prompt = y_prompt.reshape(bp, np_, d)

    qkv, z, f, ab = _inproj_call(x_sample, mod3, 1, True, g_mix[l], w_cat, w_conv[l], 512, ns)
    og, _, _ = _delta_call(qkv, z, _chunk_transposed(ab), a_log[l], dt_bias[l], g_o[l],
                           state_dn_fwd[:, l:l + 1], state_dn_bwd[:, l:l + 1], heads_per_step=1, unroll=4)
    fo = _fno_grid_call(f, wy_g)
    y_sample = _ffn_call(x_sample, og, fo, mod3, 1, w_out_a, w_out_b, g_ffn[l], w_g, w_u, w_dn, g_final, 512)

    return (y_prompt, y_sample, new_f, new_b)
```

```python
import functools

import numpy as np
import jax
import jax.numpy as jnp
from jax import lax
from jax.experimental import pallas as pl
from jax.experimental.pallas import tpu as pltpu

D_MODEL = 1024
N_HEADS = 4
HEAD_DIM = 128
QK_WIDTH = N_HEADS * HEAD_DIM
QKV_WIDTH = 3 * QK_WIDTH
N_GROUPS = 4
FOURIER_WIDTH = N_GROUPS * HEAD_DIM
N_DIR = 2
GRID_W = 64
CHUNK = 128
D_FF = 2816
RMS_EPS = 1e-6
LANES = 128
W_CAT_COLS = QKV_WIDTH + QK_WIDTH + FOURIER_WIDTH + LANES
VMEM_LIMIT = 56 * 1024 * 1024

F32 = jnp.float32
BF16 = jnp.bfloat16
HIGHEST = lax.Precision.HIGHEST


def _dot(a, b):
    return jnp.dot(a, b, preferred_element_type=F32)


def _silu(x):
    return x * jax.nn.sigmoid(x)


def _softplus(x):
    return jnp.maximum(x, 0.0) + jnp.log1p(jnp.exp(-jnp.abs(x)))


def _params(**kw):
    return pltpu.CompilerParams(vmem_limit_bytes=VMEM_LIMIT, **kw)


def _mod_kernel(cond_ref, w_ref, b_ref, o_ref):
    s = _silu(cond_ref[...]).astype(BF16)
    o_ref[...] = _dot(s, w_ref[...].astype(BF16)) + b_ref[...]


def _mod_call(cond, w_ada, b_ada):
    rows, d = cond.shape
    cols = w_ada.shape[1]
    tn = 1536
    return pl.pallas_call(
        _mod_kernel,
        grid=(cols // tn,),
        in_specs=[pl.BlockSpec((rows, d), lambda j: (0, 0)),
                  pl.BlockSpec((d, tn), lambda j: (0, j)),
                  pl.BlockSpec((1, tn), lambda j: (0, j))],
        out_specs=pl.BlockSpec((rows, tn), lambda j: (0, j)),
        out_shape=jax.ShapeDtypeStruct((rows, cols), F32),
        compiler_params=_params(),
    )(cond, w_ada, b_ada.reshape(1, cols))


def _fno_w_kernel(w_ref, cc_ref, sc_ref, wp_ref, wg_ref):
    w = w_ref[...]
    cw = jnp.dot(cc_ref[...], w, precision=HIGHEST, preferred_element_type=F32)
    sw = jnp.dot(sc_ref[...], w, precision=HIGHEST, preferred_element_type=F32)
    wp_ref[...] = jnp.concatenate([cw, sw], axis=1).astype(BF16)
    wg_ref[...] = jnp.concatenate([cw, -sw, -sw, -cw], axis=1).astype(BF16)


def _fno_w_call(w_fno, cc, sc):
    g, c, _ = w_fno.shape
    return pl.pallas_call(
        _fno_w_kernel,
        grid=(g,),
        in_specs=[pl.BlockSpec((None, c, c), lambda i: (i, 0, 0)),
                  pl.BlockSpec((c, c), lambda i: (0, 0)),
                  pl.BlockSpec((c, c), lambda i: (0, 0))],
        out_specs=[pl.BlockSpec((None, c, 2 * c), lambda i: (i, 0, 0)),
                   pl.BlockSpec((None, c, 4 * c), lambda i: (i, 0, 0))],
        out_shape=[jax.ShapeDtypeStruct((g, c, 2 * c), BF16),
                   jax.ShapeDtypeStruct((g, c, 4 * c), BF16)],
        compiler_params=_params(),
    )(w_fno, cc, sc)


HALO = 8
CONV_COLS = 2 * HEAD_DIM


def _inproj_kernel(xp_ref, x_ref, xn_ref, mod_ref, g_ref, w_ref, wc_ref, qkv_ref, z_ref, f_ref, ab_ref, *,
                   tm, seq_len):
    t = pl.program_id(1)
    x = jnp.concatenate([xp_ref[...], x_ref[...], xn_ref[...]], axis=0)
    y = x * lax.rsqrt(jnp.mean(x * x, axis=-1, keepdims=True) + RMS_EPS) * g_ref[...]
    shift1 = mod_ref[:, 0:D_MODEL]
    scale1 = mod_ref[:, D_MODEL:2 * D_MODEL]
    h_ext = (y * (1.0 + scale1) + shift1).astype(BF16)
    h = h_ext[HALO:HALO + tm, :]

    rows = lax.broadcasted_iota(jnp.int32, (tm, CONV_COLS), 0)
    pos = (t * tm + rows) & (seq_len - 1)
    first = pos == 0
    last = pos == seq_len - 1
    for j in range(QKV_WIDTH // CONV_COLS):
        cols = slice(j * CONV_COLS, (j + 1) * CONV_COLS)
        p_ext = _dot(h_ext, w_ref[:, cols])
        p = p_ext[HALO:HALO + tm, :]
        p_prev = jnp.where(rows == 0, p_ext[HALO - 1:HALO, :], pltpu.roll(p, 1, 0))
        p_next = jnp.where(rows == tm - 1, p_ext[HALO + tm:HALO + tm + 1, :], pltpu.roll(p, tm - 1, 0))
        p_prev = jnp.where(first, 0.0, p_prev)
        p_next = jnp.where(last, 0.0, p_next)
        c = _silu(wc_ref[0:1, cols] * p_prev + wc_ref[1:2, cols] * p + wc_ref[2:3, cols] * p_next)
        for hh in range(CONV_COLS // HEAD_DIM):
            head = j * (CONV_COLS // HEAD_DIM) + hh
            ch = c[:, hh * HEAD_DIM:(hh + 1) * HEAD_DIM]
            if head < 2 * N_HEADS:
                ch = ch * lax.rsqrt(jnp.sum(ch * ch, axis=-1, keepdims=True) + 1e-6)
            if head < N_HEADS:
                ch = ch * (HEAD_DIM ** -0.5)
            qkv_ref[:, head * HEAD_DIM:(head + 1) * HEAD_DIM] = ch.astype(BF16)
    c0, c1, c2 = QKV_WIDTH, QKV_WIDTH + QK_WIDTH, QKV_WIDTH + QK_WIDTH + FOURIER_WIDTH
    z_ref[...] = _dot(h, w_ref[:, c0:c1]).astype(BF16)
    f_ref[...] = _dot(h, w_ref[:, c1:c2]).astype(BF16)
    ab_ref[...] = _dot(h, w_ref[:, c2:W_CAT_COLS])


def _inproj_call(x, mod3, mod_row0, mod_per_batch, g_mix, w_cat, w_conv, tm, seq_len):
    b, n, d = x.shape
    assert seq_len & (seq_len - 1) == 0 and n % tm == 0 and tm % HALO == 0
    tok = lambda w: pl.BlockSpec((None, tm, w), lambda i, t: (i, t, 0))
    per = tm // HALO
    return pl.pallas_call(
        functools.partial(_inproj_kernel, tm=tm, seq_len=seq_len),
        grid=(b, n // tm),
        in_specs=[pl.BlockSpec((None, HALO, d), lambda i, t: (i, jnp.maximum(t * per - 1, 0), 0)),
                  tok(d),
                  pl.BlockSpec((None, HALO, d), lambda i, t: (i, jnp.minimum((t + 1) * per, n // HALO - 1), 0)),
                  pl.BlockSpec((None, 1, 6 * d), lambda i, t: (mod_row0 + (i if mod_per_batch else 0), 0, 0)),
                  pl.BlockSpec((1, d), lambda i, t: (0, 0)),
                  pl.BlockSpec((d, W_CAT_COLS), lambda i, t: (0, 0)),
                  pl.BlockSpec((3, QKV_WIDTH), lambda i, t: (0, 0))],
        out_specs=[tok(QKV_WIDTH), tok(QK_WIDTH), tok(FOURIER_WIDTH), tok(LANES)],
        out_shape=[jax.ShapeDtypeStruct((b, n, QKV_WIDTH), BF16),
                   jax.ShapeDtypeStruct((b, n, QK_WIDTH), BF16),
                   jax.ShapeDtypeStruct((b, n, FOURIER_WIDTH), BF16),
                   jax.ShapeDtypeStruct((b, n, LANES), F32)],
        compiler_params=_params(),
    )(x, x, x, mod3, g_mix.reshape(1, d), w_cat, w_conv)


LEVEL_BLOCKS = tuple(2 ** e for e in range(int(np.log2(CHUNK))))


def _tri_inverse(a_list, eye, level_masks):
    ds = [eye - jnp.where(level_masks[0], a, 0.0) for a in a_list]
    for mask in level_masks[1:]:
        ls = [jnp.where(mask, a, 0.0).astype(BF16) for a in a_list]
        dbs = [d.astype(BF16) for d in ds]
        ms = [_dot(db, l).astype(BF16) for db, l in zip(dbs, ls)]
        ds = [d - _dot(m, db) for d, m, db in zip(ds, ms, dbs)]
    return ds


def _delta_kernel(*refs, n, hp_n, unroll, has_state):
    if has_state:
        (gp_ref, q_ref, k_ref, v_ref, z_ref, abt_ref, go_ref, tric_ref, trir_ref, sf0_ref, sb0_ref,
         og_ref, sf_ref, sb_ref, lhs_sc, nst_sc, gls, osc) = refs
    else:
        (gp_ref, q_ref, k_ref, v_ref, z_ref, abt_ref, go_ref, tric_ref, trir_ref,
         og_ref, sf_ref, sb_ref, lhs_sc, nst_sc, gls, osc) = refs
    hg = pl.program_id(1)
    nc = n // CHUNK
    rblk = 256
    nblk = n // rblk

    ci = lax.broadcasted_iota(jnp.int32, (CHUNK, CHUNK), 0)
    cj = lax.broadcasted_iota(jnp.int32, (CHUNK, CHUNK), 1)
    eye = (ci == cj).astype(F32)
    level_masks = [((ci // (2 * b)) == (cj // (2 * b))) & ((ci // b) != (cj // b))
                   for b in LEVEL_BLOCKS]
    incl = (ci >= cj, ci <= cj)
    strict = (ci > cj, ci < cj)
    n_half = N_DIR * N_HEADS

    def head_cols(hp):
        return slice(hp * HEAD_DIM, (hp + 1) * HEAD_DIM)

    def split3_lanes(x):
        hi = x.astype(BF16)
        r1 = x - hi.astype(F32)
        lo = r1.astype(BF16)
        lo2 = (r1 - lo.astype(F32)).astype(BF16)
        return jnp.concatenate([hi, lo, lo2], axis=1)

    def select_col(x, col):
        lanes = lax.broadcasted_iota(jnp.int32, x.shape, 1)
        return jnp.sum(jnp.where(lanes == col, x, 0.0), axis=-1, keepdims=True)

    def chain_front(hp, c, d, q, k, v, kk, qk, cum, beta_all):
        col = d * N_HEADS + hg * hp_n + hp
        g_col = jnp.broadcast_to(select_col(cum[d], col), (CHUNK, LANES))
        beta = select_col(beta_all, n_half + col)
        a_row = abt_ref[c, pl.ds(col, 1), :]
        la_row = -jnp.exp(gp_ref[pl.ds(col, 1), :]) * _softplus(a_row + gp_ref[pl.ds(n_half + col, 1), :])
        g_row = _dot(split3_lanes(jnp.broadcast_to(la_row, (8, CHUNK))), trir_ref[d])[0:1, :]
        g_tot = g_col[CHUNK - 1:CHUNK, :] if d == 0 else g_col[0:1, :]
        decay = jnp.exp(jnp.where(incl[d], g_col - g_row, -jnp.inf))
        eg = jnp.exp(g_col)
        gls[hp, d, c] = jnp.broadcast_to(jnp.exp(g_tot), (8, LANES))
        a = jnp.where(strict[d], beta * decay * kk, 0.0)
        rhs = jnp.concatenate([beta * v, (beta * eg) * k], axis=1).astype(BF16)
        lhs2 = jnp.concatenate([(k * jnp.exp(g_tot - g_col)).T.astype(BF16), (qk * decay).astype(BF16)], axis=0)
        return a, rhs, lhs2, q * eg

    def prepare(i, carry):
        chains = []
        for hp in range(hp_n):
            for j in range(unroll):
                c = i * unroll + j
                r0 = pl.multiple_of(c * CHUNK, CHUNK)
                qb = q_ref[pl.ds(r0, CHUNK), head_cols(hp)]
                kb = k_ref[pl.ds(r0, CHUNK), head_cols(hp)]
                q = qb.astype(F32)
                k = kb.astype(F32)
                v = v_ref[pl.ds(r0, CHUNK), head_cols(hp)].astype(F32)
                kq = lax.dot_general(jnp.concatenate([kb, qb], axis=0), kb,
                                     (((1,), (1,)), ((), ())), preferred_element_type=F32)
                raw = abt_ref[c]
                log_a = -jnp.exp(gp_ref[0:n_half, :]) * _softplus(raw[0:n_half, :] + gp_ref[n_half:, :])
                gates = split3_lanes(jnp.concatenate([log_a, jax.nn.sigmoid(raw[n_half:, :])], axis=0))
                gates = jnp.concatenate([gates, jnp.zeros((LANES - 2 * n_half, 3 * CHUNK), BF16)], axis=0)
                cols3 = lax.dot_general(tric_ref[...], gates, (((1,), (1,)), ((), ())),
                                        preferred_element_type=F32)
                cum_f = cols3[0:CHUNK, :]
                beta_all = cols3[CHUNK:, :]
                cum = (cum_f, cum_f[CHUNK - 1:CHUNK, :] - cum_f + beta_all)
                for d in range(N_DIR):
                    chains.append((hp, j, c, d) + chain_front(hp, c, d, q, k, v, kq[0:CHUNK, :], kq[CHUNK:, :],
                                                              cum, beta_all))
        t_invs = _tri_inverse([ch[4] for ch in chains], eye, level_masks)
        o_local = {}
        for (hp, j, c, d, _, rhs, lhs2, qg), t_inv in zip(chains, t_invs):
            sol = _dot(t_inv.astype(BF16), rhs).astype(BF16)
            x = _dot(lhs2, sol)
            nst_sc[hp, d, c] = x[0:HEAD_DIM, 0:HEAD_DIM]
            lhs_sc[hp, d, c, 0:HEAD_DIM, :] = (-x[0:HEAD_DIM, HEAD_DIM:]).astype(BF16)
            lhs_sc[hp, d, c, HEAD_DIM:, :] = (qg - x[HEAD_DIM:, HEAD_DIM:]).astype(BF16)
            o_local[(hp, j, d)] = x[HEAD_DIM:, 0:HEAD_DIM]
        for hp in range(hp_n):
            for j in range(unroll):
                c = i * unroll + j
                r0 = pl.multiple_of(c * CHUNK, CHUNK)
                osc[hp, pl.ds(r0, CHUNK), :] = o_local[(hp, j, 0)] + o_local[(hp, j, 1)]
        return carry

    lax.fori_loop(0, nc // unroll, prepare, 0)

    def scan_step(hp, c, d, s):
        r0 = pl.multiple_of(c * CHUNK, CHUNK)
        r = _dot(lhs_sc[hp, d, c], s.astype(BF16))
        osc[hp, pl.ds(r0, CHUNK), :] += r[HEAD_DIM:, :]
        return gls[hp, d, c][0:1, :] * s + r[0:HEAD_DIM, :] + nst_sc[hp, d, c]

    def scan_body(i, carry):
        out = []
        for hp in range(hp_n):
            out.append(scan_step(hp, i, 0, carry[2 * hp]))
            out.append(scan_step(hp, nc - 1 - i, 1, carry[2 * hp + 1]))
        return tuple(out)

    init = []
    for hp in range(hp_n):
        if has_state:
            init += [sf0_ref[hp], sb0_ref[hp]]
        else:
            init += [jnp.zeros((HEAD_DIM, HEAD_DIM), F32), jnp.zeros((HEAD_DIM, HEAD_DIM), F32)]
    fin = lax.fori_loop(0, nc, scan_body, tuple(init))
    for hp in range(hp_n):
        sf_ref[hp] = fin[2 * hp]
        sb_ref[hp] = fin[2 * hp + 1]

    def post_body(i, carry):
        r0 = pl.multiple_of(i * rblk, rblk)
        for hp in range(hp_n):
            o = osc[hp, pl.ds(r0, rblk), :]
            y = o * lax.rsqrt(jnp.mean(o * o, axis=-1, keepdims=True) + RMS_EPS) * go_ref[...]
            zz = z_ref[pl.ds(r0, rblk), head_cols(hp)].astype(F32)
            og_ref[pl.ds(r0, rblk), head_cols(hp)] = (y * _silu(zz)).astype(BF16)
        return carry

    lax.fori_loop(0, nblk, post_body, 0)


def _tri_tables():
    i = np.arange(CHUNK)
    low = (i[:, None] >= i[None, :]).astype(np.float32)
    up = low.T
    tric = np.concatenate([np.concatenate([m, m, m], axis=1) for m in (low, np.eye(CHUNK))], axis=0)
    trir = np.stack([np.concatenate([m, m, m], axis=0) for m in (up, low)])
    return jnp.asarray(tric).astype(BF16), jnp.asarray(trir).astype(BF16)


def _delta_call(qkv, z, abt, a_log, dt_bias, g_o, s0_f, s0_b, *, heads_per_step, unroll):
    b, n, _ = qkv.shape
    nc = n // CHUNK
    hp_n = heads_per_step
    groups = N_HEADS // hp_n
    wide = hp_n * HEAD_DIM
    has_state = s0_f is not None
    tric, trir = _tri_tables()
    col = lambda off: pl.BlockSpec((None, n, wide), lambda i, g: (i, 0, off + g))
    st = pl.BlockSpec((None, None, hp_n, HEAD_DIM, HEAD_DIM), lambda i, g: (i, 0, g, 0, 0))
    n_gate = 2 * N_DIR * N_HEADS
    gate_params = jnp.broadcast_to(jnp.concatenate([a_log.reshape(-1), dt_bias.reshape(-1)])[:, None],
                                   (n_gate, CHUNK))
    in_specs = [pl.BlockSpec((n_gate, CHUNK), lambda i, g: (0, 0)),
                col(0), col(groups), col(2 * groups), col(0),
                pl.BlockSpec((None, nc, n_gate, CHUNK), lambda i, g: (i, 0, 0, 0)),
                pl.BlockSpec((1, HEAD_DIM), lambda i, g: (0, 0)),
                pl.BlockSpec((2 * CHUNK, 3 * CHUNK), lambda i, g: (0, 0)),
                pl.BlockSpec((N_DIR, 3 * CHUNK, CHUNK), lambda i, g: (0, 0, 0))]
    args = [gate_params, qkv, qkv, qkv, z, abt, g_o.reshape(1, HEAD_DIM), tric, trir]
    if has_state:
        in_specs += [st, st]
        args += [s0_f, s0_b]
    state_shape = jax.ShapeDtypeStruct((b, 1, N_HEADS, HEAD_DIM, HEAD_DIM), F32)
    return pl.pallas_call(
        functools.partial(_delta_kernel, n=n, hp_n=hp_n, unroll=unroll, has_state=has_state),
        grid=(b, groups),
        in_specs=in_specs,
        out_specs=[col(0), st, st],
        out_shape=[jax.ShapeDtypeStruct((b, n, QK_WIDTH), BF16), state_shape, state_shape],
        scratch_shapes=[pltpu.VMEM((hp_n, N_DIR, nc, HEAD_DIM + CHUNK, HEAD_DIM), BF16),
                        pltpu.VMEM((hp_n, N_DIR, nc, HEAD_DIM, HEAD_DIM), F32),
                        pltpu.VMEM((hp_n, N_DIR, nc, 8, LANES), F32),
                        pltpu.VMEM((hp_n, n, HEAD_DIM), F32)],
        compiler_params=_params(),
    )(*args)


def _dft_tables(n, scale=1.0):
    idx = np.arange(n)
    ang = 2.0 * np.pi * ((idx[:, None] * idx[None, :]) % n) / n
    return (np.cos(ang) * scale).astype(np.float32), (np.sin(ang) * scale).astype(np.float32)


def _fno_prompt_kernel(f_ref, wy_ref, cn_ref, sn_ref, o_ref):
    cn = cn_ref[...]
    sn = sn_ref[...]
    for g in range(N_GROUPS):
        y = _dot(f_ref[:, g * HEAD_DIM:(g + 1) * HEAD_DIM], wy_ref[g])
        o = _dot(cn, y[:, 0:HEAD_DIM].astype(BF16)) + _dot(sn, y[:, HEAD_DIM:].astype(BF16))
        o_ref[:, g * HEAD_DIM:(g + 1) * HEAD_DIM] = o.astype(BF16)


def _fno_prompt_call(f, wy):
    b, n, w = f.shape
    cn, sn = _dft_tables(n, (n * HEAD_DIM) ** -0.5)
    cn = jnp.asarray(cn).astype(BF16)
    sn_neg = jnp.asarray(-sn).astype(BF16)
    return pl.pallas_call(
        _fno_prompt_kernel,
        grid=(b,),
        in_specs=[pl.BlockSpec((None, n, w), lambda i: (i, 0, 0)),
                  pl.BlockSpec((N_GROUPS, HEAD_DIM, 2 * HEAD_DIM), lambda i: (0, 0, 0)),
                  pl.BlockSpec((n, n), lambda i: (0, 0)),
                  pl.BlockSpec((n, n), lambda i: (0, 0))],
        out_specs=pl.BlockSpec((None, n, w), lambda i: (i, 0, 0)),
        out_shape=jax.ShapeDtypeStruct((b, n, w), BF16),
        compiler_params=_params(),
    )(f, wy, cn, sn_neg)


COL_UNROLL = 2
ROW_UNROLL = 8
ROW_PITCH = GRID_W + 8


def _fno_grid_kernel(f_ref, wy_ref, bdc_ref, bds_ref, crs_ref, o_ref, zr_sc, zi_sc, o_sc, *, n):
    tb = bdc_ref.shape[0]
    two = 2 * HEAD_DIM
    rows = n // GRID_W
    rows_per_block = tb // GRID_W

    def col_body(i, carry):
        blocks = [i * COL_UNROLL + j for j in range(COL_UNROLL)]
        ys = [_dot(f_ref[pl.ds(pl.multiple_of(blk * tb, tb), tb), :], wy_ref[...]).astype(BF16)
              for blk in blocks]
        zs = [_dot(bdc_ref[...], y[:, 0:two]) + _dot(bds_ref[...], y[:, two:]) for y in ys]
        for blk, z in zip(blocks, zs):
            for rr in range(rows_per_block):
                dst = pl.ds(pl.multiple_of((blk * rows_per_block + rr) * ROW_PITCH, 8), GRID_W)
                zr_sc[dst, :] = z[rr * GRID_W:(rr + 1) * GRID_W, 0:HEAD_DIM]
                zi_sc[dst, :] = z[rr * GRID_W:(rr + 1) * GRID_W, HEAD_DIM:]
        return carry

    lax.fori_loop(0, n // (tb * COL_UNROLL), col_body, 0)

    def row_body(i, carry):
        cols = [pl.ds(i * ROW_UNROLL + j, rows, stride=ROW_PITCH) for j in range(ROW_UNROLL)]
        zs = [jnp.concatenate([zr_sc[col, :], zi_sc[col, :]], axis=0).astype(BF16) for col in cols]
        outs = [_dot(crs_ref[...], z) for z in zs]
        for col, o in zip(cols, outs):
            o_sc[col, :] = o
        return carry

    lax.fori_loop(0, GRID_W // ROW_UNROLL, row_body, 0)

    def out_body(r, carry):
        src = pl.ds(pl.multiple_of(r * ROW_PITCH, 8), GRID_W)
        o_ref[pl.ds(pl.multiple_of(r * GRID_W, GRID_W), GRID_W), :] = o_sc[src, :].astype(BF16)
        return carry

    lax.fori_loop(0, rows, out_body, 0)


def _fno_grid_call(f, wy4):
    b, n, w = f.shape
    rows = n // GRID_W
    tb = 256
    cw, sw = _dft_tables(GRID_W)
    rep = np.eye(tb // GRID_W, dtype=np.float32)
    bdc = jnp.asarray(np.kron(rep, cw)).astype(BF16)
    bds = jnp.asarray(np.kron(rep, sw)).astype(BF16)
    cr, sr = _dft_tables(rows, (n * HEAD_DIM) ** -0.5)
    crs = jnp.asarray(np.concatenate([cr, sr], axis=1)).astype(BF16)
    const = lambda s: pl.BlockSpec(s, lambda i, g: (0,) * len(s))
    return pl.pallas_call(
        functools.partial(_fno_grid_kernel, n=n),
        grid=(b, N_GROUPS),
        in_specs=[pl.BlockSpec((None, n, HEAD_DIM), lambda i, g: (i, 0, g)),
                  pl.BlockSpec((None, HEAD_DIM, 4 * HEAD_DIM), lambda i, g: (g, 0, 0)),
                  const((tb, tb)), const((tb, tb)), const((rows, 2 * rows))],
        out_specs=pl.BlockSpec((None, n, HEAD_DIM), lambda i, g: (i, 0, g)),
        out_shape=jax.ShapeDtypeStruct((b, n, w), BF16),
        scratch_shapes=[pltpu.VMEM((rows * ROW_PITCH, HEAD_DIM), F32)] * 3,
        compiler_params=_params(),
    )(f, wy4, bdc, bds, crs)


FF_BLOCK = 256


def _ffn_kernel(x_ref, og_ref, fo_ref, mod_ref, woa_ref, wob_ref, gffn_ref, wg_ref, wu_ref, wd_ref,
                gfin_ref, y_ref):
    d = D_MODEL
    gate1 = mod_ref[:, 2 * d:3 * d]
    shift2 = mod_ref[:, 3 * d:4 * d]
    scale2 = mod_ref[:, 4 * d:5 * d]
    gate2 = mod_ref[:, 5 * d:6 * d]
    mo = _dot(og_ref[...], woa_ref[...]) + _dot(fo_ref[...], wob_ref[...])
    x1 = x_ref[...] + gate1 * mo
    hn = x1 * lax.rsqrt(jnp.mean(x1 * x1, axis=-1, keepdims=True) + RMS_EPS) * gffn_ref[...]
    h2 = (hn * (1.0 + scale2) + shift2).astype(BF16)
    acc = jnp.zeros(x1.shape, F32)
    for j in range(D_FF // FF_BLOCK):
        sl = slice(j * FF_BLOCK, (j + 1) * FF_BLOCK)
        gt = _dot(h2, wg_ref[:, sl])
        up = _dot(h2, wu_ref[:, sl])
        acc = acc + _dot((_silu(gt) * up).astype(BF16), wd_ref[sl, :])
    x2 = x1 + gate2 * acc
    y_ref[...] = x2 * lax.rsqrt(jnp.mean(x2 * x2, axis=-1, keepdims=True) + RMS_EPS) * gfin_ref[...]


def _ffn_call(x, og, fo, mod3, mod_row0, w_out_a, w_out_b, g_ffn, w_g, w_u, w_down, g_final, tm):
    b, n, d = x.shape
    tok = lambda w: pl.BlockSpec((None, tm, w), lambda i, t: (i, t, 0))
    const = lambda s: pl.BlockSpec(s, lambda i, t: (0,) * len(s), pipeline_mode=pl.Buffered(1))
    return pl.pallas_call(
        _ffn_kernel,
        grid=(b, n // tm),
        in_specs=[tok(d), tok(QK_WIDTH), tok(FOURIER_WIDTH),
                  pl.BlockSpec((None, 1, 6 * d), lambda i, t: (mod_row0 + i, 0, 0)),
                  const((QK_WIDTH, d)), const((FOURIER_WIDTH, d)), const((1, d)),
                  const((d, D_FF)), const((d, D_FF)), const((D_FF, d)), const((1, d))],
        out_specs=tok(d),
        out_shape=jax.ShapeDtypeStruct((b, n, d), F32),
        compiler_params=_params(),
    )(x, og, fo, mod3, w_out_a, w_out_b, g_ffn.reshape(1, d), w_g, w_u, w_down, g_final.reshape(1, d))


def _chunk_transposed(ab):
    b, n, _ = ab.shape
    g = ab[:, :, 0:2 * N_DIR * N_HEADS].reshape(b, n // CHUNK, CHUNK, 2 * N_DIR * N_HEADS)
    return jnp.swapaxes(g, 2, 3)


def kernel(x_prompt, x_sample, c, state_dn_fwd, state_dn_bwd, c_ctx, w_ada, b_ada, g_mix, w_in, w_conv,
           a_log, dt_bias, g_o, w_fno, w_out, g_ffn, w_gu, w_down, g_final):
    d = D_MODEL
    bp, np_, _ = x_prompt.shape
    bs, ns, _ = x_sample.shape
    l = 0

    wi = w_in[l]
    n_gate = 2 * N_DIR * N_HEADS
    g0 = QKV_WIDTH + QK_WIDTH
    w_cat = jnp.concatenate([wi[:, 0:g0], wi[:, g0 + n_gate:], wi[:, g0:g0 + n_gate],
                             jnp.zeros((d, LANES - n_gate), F32)], axis=1).astype(BF16)
    w_out_b16 = w_out[l].astype(BF16)
    w_out_a, w_out_b = w_out_b16[0:QK_WIDTH], w_out_b16[QK_WIDTH:]
    w_g = w_gu[l][:, 0:D_FF].astype(BF16)
    w_u = w_gu[l][:, D_FF:].astype(BF16)
    w_dn = w_down[l].astype(BF16)

    cond = jnp.concatenate([c_ctx[None, :], c, jnp.zeros((16 - 1 - bs, d), F32)], axis=0)
    mod = _mod_call(cond, w_ada[l], b_ada[l])
    mod3 = mod.reshape(16, 1, 6 * d)

    cc, sc = _dft_tables(HEAD_DIM)
    wy_p, wy_g = _fno_w_call(w_fno[l], jnp.asarray(cc), jnp.asarray(sc))

    xp = x_prompt.reshape(1, bp * np_, d)
    qkv, z, f, ab = _inproj_call(xp, mod3, 0, False, g_mix[l], w_cat, w_conv[l], 512, np_)
    qkv = qkv.reshape(bp, np_, QKV_WIDTH)
    z = z.reshape(bp, np_, QK_WIDTH)
    f = f.reshape(bp, np_, FOURIER_WIDTH)
    ab = ab.reshape(bp, np_, LANES)
    og, new_f, new_b = _delta_call(qkv, z, _chunk_transposed(ab), a_log[l], dt_bias[l], g_o[l], None, None,
                                   heads_per_step=4, unroll=2)
    fo = _fno_prompt_call(f, wy_p)
    y_prompt = _ffn_call(xp, og.reshape(1, bp * np_, QK_WIDTH), fo.reshape(1, bp * np_, FOURIER_WIDTH),
                         mod3, 0, w_out_a, w_out_b, g_ffn[l], w_g, w_u, w_dn, g_final, 512)
    y_prompt = y_prompt.reshape(bp, np_, d)

    qkv, z, f, ab = _inproj_call(x_sample, mod3, 1, True, g_mix[l], w_cat, w_conv[l], 512, ns)
    og, _, _ = _delta_call(qkv, z, _chunk_transposed(ab), a_log[l], dt_bias[l], g_o[l],
                           state_dn_fwd[:, l:l + 1], state_dn_bwd[:, l:l + 1], heads_per_step=1, unroll=8)
    fo = _fno_grid_call(f, wy_g)
    y_sample = _ffn_call(x_sample, og, fo, mod3, 1, w_out_a, w_out_b, g_ffn[l], w_g, w_u, w_dn, g_final, 512)

    return (y_prompt, y_sample, new_f, new_b)
```

```python
import functools

import numpy as np
import jax
import jax.numpy as jnp
from jax import lax
from jax.experimental import pallas as pl
from jax.experimental.pallas import tpu as pltpu

D_MODEL = 1024
N_HEADS = 4
HEAD_DIM = 128
QK_WIDTH = N_HEADS * HEAD_DIM
QKV_WIDTH = 3 * QK_WIDTH
N_GROUPS = 4
FOURIER_WIDTH = N_GROUPS * HEAD_DIM
N_DIR = 2
GRID_W = 64
CHUNK = 128
D_FF = 2816
RMS_EPS = 1e-6
LANES = 128
W_CAT_COLS = QKV_WIDTH + QK_WIDTH + FOURIER_WIDTH + LANES
VMEM_LIMIT = 56 * 1024 * 1024

F32 = jnp.float32
BF16 = jnp.bfloat16
HIGHEST = lax.Precision.HIGHEST


def _dot(a, b):
    return jnp.dot(a, b, preferred_element_type=F32)


def _silu(x):
    return x * jax.nn.sigmoid(x)


def _softplus(x):
    return jnp.maximum(x, 0.0) + jnp.log1p(jnp.exp(-jnp.abs(x)))


def _params(**kw):
    return pltpu.CompilerParams(vmem_limit_bytes=VMEM_LIMIT, **kw)


def _mod_kernel(cond_ref, w_ref, b_ref, o_ref):
    s = _silu(cond_ref[...]).astype(BF16)
    o_ref[...] = _dot(s, w_ref[...].astype(BF16)) + b_ref[...]


def _mod_call(cond, w_ada, b_ada):
    rows, d = cond.shape
    cols = w_ada.shape[1]
    tn = 1536
    return pl.pallas_call(
        _mod_kernel,
        grid=(cols // tn,),
        in_specs=[pl.BlockSpec((rows, d), lambda j: (0, 0)),
                  pl.BlockSpec((d, tn), lambda j: (0, j)),
                  pl.BlockSpec((1, tn), lambda j: (0, j))],
        out_specs=pl.BlockSpec((rows, tn), lambda j: (0, j)),
        out_shape=jax.ShapeDtypeStruct((rows, cols), F32),
        compiler_params=_params(),
    )(cond, w_ada, b_ada.reshape(1, cols))


def _fno_w_kernel(w_ref, cc_ref, sc_ref, wp_ref, wg_ref):
    w = w_ref[...]
    cw = jnp.dot(cc_ref[...], w, precision=HIGHEST, preferred_element_type=F32)
    sw = jnp.dot(sc_ref[...], w, precision=HIGHEST, preferred_element_type=F32)
    wp_ref[...] = jnp.concatenate([cw, sw], axis=1).astype(BF16)
    wg_ref[...] = jnp.concatenate([cw, -sw, -sw, -cw], axis=1).astype(BF16)


def _fno_w_call(w_fno, cc, sc):
    g, c, _ = w_fno.shape
    return pl.pallas_call(
        _fno_w_kernel,
        grid=(g,),
        in_specs=[pl.BlockSpec((None, c, c), lambda i: (i, 0, 0)),
                  pl.BlockSpec((c, c), lambda i: (0, 0)),
                  pl.BlockSpec((c, c), lambda i: (0, 0))],
        out_specs=[pl.BlockSpec((None, c, 2 * c), lambda i: (i, 0, 0)),
                   pl.BlockSpec((None, c, 4 * c), lambda i: (i, 0, 0))],
        out_shape=[jax.ShapeDtypeStruct((g, c, 2 * c), BF16),
                   jax.ShapeDtypeStruct((g, c, 4 * c), BF16)],
        compiler_params=_params(),
    )(w_fno, cc, sc)


HALO = 8
CONV_COLS = 2 * HEAD_DIM


def _inproj_kernel(xp_ref, x_ref, xn_ref, mod_ref, g_ref, w_ref, wc_ref, qkv_ref, z_ref, f_ref, ab_ref, *,
                   tm, seq_len):
    t = pl.program_id(1)
    x = jnp.concatenate([xp_ref[...], x_ref[...], xn_ref[...]], axis=0)
    y = x * lax.rsqrt(jnp.mean(x * x, axis=-1, keepdims=True) + RMS_EPS) * g_ref[...]
    shift1 = mod_ref[:, 0:D_MODEL]
    scale1 = mod_ref[:, D_MODEL:2 * D_MODEL]
    h_ext = (y * (1.0 + scale1) + shift1).astype(BF16)
    h = h_ext[HALO:HALO + tm, :]

    rows = lax.broadcasted_iota(jnp.int32, (tm, HEAD_DIM), 0)
    pos = (t * tm + rows) & (seq_len - 1)
    first = pos == 0
    last = pos == seq_len - 1
    for j in range(QKV_WIDTH // CONV_COLS):
        p_wide = _dot(h_ext, w_ref[:, j * CONV_COLS:(j + 1) * CONV_COLS])
        for hh in range(CONV_COLS // HEAD_DIM):
            head = j * (CONV_COLS // HEAD_DIM) + hh
            cols = slice(head * HEAD_DIM, (head + 1) * HEAD_DIM)
            p_ext = p_wide[:, hh * HEAD_DIM:(hh + 1) * HEAD_DIM]
            p = p_ext[HALO:HALO + tm, :]
            p_prev = jnp.where(rows == 0, p_ext[HALO - 1:HALO, :], pltpu.roll(p, 1, 0))
            p_next = jnp.where(rows == tm - 1, p_ext[HALO + tm:HALO + tm + 1, :], pltpu.roll(p, tm - 1, 0))
            p_prev = jnp.where(first, 0.0, p_prev)
            p_next = jnp.where(last, 0.0, p_next)
            ch = _silu(wc_ref[0:1, cols] * p_prev + wc_ref[1:2, cols] * p + wc_ref[2:3, cols] * p_next)
            if head < 2 * N_HEADS:
                ch = ch * lax.rsqrt(jnp.sum(ch * ch, axis=-1, keepdims=True) + 1e-6)
            if head < N_HEADS:
                ch = ch * (HEAD_DIM ** -0.5)
            qkv_ref[:, cols] = ch.astype(BF16)
    c0, c1, c2 = QKV_WIDTH, QKV_WIDTH + QK_WIDTH, QKV_WIDTH + QK_WIDTH + FOURIER_WIDTH
    z_ref[...] = _dot(h, w_ref[:, c0:c1]).astype(BF16)
    f_ref[...] = _dot(h, w_ref[:, c1:c2]).astype(BF16)
    ab_ref[...] = _dot(h, w_ref[:, c2:W_CAT_COLS])


def _inproj_call(x, mod3, mod_row0, mod_per_batch, g_mix, w_cat, w_conv, tm, seq_len):
    b, n, d = x.shape
    assert seq_len & (seq_len - 1) == 0 and n % tm == 0 and tm % HALO == 0
    tok = lambda w: pl.BlockSpec((None, tm, w), lambda i, t: (i, t, 0))
    per = tm // HALO
    return pl.pallas_call(
        functools.partial(_inproj_kernel, tm=tm, seq_len=seq_len),
        grid=(b, n // tm),
        in_specs=[pl.BlockSpec((None, HALO, d), lambda i, t: (i, jnp.maximum(t * per - 1, 0), 0)),
                  tok(d),
                  pl.BlockSpec((None, HALO, d), lambda i, t: (i, jnp.minimum((t + 1) * per, n // HALO - 1), 0)),
                  pl.BlockSpec((None, 1, 6 * d), lambda i, t: (mod_row0 + (i if mod_per_batch else 0), 0, 0)),
                  pl.BlockSpec((1, d), lambda i, t: (0, 0)),
                  pl.BlockSpec((d, W_CAT_COLS), lambda i, t: (0, 0)),
                  pl.BlockSpec((3, QKV_WIDTH), lambda i, t: (0, 0))],
        out_specs=[tok(QKV_WIDTH), tok(QK_WIDTH), tok(FOURIER_WIDTH), tok(LANES)],
        out_shape=[jax.ShapeDtypeStruct((b, n, QKV_WIDTH), BF16),
                   jax.ShapeDtypeStruct((b, n, QK_WIDTH), BF16),
                   jax.ShapeDtypeStruct((b, n, FOURIER_WIDTH), BF16),
                   jax.ShapeDtypeStruct((b, n, LANES), F32)],
        compiler_params=_params(),
    )(x, x, x, mod3, g_mix.reshape(1, d), w_cat, w_conv)


LEVEL_BLOCKS = tuple(2 ** e for e in range(int(np.log2(CHUNK))))


def _tri_inverse(a_list, eye, level_masks):
    ds = [eye - jnp.where(level_masks[0], a, 0.0) for a in a_list]
    for mask in level_masks[1:]:
        ls = [jnp.where(mask, a, 0.0).astype(BF16) for a in a_list]
        dbs = [d.astype(BF16) for d in ds]
        ms = [_dot(db, l).astype(BF16) for db, l in zip(dbs, ls)]
        ds = [d - _dot(m, db) for d, m, db in zip(ds, ms, dbs)]
    return ds


def _delta_kernel(*refs, n, hp_n, unroll, has_state):
    if has_state:
        (gp_ref, q_ref, k_ref, v_ref, z_ref, abt_ref, go_ref, trir_ref, sf0_ref, sb0_ref,
         og_ref, sf_ref, sb_ref, lhs_sc, nst_sc, gls, osc) = refs
    else:
        (gp_ref, q_ref, k_ref, v_ref, z_ref, abt_ref, go_ref, trir_ref,
         og_ref, sf_ref, sb_ref, lhs_sc, nst_sc, gls, osc) = refs
    hg = pl.program_id(1)
    nc = n // CHUNK

    ci = lax.broadcasted_iota(jnp.int32, (CHUNK, CHUNK), 0)
    cj = lax.broadcasted_iota(jnp.int32, (CHUNK, CHUNK), 1)
    eye = (ci == cj).astype(F32)
    level_masks = [((ci // (2 * b)) == (cj // (2 * b))) & ((ci // b) != (cj // b))
                   for b in LEVEL_BLOCKS]
    incl = (ci >= cj, ci <= cj)
    strict = (ci > cj, ci < cj)
    n_half = N_DIR * N_HEADS
    gate_rows = lax.broadcasted_iota(jnp.int32, (n_half, CHUNK), 0)

    def head_cols(hp):
        return slice(hp * HEAD_DIM, (hp + 1) * HEAD_DIM)

    def split3_lanes(x):
        hi = x.astype(BF16)
        r1 = x - hi.astype(F32)
        lo = r1.astype(BF16)
        lo2 = (r1 - lo.astype(F32)).astype(BF16)
        return jnp.concatenate([hi, lo, lo2], axis=1)

    def select_col(x, col):
        lanes = lax.broadcasted_iota(jnp.int32, x.shape, 1)
        return jnp.sum(jnp.where(lanes == col, x, 0.0), axis=-1, keepdims=True)

    def chain_front(hp, c, d, q, k, v, kk, qk, cum_rows, gate_cols):
        gate = d * N_HEADS + hg * hp_n + hp
        g_col = jnp.broadcast_to(select_col(gate_cols, d * n_half + gate), (CHUNK, LANES))
        beta = select_col(gate_cols, N_DIR * n_half + gate)
        g_row = jnp.sum(jnp.where(gate_rows == gate, cum_rows[d], 0.0), axis=0, keepdims=True)
        g_tot = g_col[CHUNK - 1:CHUNK, :] if d == 0 else g_col[0:1, :]
        decay = jnp.exp(jnp.where(incl[d], g_col - g_row, -jnp.inf))
        eg = jnp.exp(g_col)
        gls[hp, d, c] = jnp.broadcast_to(jnp.exp(g_tot), (8, LANES))
        a = jnp.where(strict[d], beta * decay * kk, 0.0)
        rhs = jnp.concatenate([beta * v, (beta * eg) * k], axis=1).astype(BF16)
        lhs2 = jnp.concatenate([(k * jnp.exp(g_tot - g_col)).T.astype(BF16), (qk * decay).astype(BF16)], axis=0)
        return a, rhs, lhs2, q * eg

    def prepare(i, carry):
        chains = []
        for hp in range(hp_n):
            for j in range(unroll):
                c = i * unroll + j
                r0 = pl.multiple_of(c * CHUNK, CHUNK)
                qb = q_ref[pl.ds(r0, CHUNK), head_cols(hp)]
                kb = k_ref[pl.ds(r0, CHUNK), head_cols(hp)]
                q = qb.astype(F32)
                k = kb.astype(F32)
                v = v_ref[pl.ds(r0, CHUNK), head_cols(hp)].astype(F32)
                kq = lax.dot_general(jnp.concatenate([kb, qb], axis=0), kb,
                                     (((1,), (1,)), ((), ())), preferred_element_type=F32)
                raw = abt_ref[c]
                log_a = -jnp.exp(gp_ref[0:n_half, :]) * _softplus(raw[0:n_half, :] + gp_ref[n_half:, :])
                log_a3 = split3_lanes(log_a)
                cum_rows = (_dot(log_a3, trir_ref[0]), _dot(log_a3, trir_ref[1]))
                gate_cols = jnp.concatenate([cum_rows[0], cum_rows[1], jax.nn.sigmoid(raw[n_half:, :]),
                                             jnp.zeros((CHUNK - 3 * n_half, CHUNK), F32)], axis=0).T
                for d in range(N_DIR):
                    chains.append((hp, j, c, d) + chain_front(hp, c, d, q, k, v, kq[0:CHUNK, :], kq[CHUNK:, :],
                                                              cum_rows, gate_cols))
        t_invs = _tri_inverse([ch[4] for ch in chains], eye, level_masks)
        o_local = {}
        for (hp, j, c, d, _, rhs, lhs2, qg), t_inv in zip(chains, t_invs):
            sol = _dot(t_inv.astype(BF16), rhs).astype(BF16)
            x = _dot(lhs2, sol)
            nst_sc[hp, d, c] = x[0:HEAD_DIM, 0:HEAD_DIM]
            lhs_sc[hp, d, c, 0:HEAD_DIM, :] = (-x[0:HEAD_DIM, HEAD_DIM:]).astype(BF16)
            lhs_sc[hp, d, c, HEAD_DIM:, :] = (qg - x[HEAD_DIM:, HEAD_DIM:]).astype(BF16)
            o_local[(hp, j, d)] = x[HEAD_DIM:, 0:HEAD_DIM]
        for hp in range(hp_n):
            for j in range(unroll):
                c = i * unroll + j
                r0 = pl.multiple_of(c * CHUNK, CHUNK)
                osc[hp, pl.ds(r0, CHUNK), :] = o_local[(hp, j, 0)] + o_local[(hp, j, 1)]
        return carry

    lax.fori_loop(0, nc // unroll, prepare, 0)

    def scan_step(hp, c, d, s):
        r0 = pl.multiple_of(c * CHUNK, CHUNK)
        r = _dot(lhs_sc[hp, d, c], s.astype(BF16))
        osc[hp, pl.ds(r0, CHUNK), :] += r[HEAD_DIM:, :]
        return gls[hp, d, c][0:1, :] * s + r[0:HEAD_DIM, :] + nst_sc[hp, d, c]

    def finalize(hp, c):
        r0 = pl.multiple_of(c * CHUNK, CHUNK)
        o = osc[hp, pl.ds(r0, CHUNK), :]
        y = o * lax.rsqrt(jnp.mean(o * o, axis=-1, keepdims=True) + RMS_EPS) * go_ref[...]
        zz = z_ref[pl.ds(r0, CHUNK), head_cols(hp)].astype(F32)
        og_ref[pl.ds(r0, CHUNK), head_cols(hp)] = (y * _silu(zz)).astype(BF16)

    def scan_body(i, carry, finish):
        out = []
        for hp in range(hp_n):
            out.append(scan_step(hp, i, 0, carry[2 * hp]))
            out.append(scan_step(hp, nc - 1 - i, 1, carry[2 * hp + 1]))
        if finish:
            for hp in range(hp_n):
                finalize(hp, i)
                finalize(hp, nc - 1 - i)
        return tuple(out)

    init = []
    for hp in range(hp_n):
        if has_state:
            init += [sf0_ref[hp], sb0_ref[hp]]
        else:
            init += [jnp.zeros((HEAD_DIM, HEAD_DIM), F32), jnp.zeros((HEAD_DIM, HEAD_DIM), F32)]
    half = lax.fori_loop(0, nc // 2, functools.partial(scan_body, finish=False), tuple(init))
    fin = lax.fori_loop(nc // 2, nc, functools.partial(scan_body, finish=True), half)
    for hp in range(hp_n):
        sf_ref[hp] = fin[2 * hp]
        sb_ref[hp] = fin[2 * hp + 1]


def _tri_tables():
    i = np.arange(CHUNK)
    low = (i[:, None] >= i[None, :]).astype(np.float32)
    trir = np.stack([np.concatenate([m, m, m], axis=0) for m in (low.T, low)])
    return jnp.asarray(trir).astype(BF16)


def _delta_call(qkv, z, abt, a_log, dt_bias, g_o, s0_f, s0_b, *, heads_per_step, unroll):
    b, n, _ = qkv.shape
    nc = n // CHUNK
    assert nc % 2 == 0 and nc % unroll == 0
    hp_n = heads_per_step
    groups = N_HEADS // hp_n
    wide = hp_n * HEAD_DIM
    has_state = s0_f is not None
    trir = _tri_tables()
    col = lambda off: pl.BlockSpec((None, n, wide), lambda i, g: (i, 0, off + g))
    st = pl.BlockSpec((None, None, hp_n, HEAD_DIM, HEAD_DIM), lambda i, g: (i, 0, g, 0, 0))
    n_gate = 2 * N_DIR * N_HEADS
    gate_params = jnp.broadcast_to(jnp.concatenate([a_log.reshape(-1), dt_bias.reshape(-1)])[:, None],
                                   (n_gate, CHUNK))
    in_specs = [pl.BlockSpec((n_gate, CHUNK), lambda i, g: (0, 0)),
                col(0), col(groups), col(2 * groups), col(0),
                pl.BlockSpec((None, nc, n_gate, CHUNK), lambda i, g: (i, 0, 0, 0)),
                pl.BlockSpec((1, HEAD_DIM), lambda i, g: (0, 0)),
                pl.BlockSpec((N_DIR, 3 * CHUNK, CHUNK), lambda i, g: (0, 0, 0))]
    args = [gate_params, qkv, qkv, qkv, z, abt, g_o.reshape(1, HEAD_DIM), trir]
    if has_state:
        in_specs += [st, st]
        args += [s0_f, s0_b]
    state_shape = jax.ShapeDtypeStruct((b, 1, N_HEADS, HEAD_DIM, HEAD_DIM), F32)
    return pl.pallas_call(
        functools.partial(_delta_kernel, n=n, hp_n=hp_n, unroll=unroll, has_state=has_state),
        grid=(b, groups),
        in_specs=in_specs,
        out_specs=[col(0), st, st],
        out_shape=[jax.ShapeDtypeStruct((b, n, QK_WIDTH), BF16), state_shape, state_shape],
        scratch_shapes=[pltpu.VMEM((hp_n, N_DIR, nc, HEAD_DIM + CHUNK, HEAD_DIM), BF16),
                        pltpu.VMEM((hp_n, N_DIR, nc, HEAD_DIM, HEAD_DIM), F32),
                        pltpu.VMEM((hp_n, N_DIR, nc, 8, LANES), F32),
                        pltpu.VMEM((hp_n, n, HEAD_DIM), F32)],
        compiler_params=_params(),
    )(*args)


def _dft_tables(n, scale=1.0):
    idx = np.arange(n)
    ang = 2.0 * np.pi * ((idx[:, None] * idx[None, :]) % n) / n
    return (np.cos(ang) * scale).astype(np.float32), (np.sin(ang) * scale).astype(np.float32)


def _fno_prompt_kernel(f_ref, wy_ref, cn_ref, sn_ref, o_ref):
    cn = cn_ref[...]
    sn = sn_ref[...]
    for g in range(N_GROUPS):
        y = _dot(f_ref[:, g * HEAD_DIM:(g + 1) * HEAD_DIM], wy_ref[g])
        o = _dot(cn, y[:, 0:HEAD_DIM].astype(BF16)) + _dot(sn, y[:, HEAD_DIM:].astype(BF16))
        o_ref[:, g * HEAD_DIM:(g + 1) * HEAD_DIM] = o.astype(BF16)


def _fno_prompt_call(f, wy):
    b, n, w = f.shape
    cn, sn = _dft_tables(n, (n * HEAD_DIM) ** -0.5)
    cn = jnp.asarray(cn).astype(BF16)
    sn_neg = jnp.asarray(-sn).astype(BF16)
    return pl.pallas_call(
        _fno_prompt_kernel,
        grid=(b,),
        in_specs=[pl.BlockSpec((None, n, w), lambda i: (i, 0, 0)),
                  pl.BlockSpec((N_GROUPS, HEAD_DIM, 2 * HEAD_DIM), lambda i: (0, 0, 0)),
                  pl.BlockSpec((n, n), lambda i: (0, 0)),
                  pl.BlockSpec((n, n), lambda i: (0, 0))],
        out_specs=pl.BlockSpec((None, n, w), lambda i: (i, 0, 0)),
        out_shape=jax.ShapeDtypeStruct((b, n, w), BF16),
        compiler_params=_params(),
    )(f, wy, cn, sn_neg)


COL_UNROLL = 4
ROW_UNROLL = 8
ROW_PITCH = GRID_W + 8


def _fno_grid_kernel(f_ref, wy_ref, bdc_ref, bds_ref, crs_ref, o_ref, zr_sc, zi_sc, o_sc, *, n):
    tb = bdc_ref.shape[0]
    two = 2 * HEAD_DIM
    rows = n // GRID_W
    rows_per_block = tb // GRID_W

    def col_body(i, carry):
        blocks = [i * COL_UNROLL + j for j in range(COL_UNROLL)]
        ys = [_dot(f_ref[pl.ds(pl.multiple_of(blk * tb, tb), tb), :], wy_ref[...]).astype(BF16)
              for blk in blocks]
        zs = [_dot(bdc_ref[...], y[:, 0:two]) + _dot(bds_ref[...], y[:, two:]) for y in ys]
        for blk, z in zip(blocks, zs):
            for rr in range(rows_per_block):
                dst = pl.ds(pl.multiple_of((blk * rows_per_block + rr) * ROW_PITCH, 8), GRID_W)
                zr_sc[dst, :] = z[rr * GRID_W:(rr + 1) * GRID_W, 0:HEAD_DIM]
                zi_sc[dst, :] = z[rr * GRID_W:(rr + 1) * GRID_W, HEAD_DIM:]
        return carry

    lax.fori_loop(0, n // (tb * COL_UNROLL), col_body, 0)

    def row_body(i, carry):
        cols = [pl.ds(i * ROW_UNROLL + j, rows, stride=ROW_PITCH) for j in range(ROW_UNROLL)]
        zs = [jnp.concatenate([zr_sc[col, :], zi_sc[col, :]], axis=0).astype(BF16) for col in cols]
        outs = [_dot(crs_ref[...], z) for z in zs]
        for col, o in zip(cols, outs):
            o_sc[col, :] = o
        return carry

    lax.fori_loop(0, GRID_W // ROW_UNROLL, row_body, 0)

    def out_body(r, carry):
        src = pl.ds(pl.multiple_of(r * ROW_PITCH, 8), GRID_W)
        o_ref[pl.ds(pl.multiple_of(r * GRID_W, GRID_W), GRID_W), :] = o_sc[src, :].astype(BF16)
        return carry

    lax.fori_loop(0, rows, out_body, 0)


def _fno_grid_call(f, wy4):
    b, n, w = f.shape
    rows = n // GRID_W
    tb = 256
    cw, sw = _dft_tables(GRID_W)
    rep = np.eye(tb // GRID_W, dtype=np.float32)
    bdc = jnp.asarray(np.kron(rep, cw)).astype(BF16)
    bds = jnp.asarray(np.kron(rep, sw)).astype(BF16)
    cr, sr = _dft_tables(rows, (n * HEAD_DIM) ** -0.5)
    crs = jnp.asarray(np.concatenate([cr, sr], axis=1)).astype(BF16)
    const = lambda s: pl.BlockSpec(s, lambda i, g: (0,) * len(s))
    return pl.pallas_call(
        functools.partial(_fno_grid_kernel, n=n),
        grid=(b, N_GROUPS),
        in_specs=[pl.BlockSpec((None, n, HEAD_DIM), lambda i, g: (i, 0, g)),
                  pl.BlockSpec((None, HEAD_DIM, 4 * HEAD_DIM), lambda i, g: (g, 0, 0)),
                  const((tb, tb)), const((tb, tb)), const((rows, 2 * rows))],
        out_specs=pl.BlockSpec((None, n, HEAD_DIM), lambda i, g: (i, 0, g)),
        out_shape=jax.ShapeDtypeStruct((b, n, w), BF16),
        scratch_shapes=[pltpu.VMEM((rows * ROW_PITCH, HEAD_DIM), F32)] * 3,
        compiler_params=_params(),
    )(f, wy4, bdc, bds, crs)


FF_BLOCK = 256


def _ffn_kernel(x_ref, og_ref, fo_ref, mod_ref, woa_ref, wob_ref, gffn_ref, wg_ref, wu_ref, wd_ref,
                gfin_ref, y_ref):
    d = D_MODEL
    gate1 = mod_ref[:, 2 * d:3 * d]
    shift2 = mod_ref[:, 3 * d:4 * d]
    scale2 = mod_ref[:, 4 * d:5 * d]
    gate2 = mod_ref[:, 5 * d:6 * d]
    mo = _dot(og_ref[...], woa_ref[...]) + _dot(fo_ref[...], wob_ref[...])
    x1 = x_ref[...] + gate1 * mo
    hn = x1 * lax.rsqrt(jnp.mean(x1 * x1, axis=-1, keepdims=True) + RMS_EPS) * gffn_ref[...]
    h2 = (hn * (1.0 + scale2) + shift2).astype(BF16)
    acc = jnp.zeros(x1.shape, F32)
    for j in range(D_FF // FF_BLOCK):
        sl = slice(j * FF_BLOCK, (j + 1) * FF_BLOCK)
        gt = _dot(h2, wg_ref[:, sl])
        up = _dot(h2, wu_ref[:, sl])
        acc = acc + _dot((_silu(gt) * up).astype(BF16), wd_ref[sl, :])
    x2 = x1 + gate2 * acc
    y_ref[...] = x2 * lax.rsqrt(jnp.mean(x2 * x2, axis=-1, keepdims=True) + RMS_EPS) * gfin_ref[...]


def _ffn_call(x, og, fo, mod3, mod_row0, w_out_a, w_out_b, g_ffn, w_g, w_u, w_down, g_final, tm):
    b, n, d = x.shape
    tok = lambda w: pl.BlockSpec((None, tm, w), lambda i, t: (i, t, 0))
    const = lambda s: pl.BlockSpec(s, lambda i, t: (0,) * len(s), pipeline_mode=pl.Buffered(1))
    return pl.pallas_call(
        _ffn_kernel,
        grid=(b, n // tm),
        in_specs=[tok(d), tok(QK_WIDTH), tok(FOURIER_WIDTH),
                  pl.BlockSpec((None, 1, 6 * d), lambda i, t: (mod_row0 + i, 0, 0)),
                  const((QK_WIDTH, d)), const((FOURIER_WIDTH, d)), const((1, d)),
                  const((d, D_FF)), const((d, D_FF)), const((D_FF, d)), const((1, d))],
        out_specs=tok(d),
        out_shape=jax.ShapeDtypeStruct((b, n, d), F32),
        compiler_params=_params(),
    )(x, og, fo, mod3, w_out_a, w_out_b, g_ffn.reshape(1, d), w_g, w_u, w_down, g_final.reshape(1, d))


def _chunk_transposed(ab):
    b, n, _ = ab.shape
    g = ab[:, :, 0:2 * N_DIR * N_HEADS].reshape(b, n // CHUNK, CHUNK, 2 * N_DIR * N_HEADS)
    return jnp.swapaxes(g, 2, 3)


def kernel(x_prompt, x_sample, c, state_dn_fwd, state_dn_bwd, c_ctx, w_ada, b_ada, g_mix, w_in, w_conv,
           a_log, dt_bias, g_o, w_fno, w_out, g_ffn, w_gu, w_down, g_final):
    d = D_MODEL
    bp, np_, _ = x_prompt.shape
    bs, ns, _ = x_sample.shape
    l = 0

    wi = w_in[l]
    n_gate = 2 * N_DIR * N_HEADS
    g0 = QKV_WIDTH + QK_WIDTH
    w_cat = jnp.concatenate([wi[:, 0:g0], wi[:, g0 + n_gate:], wi[:, g0:g0 + n_gate],
                             jnp.zeros((d, LANES - n_gate), F32)], axis=1).astype(BF16)
    w_out_b16 = w_out[l].astype(BF16)
    w_out_a, w_out_b = w_out_b16[0:QK_WIDTH], w_out_b16[QK_WIDTH:]
    w_g = w_gu[l][:, 0:D_FF].astype(BF16)
    w_u = w_gu[l][:, D_FF:].astype(BF16)
    w_dn = w_down[l].astype(BF16)

    cond = jnp.concatenate([c_ctx[None, :], c, jnp.zeros((16 - 1 - bs, d), F32)], axis=0)
    mod = _mod_call(cond, w_ada[l], b_ada[l])
    mod3 = mod.reshape(16, 1, 6 * d)

    cc, sc = _dft_tables(HEAD_DIM)
    wy_p, wy_g = _fno_w_call(w_fno[l], jnp.asarray(cc), jnp.asarray(sc))

    xp = x_prompt.reshape(1, bp * np_, d)
    qkv, z, f, ab = _inproj_call(xp, mod3, 0, False, g_mix[l], w_cat, w_conv[l], 512, np_)
    qkv = qkv.reshape(bp, np_, QKV_WIDTH)
    z = z.reshape(bp, np_, QK_WIDTH)
    f = f.reshape(bp, np_, FOURIER_WIDTH)
    ab = ab.reshape(bp, np_, LANES)
    og, new_f, new_b = _delta_call(qkv, z, _chunk_transposed(ab), a_log[l], dt_bias[l], g_o[l], None, None,
                                   heads_per_step=4, unroll=2)
    fo = _fno_prompt_call(f, wy_p)
    y_prompt = _ffn_call(xp, og.reshape(1, bp * np_, QK_WIDTH), fo.reshape(1, bp * np_, FOURIER_WIDTH),
                         mod3, 0, w_out_a, w_out_b, g_ffn[l], w_g, w_u, w_dn, g_final, 512)
    y_prompt = y_prompt.reshape(bp, np_, d)

    qkv, z, f, ab = _inproj_call(x_sample, mod3, 1, True, g_mix[l], w_cat, w_conv[l], 512, ns)
    og, _, _ = _delta_call(qkv, z, _chunk_transposed(ab), a_log[l], dt_bias[l], g_o[l],
                           state_dn_fwd[:, l:l + 1], state_dn_bwd[:, l:l + 1], heads_per_step=2, unroll=4)
    fo = _fno_grid_call(f, wy_g)
    y_sample = _ffn_call(x_sample, og, fo, mod3, 1, w_out_a, w_out_b, g_ffn[l], w_g, w_u, w_dn, g_final, 512)

    return (y_prompt, y_sample, new_f, new_b)
```

```python
import functools

import numpy as np
import jax
import jax.numpy as jnp
from jax import lax
from jax.experimental import pallas as pl
from jax.experimental.pallas import tpu as pltpu

D_MODEL = 1024
N_HEADS = 4
HEAD_DIM = 128
QK_WIDTH = N_HEADS * HEAD_DIM
QKV_WIDTH = 3 * QK_WIDTH
N_GROUPS = 4
FOURIER_WIDTH = N_GROUPS * HEAD_DIM
N_DIR = 2
GRID_W = 64
CHUNK = 128
D_FF = 2816
RMS_EPS = 1e-6
LANES = 128
W_CAT_COLS = QKV_WIDTH + QK_WIDTH + FOURIER_WIDTH + LANES
VMEM_LIMIT = 56 * 1024 * 1024

F32 = jnp.float32
BF16 = jnp.bfloat16
HIGHEST = lax.Precision.HIGHEST


def _dot(a, b):
    return jnp.dot(a, b, preferred_element_type=F32)


def _silu(x):
    return x * jax.nn.sigmoid(x)


def _softplus(x):
    return jnp.maximum(x, 0.0) + jnp.log1p(jnp.exp(-jnp.abs(x)))


def _params(**kw):
    return pltpu.CompilerParams(vmem_limit_bytes=VMEM_LIMIT, **kw)


def _mod_kernel(cond_ref, w_ref, b_ref, o_ref):
    s = _silu(cond_ref[...]).astype(BF16)
    o_ref[...] = _dot(s, w_ref[...].astype(BF16)) + b_ref[...]


def _mod_call(cond, w_ada, b_ada):
    rows, d = cond.shape
    cols = w_ada.shape[1]
    tn = 1536
    return pl.pallas_call(
        _mod_kernel,
        grid=(cols // tn,),
        in_specs=[pl.BlockSpec((rows, d), lambda j: (0, 0)),
                  pl.BlockSpec((d, tn), lambda j: (0, j)),
                  pl.BlockSpec((1, tn), lambda j: (0, j))],
        out_specs=pl.BlockSpec((rows, tn), lambda j: (0, j)),
        out_shape=jax.ShapeDtypeStruct((rows, cols), F32),
        compiler_params=_params(),
    )(cond, w_ada, b_ada.reshape(1, cols))


def _fno_w_kernel(w_ref, cc_ref, sc_ref, wp_ref, wg_ref):
    w = w_ref[...]
    cw = jnp.dot(cc_ref[...], w, precision=HIGHEST, preferred_element_type=F32)
    sw = jnp.dot(sc_ref[...], w, precision=HIGHEST, preferred_element_type=F32)
    wp_ref[...] = jnp.concatenate([cw, sw], axis=1).astype(BF16)
    wg_ref[...] = jnp.concatenate([cw, -sw, -sw, -cw], axis=1).astype(BF16)


def _fno_w_call(w_fno, cc, sc):
    g, c, _ = w_fno.shape
    return pl.pallas_call(
        _fno_w_kernel,
        grid=(g,),
        in_specs=[pl.BlockSpec((None, c, c), lambda i: (i, 0, 0)),
                  pl.BlockSpec((c, c), lambda i: (0, 0)),
                  pl.BlockSpec((c, c), lambda i: (0, 0))],
        out_specs=[pl.BlockSpec((None, c, 2 * c), lambda i: (i, 0, 0)),
                   pl.BlockSpec((None, c, 4 * c), lambda i: (i, 0, 0))],
        out_shape=[jax.ShapeDtypeStruct((g, c, 2 * c), BF16),
                   jax.ShapeDtypeStruct((g, c, 4 * c), BF16)],
        compiler_params=_params(),
    )(w_fno, cc, sc)


HALO = 8
CONV_COLS = 2 * HEAD_DIM
CONV_ROWS = 128


def _inproj_kernel(xp_ref, x_ref, xn_ref, mod_ref, g_ref, w_ref, wc_ref, qkv_ref, z_ref, f_ref, ab_ref, *,
                   tm, seq_len):
    t = pl.program_id(1)
    x = jnp.concatenate([xp_ref[...], x_ref[...], xn_ref[...]], axis=0)
    y = x * lax.rsqrt(jnp.mean(x * x, axis=-1, keepdims=True) + RMS_EPS) * g_ref[...]
    shift1 = mod_ref[:, 0:D_MODEL]
    scale1 = mod_ref[:, D_MODEL:2 * D_MODEL]
    h_all = (y * (1.0 + scale1) + shift1).astype(BF16)
    h = h_all[HALO:HALO + tm, :]
    at_start = ((t * tm) & (seq_len - 1)) == 0
    at_end = (((t + 1) * tm) & (seq_len - 1)) == 0
    zero_halo = jnp.zeros((HALO, D_MODEL), BF16)
    h_ext = jnp.concatenate([jnp.where(at_start, zero_halo, h_all[0:HALO, :]), h,
                             jnp.where(at_end, zero_halo, h_all[HALO + tm:, :])], axis=0)

    rows = lax.broadcasted_iota(jnp.int32, (CONV_ROWS, HEAD_DIM), 0)
    n_blocks = tm // CONV_ROWS
    for j in range(QKV_WIDTH // CONV_COLS):
        p_wide = _dot(h_ext, w_ref[:, j * CONV_COLS:(j + 1) * CONV_COLS])
        for hh in range(CONV_COLS // HEAD_DIM):
            head = j * (CONV_COLS // HEAD_DIM) + hh
            cols = slice(head * HEAD_DIM, (head + 1) * HEAD_DIM)
            lanes = slice(hh * HEAD_DIM, (hh + 1) * HEAD_DIM)
            for rb in range(n_blocks):
                lo = HALO + rb * CONV_ROWS
                p = p_wide[lo:lo + CONV_ROWS, lanes]
                starts_seq = rb > 0 and (rb * CONV_ROWS) % seq_len == 0
                ends_seq = rb < n_blocks - 1 and ((rb + 1) * CONV_ROWS) % seq_len == 0
                before = 0.0 if starts_seq else p_wide[lo - 1:lo, lanes]
                after = 0.0 if ends_seq else p_wide[lo + CONV_ROWS:lo + CONV_ROWS + 1, lanes]
                p_prev = jnp.where(rows == 0, before, pltpu.roll(p, 1, 0))
                p_next = jnp.where(rows == CONV_ROWS - 1, after, pltpu.roll(p, CONV_ROWS - 1, 0))
                ch = _silu(wc_ref[0:1, cols] * p_prev + wc_ref[1:2, cols] * p + wc_ref[2:3, cols] * p_next)
                if head < 2 * N_HEADS:
                    ch = ch * lax.rsqrt(jnp.sum(ch * ch, axis=-1, keepdims=True) + 1e-6)
                if head < N_HEADS:
                    ch = ch * (HEAD_DIM ** -0.5)
                qkv_ref[rb * CONV_ROWS:(rb + 1) * CONV_ROWS, cols] = ch.astype(BF16)
    c0, c1, c2 = QKV_WIDTH, QKV_WIDTH + QK_WIDTH, QKV_WIDTH + QK_WIDTH + FOURIER_WIDTH
    z_ref[...] = _dot(h, w_ref[:, c0:c1]).astype(BF16)
    f_ref[...] = _dot(h, w_ref[:, c1:c2]).astype(BF16)
    ab_ref[...] = _dot(h, w_ref[:, c2:W_CAT_COLS])


def _inproj_call(x, mod3, mod_row0, mod_per_batch, g_mix, w_cat, w_conv, tm, seq_len):
    b, n, d = x.shape
    assert seq_len & (seq_len - 1) == 0 and n % tm == 0 and tm % CONV_ROWS == 0 and seq_len % CONV_ROWS == 0
    assert tm % seq_len == 0 or seq_len % tm == 0
    tok = lambda w: pl.BlockSpec((None, tm, w), lambda i, t: (i, t, 0))
    per = tm // HALO
    return pl.pallas_call(
        functools.partial(_inproj_kernel, tm=tm, seq_len=seq_len),
        grid=(b, n // tm),
        in_specs=[pl.BlockSpec((None, HALO, d), lambda i, t: (i, jnp.maximum(t * per - 1, 0), 0)),
                  tok(d),
                  pl.BlockSpec((None, HALO, d), lambda i, t: (i, jnp.minimum((t + 1) * per, n // HALO - 1), 0)),
                  pl.BlockSpec((None, 1, 6 * d), lambda i, t: (mod_row0 + (i if mod_per_batch else 0), 0, 0)),
                  pl.BlockSpec((1, d), lambda i, t: (0, 0)),
                  pl.BlockSpec((d, W_CAT_COLS), lambda i, t: (0, 0)),
                  pl.BlockSpec((3, QKV_WIDTH), lambda i, t: (0, 0))],
        out_specs=[tok(QKV_WIDTH), tok(QK_WIDTH), tok(FOURIER_WIDTH), tok(LANES)],
        out_shape=[jax.ShapeDtypeStruct((b, n, QKV_WIDTH), BF16),
                   jax.ShapeDtypeStruct((b, n, QK_WIDTH), BF16),
                   jax.ShapeDtypeStruct((b, n, FOURIER_WIDTH), BF16),
                   jax.ShapeDtypeStruct((b, n, LANES), F32)],
        compiler_params=_params(),
    )(x, x, x, mod3, g_mix.reshape(1, d), w_cat, w_conv)


LEVEL_BLOCKS = tuple(2 ** e for e in range(int(np.log2(CHUNK))))


def _tri_inverse(a_list, eye, level_masks):
    ds = [eye - jnp.where(level_masks[0], a, 0.0) for a in a_list]
    for mask in level_masks[1:]:
        ls = [jnp.where(mask, a, 0.0).astype(BF16) for a in a_list]
        dbs = [d.astype(BF16) for d in ds]
        ms = [_dot(db, l).astype(BF16) for db, l in zip(dbs, ls)]
        ds = [d - _dot(m, db) for d, m, db in zip(ds, ms, dbs)]
    return ds


def _delta_kernel(*refs, n, hp_n, unroll, has_state):
    if has_state:
        (gp_ref, q_ref, k_ref, v_ref, z_ref, abt_ref, go_ref, trir_ref, sf0_ref, sb0_ref,
         og_ref, sf_ref, sb_ref, lhs_sc, nst_sc, gls, osc) = refs
    else:
        (gp_ref, q_ref, k_ref, v_ref, z_ref, abt_ref, go_ref, trir_ref,
         og_ref, sf_ref, sb_ref, lhs_sc, nst_sc, gls, osc) = refs
    hg = pl.program_id(1)
    nc = n // CHUNK

    ci = lax.broadcasted_iota(jnp.int32, (CHUNK, CHUNK), 0)
    cj = lax.broadcasted_iota(jnp.int32, (CHUNK, CHUNK), 1)
    eye = (ci == cj).astype(F32)
    level_masks = [((ci // (2 * b)) == (cj // (2 * b))) & ((ci // b) != (cj // b))
                   for b in LEVEL_BLOCKS]
    incl = (ci >= cj, ci <= cj)
    strict = (ci > cj, ci < cj)
    n_half = N_DIR * N_HEADS
    gate_rows = lax.broadcasted_iota(jnp.int32, (n_half, CHUNK), 0)

    def head_cols(hp):
        return slice(hp * HEAD_DIM, (hp + 1) * HEAD_DIM)

    def split3_lanes(x):
        hi = x.astype(BF16)
        r1 = x - hi.astype(F32)
        lo = r1.astype(BF16)
        lo2 = (r1 - lo.astype(F32)).astype(BF16)
        return jnp.concatenate([hi, lo, lo2], axis=1)

    def select_col(x, col):
        lanes = lax.broadcasted_iota(jnp.int32, x.shape, 1)
        return jnp.sum(jnp.where(lanes == col, x, 0.0), axis=-1, keepdims=True)

    def chain_front(hp, c, d, q, k, v, kk, qk, cum_rows, gate_cols):
        gate = d * N_HEADS + hg * hp_n + hp
        g_col = jnp.broadcast_to(select_col(gate_cols, d * n_half + gate), (CHUNK, LANES))
        beta = select_col(gate_cols, N_DIR * n_half + gate)
        g_row = jnp.sum(jnp.where(gate_rows == gate, cum_rows[d], 0.0), axis=0, keepdims=True)
        g_tot = g_col[CHUNK - 1:CHUNK, :] if d == 0 else g_col[0:1, :]
        decay = jnp.exp(jnp.where(incl[d], g_col - g_row, -jnp.inf))
        eg = jnp.exp(g_col)
        gls[hp, d, c] = jnp.broadcast_to(jnp.exp(g_tot), (8, LANES))
        a = jnp.where(strict[d], beta * decay * kk, 0.0)
        rhs = jnp.concatenate([beta * v, (beta * eg) * k], axis=1).astype(BF16)
        lhs2 = jnp.concatenate([(k * jnp.exp(g_tot - g_col)).T.astype(BF16), (qk * decay).astype(BF16)], axis=0)
        return a, rhs, lhs2, q * eg

    def prepare(i, carry):
        chains = []
        for hp in range(hp_n):
            for j in range(unroll):
                c = i * unroll + j
                r0 = pl.multiple_of(c * CHUNK, CHUNK)
                qb = q_ref[pl.ds(r0, CHUNK), head_cols(hp)]
                kb = k_ref[pl.ds(r0, CHUNK), head_cols(hp)]
                q = qb.astype(F32)
                k = kb.astype(F32)
                v = v_ref[pl.ds(r0, CHUNK), head_cols(hp)].astype(F32)
                kq = lax.dot_general(jnp.concatenate([kb, qb], axis=0), kb,
                                     (((1,), (1,)), ((), ())), preferred_element_type=F32)
                raw = abt_ref[c]
                log_a = -jnp.exp(gp_ref[0:n_half, :]) * _softplus(raw[0:n_half, :] + gp_ref[n_half:, :])
                log_a3 = split3_lanes(log_a)
                cum_rows = (_dot(log_a3, trir_ref[0]), _dot(log_a3, trir_ref[1]))
                gate_cols = jnp.concatenate([cum_rows[0], cum_rows[1], jax.nn.sigmoid(raw[n_half:, :]),
                                             jnp.zeros((CHUNK - 3 * n_half, CHUNK), F32)], axis=0).T
                for d in range(N_DIR):
                    chains.append((hp, j, c, d) + chain_front(hp, c, d, q, k, v, kq[0:CHUNK, :], kq[CHUNK:, :],
                                                              cum_rows, gate_cols))
        t_invs = _tri_inverse([ch[4] for ch in chains], eye, level_masks)
        o_local = {}
        for (hp, j, c, d, _, rhs, lhs2, qg), t_inv in zip(chains, t_invs):
            sol = _dot(t_inv.astype(BF16), rhs).astype(BF16)
            x = _dot(lhs2, sol)
            nst_sc[hp, d, c] = x[0:HEAD_DIM, 0:HEAD_DIM]
            lhs_sc[hp, d, c, 0:HEAD_DIM, :] = (-x[0:HEAD_DIM, HEAD_DIM:]).astype(BF16)
            lhs_sc[hp, d, c, HEAD_DIM:, :] = (qg - x[HEAD_DIM:, HEAD_DIM:]).astype(BF16)
            o_local[(hp, j, d)] = x[HEAD_DIM:, 0:HEAD_DIM]
        for hp in range(hp_n):
            for j in range(unroll):
                c = i * unroll + j
                r0 = pl.multiple_of(c * CHUNK, CHUNK)
                osc[hp, pl.ds(r0, CHUNK), :] = o_local[(hp, j, 0)] + o_local[(hp, j, 1)]
        return carry

    lax.fori_loop(0, nc // unroll, prepare, 0)

    def scan_step(hp, c, d, s):
        r0 = pl.multiple_of(c * CHUNK, CHUNK)
        r = _dot(lhs_sc[hp, d, c], s.astype(BF16))
        osc[hp, pl.ds(r0, CHUNK), :] += r[HEAD_DIM:, :]
        return gls[hp, d, c][0:1, :] * s + r[0:HEAD_DIM, :] + nst_sc[hp, d, c]

    def finalize(hp, c):
        r0 = pl.multiple_of(c * CHUNK, CHUNK)
        o = osc[hp, pl.ds(r0, CHUNK), :]
        y = o * lax.rsqrt(jnp.mean(o * o, axis=-1, keepdims=True) + RMS_EPS) * go_ref[...]
        zz = z_ref[pl.ds(r0, CHUNK), head_cols(hp)].astype(F32)
        og_ref[pl.ds(r0, CHUNK), head_cols(hp)] = (y * _silu(zz)).astype(BF16)

    def scan_body(i, carry, finish):
        out = []
        for hp in range(hp_n):
            out.append(scan_step(hp, i, 0, carry[2 * hp]))
            out.append(scan_step(hp, nc - 1 - i, 1, carry[2 * hp + 1]))
        if finish:
            for hp in range(hp_n):
                finalize(hp, i)
                finalize(hp, nc - 1 - i)
        return tuple(out)

    init = []
    for hp in range(hp_n):
        if has_state:
            init += [sf0_ref[hp], sb0_ref[hp]]
        else:
            init += [jnp.zeros((HEAD_DIM, HEAD_DIM), F32), jnp.zeros((HEAD_DIM, HEAD_DIM), F32)]
    half = lax.fori_loop(0, nc // 2, functools.partial(scan_body, finish=False), tuple(init))
    fin = lax.fori_loop(nc // 2, nc, functools.partial(scan_body, finish=True), half)
    for hp in range(hp_n):
        sf_ref[hp] = fin[2 * hp]
        sb_ref[hp] = fin[2 * hp + 1]


def _tri_tables():
    i = np.arange(CHUNK)
    low = (i[:, None] >= i[None, :]).astype(np.float32)
    trir = np.stack([np.concatenate([m, m, m], axis=0) for m in (low.T, low)])
    return jnp.asarray(trir).astype(BF16)


def _delta_call(qkv, z, abt, a_log, dt_bias, g_o, s0_f, s0_b, *, heads_per_step, unroll):
    b, n, _ = qkv.shape
    nc = n // CHUNK
    assert nc % 2 == 0 and nc % unroll == 0
    hp_n = heads_per_step
    groups = N_HEADS // hp_n
    wide = hp_n * HEAD_DIM
    has_state = s0_f is not None
    trir = _tri_tables()
    col = lambda off: pl.BlockSpec((None, n, wide), lambda i, g: (i, 0, off + g))
    st = pl.BlockSpec((None, None, hp_n, HEAD_DIM, HEAD_DIM), lambda i, g: (i, 0, g, 0, 0))
    n_gate = 2 * N_DIR * N_HEADS
    gate_params = jnp.broadcast_to(jnp.concatenate([a_log.reshape(-1), dt_bias.reshape(-1)])[:, None],
                                   (n_gate, CHUNK))
    in_specs = [pl.BlockSpec((n_gate, CHUNK), lambda i, g: (0, 0)),
                col(0), col(groups), col(2 * groups), col(0),
                pl.BlockSpec((None, nc, n_gate, CHUNK), lambda i, g: (i, 0, 0, 0)),
                pl.BlockSpec((1, HEAD_DIM), lambda i, g: (0, 0)),
                pl.BlockSpec((N_DIR, 3 * CHUNK, CHUNK), lambda i, g: (0, 0, 0))]
    args = [gate_params, qkv, qkv, qkv, z, abt, g_o.reshape(1, HEAD_DIM), trir]
    if has_state:
        in_specs += [st, st]
        args += [s0_f, s0_b]
    state_shape = jax.ShapeDtypeStruct((b, 1, N_HEADS, HEAD_DIM, HEAD_DIM), F32)
    return pl.pallas_call(
        functools.partial(_delta_kernel, n=n, hp_n=hp_n, unroll=unroll, has_state=has_state),
        grid=(b, groups),
        in_specs=in_specs,
        out_specs=[col(0), st, st],
        out_shape=[jax.ShapeDtypeStruct((b, n, QK_WIDTH), BF16), state_shape, state_shape],
        scratch_shapes=[pltpu.VMEM((hp_n, N_DIR, nc, HEAD_DIM + CHUNK, HEAD_DIM), BF16),
                        pltpu.VMEM((hp_n, N_DIR, nc, HEAD_DIM, HEAD_DIM), F32),
                        pltpu.VMEM((hp_n, N_DIR, nc, 8, LANES), F32),
                        pltpu.VMEM((hp_n, n, HEAD_DIM), F32)],
        compiler_params=_params(),
    )(*args)


def _dft_tables(n, scale=1.0):
    idx = np.arange(n)
    ang = 2.0 * np.pi * ((idx[:, None] * idx[None, :]) % n) / n
    return (np.cos(ang) * scale).astype(np.float32), (np.sin(ang) * scale).astype(np.float32)


def _fno_prompt_kernel(f_ref, wy_ref, cn_ref, sn_ref, o_ref):
    cn = cn_ref[...]
    sn = sn_ref[...]
    for g in range(N_GROUPS):
        y = _dot(f_ref[:, g * HEAD_DIM:(g + 1) * HEAD_DIM], wy_ref[g])
        o = _dot(cn, y[:, 0:HEAD_DIM].astype(BF16)) + _dot(sn, y[:, HEAD_DIM:].astype(BF16))
        o_ref[:, g * HEAD_DIM:(g + 1) * HEAD_DIM] = o.astype(BF16)


def _fno_prompt_call(f, wy):
    b, n, w = f.shape
    cn, sn = _dft_tables(n, (n * HEAD_DIM) ** -0.5)
    cn = jnp.asarray(cn).astype(BF16)
    sn_neg = jnp.asarray(-sn).astype(BF16)
    return pl.pallas_call(
        _fno_prompt_kernel,
        grid=(b,),
        in_specs=[pl.BlockSpec((None, n, w), lambda i: (i, 0, 0)),
                  pl.BlockSpec((N_GROUPS, HEAD_DIM, 2 * HEAD_DIM), lambda i: (0, 0, 0)),
                  pl.BlockSpec((n, n), lambda i: (0, 0)),
                  pl.BlockSpec((n, n), lambda i: (0, 0))],
        out_specs=pl.BlockSpec((None, n, w), lambda i: (i, 0, 0)),
        out_shape=jax.ShapeDtypeStruct((b, n, w), BF16),
        compiler_params=_params(),
    )(f, wy, cn, sn_neg)


COL_UNROLL = 4
ROW_UNROLL = 8
ROW_PITCH = GRID_W + 8


def _fno_grid_kernel(f_ref, wy_ref, bdc_ref, bds_ref, crs_ref, o_ref, zr_sc, zi_sc, o_sc, *, n):
    tb = bdc_ref.shape[0]
    two = 2 * HEAD_DIM
    rows = n // GRID_W
    rows_per_block = tb // GRID_W

    def col_body(i, carry):
        blocks = [i * COL_UNROLL + j for j in range(COL_UNROLL)]
        ys = [_dot(f_ref[pl.ds(pl.multiple_of(blk * tb, tb), tb), :], wy_ref[...]).astype(BF16)
              for blk in blocks]
        zs = [_dot(bdc_ref[...], y[:, 0:two]) + _dot(bds_ref[...], y[:, two:]) for y in ys]
        for blk, z in zip(blocks, zs):
            for rr in range(rows_per_block):
                dst = pl.ds(pl.multiple_of((blk * rows_per_block + rr) * ROW_PITCH, 8), GRID_W)
                zr_sc[dst, :] = z[rr * GRID_W:(rr + 1) * GRID_W, 0:HEAD_DIM]
                zi_sc[dst, :] = z[rr * GRID_W:(rr + 1) * GRID_W, HEAD_DIM:]
        return carry

    lax.fori_loop(0, n // (tb * COL_UNROLL), col_body, 0)

    def row_body(i, carry):
        cols = [pl.ds(i * ROW_UNROLL + j, rows, stride=ROW_PITCH) for j in range(ROW_UNROLL)]
        zs = [jnp.concatenate([zr_sc[col, :], zi_sc[col, :]], axis=0).astype(BF16) for col in cols]
        outs = [_dot(crs_ref[...], z) for z in zs]
        for col, o in zip(cols, outs):
            o_sc[col, :] = o
        return carry

    lax.fori_loop(0, GRID_W // ROW_UNROLL, row_body, 0)

    def out_body(r, carry):
        src = pl.ds(pl.multiple_of(r * ROW_PITCH, 8), GRID_W)
        o_ref[pl.ds(pl.multiple_of(r * GRID_W, GRID_W), GRID_W), :] = o_sc[src, :].astype(BF16)
        return carry

    lax.fori_loop(0, rows, out_body, 0)


def _fno_grid_call(f, wy4):
    b, n, w = f.shape
    rows = n // GRID_W
    tb = 256
    cw, sw = _dft_tables(GRID_W)
    rep = np.eye(tb // GRID_W, dtype=np.float32)
    bdc = jnp.asarray(np.kron(rep, cw)).astype(BF16)
    bds = jnp.asarray(np.kron(rep, sw)).astype(BF16)
    cr, sr = _dft_tables(rows, (n * HEAD_DIM) ** -0.5)
    crs = jnp.asarray(np.concatenate([cr, sr], axis=1)).astype(BF16)
    const = lambda s: pl.BlockSpec(s, lambda i, g: (0,) * len(s))
    return pl.pallas_call(
        functools.partial(_fno_grid_kernel, n=n),
        grid=(b, N_GROUPS),
        in_specs=[pl.BlockSpec((None, n, HEAD_DIM), lambda i, g: (i, 0, g)),
                  pl.BlockSpec((None, HEAD_DIM, 4 * HEAD_DIM), lambda i, g: (g, 0, 0)),
                  const((tb, tb)), const((tb, tb)), const((rows, 2 * rows))],
        out_specs=pl.BlockSpec((None, n, HEAD_DIM), lambda i, g: (i, 0, g)),
        out_shape=jax.ShapeDtypeStruct((b, n, w), BF16),
        scratch_shapes=[pltpu.VMEM((rows * ROW_PITCH, HEAD_DIM), F32)] * 3,
        compiler_params=_params(),
    )(f, wy4, bdc, bds, crs)


FF_BLOCK = 256
FFN_SUB_ROWS = 512


def _ffn_kernel(x_ref, og_ref, fo_ref, mod_ref, woa_ref, wob_ref, gffn_ref, wg_ref, wu_ref, wd_ref,
                gfin_ref, y_ref, *, n_sub):
    d = D_MODEL
    gate1 = mod_ref[:, 2 * d:3 * d]
    shift2 = mod_ref[:, 3 * d:4 * d]
    scale2 = mod_ref[:, 4 * d:5 * d]
    gate2 = mod_ref[:, 5 * d:6 * d]
    sub = x_ref.shape[0] // n_sub
    blocks = [slice(s * sub, (s + 1) * sub) for s in range(n_sub)]
    x1s, h2s = [], []
    for rs in blocks:
        mo = _dot(og_ref[rs, :], woa_ref[...]) + _dot(fo_ref[rs, :], wob_ref[...])
        x1 = x_ref[rs, :] + gate1 * mo
        hn = x1 * lax.rsqrt(jnp.mean(x1 * x1, axis=-1, keepdims=True) + RMS_EPS) * gffn_ref[...]
        x1s.append(x1)
        h2s.append((hn * (1.0 + scale2) + shift2).astype(BF16))
    accs = [jnp.zeros((sub, d), F32) for _ in blocks]
    for j in range(D_FF // FF_BLOCK):
        sl = slice(j * FF_BLOCK, (j + 1) * FF_BLOCK)
        for s in range(n_sub):
            gt = _dot(h2s[s], wg_ref[:, sl])
            up = _dot(h2s[s], wu_ref[:, sl])
            accs[s] = accs[s] + _dot((_silu(gt) * up).astype(BF16), wd_ref[sl, :])
    for rs, x1, acc in zip(blocks, x1s, accs):
        x2 = x1 + gate2 * acc
        y_ref[rs, :] = x2 * lax.rsqrt(jnp.mean(x2 * x2, axis=-1, keepdims=True) + RMS_EPS) * gfin_ref[...]


def _ffn_call(x, og, fo, mod3, mod_row0, w_out_a, w_out_b, g_ffn, w_g, w_u, w_down, g_final, tm):
    b, n, d = x.shape
    tok = lambda w: pl.BlockSpec((None, tm, w), lambda i, t: (i, t, 0))
    const = lambda s: pl.BlockSpec(s, lambda i, t: (0,) * len(s), pipeline_mode=pl.Buffered(1))
    return pl.pallas_call(
        functools.partial(_ffn_kernel, n_sub=tm // FFN_SUB_ROWS),
        grid=(b, n // tm),
        in_specs=[tok(d), tok(QK_WIDTH), tok(FOURIER_WIDTH),
                  pl.BlockSpec((None, 1, 6 * d), lambda i, t: (mod_row0 + i, 0, 0)),
                  const((QK_WIDTH, d)), const((FOURIER_WIDTH, d)), const((1, d)),
                  const((d, D_FF)), const((d, D_FF)), const((D_FF, d)), const((1, d))],
        out_specs=tok(d),
        out_shape=jax.ShapeDtypeStruct((b, n, d), F32),
        compiler_params=_params(),
    )(x, og, fo, mod3, w_out_a, w_out_b, g_ffn.reshape(1, d), w_g, w_u, w_down, g_final.reshape(1, d))


def _chunk_transposed(ab):
    b, n, _ = ab.shape
    g = ab[:, :, 0:2 * N_DIR * N_HEADS].reshape(b, n // CHUNK, CHUNK, 2 * N_DIR * N_HEADS)
    return jnp.swapaxes(g, 2, 3)


def kernel(x_prompt, x_sample, c, state_dn_fwd, state_dn_bwd, c_ctx, w_ada, b_ada, g_mix, w_in, w_conv,
           a_log, dt_bias, g_o, w_fno, w_out, g_ffn, w_gu, w_down, g_final):
    d = D_MODEL
    bp, np_, _ = x_prompt.shape
    bs, ns, _ = x_sample.shape
    l = 0

    wi = w_in[l]
    n_gate = 2 * N_DIR * N_HEADS
    g0 = QKV_WIDTH + QK_WIDTH
    w_cat = jnp.concatenate([wi[:, 0:g0], wi[:, g0 + n_gate:], wi[:, g0:g0 + n_gate],
                             jnp.zeros((d, LANES - n_gate), F32)], axis=1).astype(BF16)
    w_out_b16 = w_out[l].astype(BF16)
    w_out_a, w_out_b = w_out_b16[0:QK_WIDTH], w_out_b16[QK_WIDTH:]
    w_g = w_gu[l][:, 0:D_FF].astype(BF16)
    w_u = w_gu[l][:, D_FF:].astype(BF16)
    w_dn = w_down[l].astype(BF16)

    cond = jnp.concatenate([c_ctx[None, :], c, jnp.zeros((16 - 1 - bs, d), F32)], axis=0)
    mod = _mod_call(cond, w_ada[l], b_ada[l])
    mod3 = mod.reshape(16, 1, 6 * d)

    cc, sc = _dft_tables(HEAD_DIM)
    wy_p, wy_g = _fno_w_call(w_fno[l], jnp.asarray(cc), jnp.asarray(sc))

    xp = x_prompt.reshape(1, bp * np_, d)
    qkv, z, f, ab = _inproj_call(xp, mod3, 0, False, g_mix[l], w_cat, w_conv[l], 512, np_)
    qkv = qkv.reshape(bp, np_, QKV_WIDTH)
    z = z.reshape(bp, np_, QK_WIDTH)
    f = f.reshape(bp, np_, FOURIER_WIDTH)
    ab = ab.reshape(bp, np_, LANES)
    og, new_f, new_b = _delta_call(qkv, z, _chunk_transposed(ab), a_log[l], dt_bias[l], g_o[l], None, None,
                                   heads_per_step=4, unroll=2)
    fo = _fno_prompt_call(f, wy_p)
    y_prompt = _ffn_call(xp, og.reshape(1, bp * np_, QK_WIDTH), fo.reshape(1, bp * np_, FOURIER_WIDTH),
                         mod3, 0, w_out_a, w_out_b, g_ffn[l], w_g, w_u, w_dn, g_final, 1024)
    y_prompt = y_prompt.reshape(bp, np_, d)

    qkv, z, f, ab = _inproj_call(x_sample, mod3, 1, True, g_mix[l], w_cat, w_conv[l], 512, ns)
    og, _, _ = _delta_call(qkv, z, _chunk_transposed(ab), a_log[l], dt_bias[l], g_o[l],
                           state_dn_fwd[:, l:l + 1], state_dn_bwd[:, l:l + 1], heads_per_step=2, unroll=4)
    fo = _fno_grid_call(f, wy_g)
    y_sample = _ffn_call(x_sample, og, fo, mod3, 1, w_out_a, w_out_b, g_ffn[l], w_g, w_u, w_dn, g_final, 1024)

    return (y_prompt, y_sample, new_f, new_b)
```

```python
import functools

import numpy as np
import jax
import jax.numpy as jnp
from jax import lax
from jax.experimental import pallas as pl
from jax.experimental.pallas import tpu as pltpu

D_MODEL = 1024
N_HEADS = 4
HEAD_DIM = 128
QK_WIDTH = N_HEADS * HEAD_DIM
QKV_WIDTH = 3 * QK_WIDTH
N_GROUPS = 4
FOURIER_WIDTH = N_GROUPS * HEAD_DIM
N_DIR = 2
GRID_W = 64
CHUNK = 128
D_FF = 2816
RMS_EPS = 1e-6
LANES = 128
W_CAT_COLS = QKV_WIDTH + QK_WIDTH + FOURIER_WIDTH + LANES
VMEM_LIMIT = 56 * 1024 * 1024

F32 = jnp.float32
BF16 = jnp.bfloat16
HIGHEST = lax.Precision.HIGHEST


def _dot(a, b):
    return jnp.dot(a, b, preferred_element_type=F32)


def _silu(x):
    return x * jax.nn.sigmoid(x)


def _softplus(x):
    return jnp.maximum(x, 0.0) + jnp.log1p(jnp.exp(-jnp.abs(x)))


def _params(**kw):
    return pltpu.CompilerParams(vmem_limit_bytes=VMEM_LIMIT, **kw)


def _mod_kernel(cond_ref, w_ref, b_ref, o_ref):
    s = _silu(cond_ref[...]).astype(BF16)
    o_ref[...] = _dot(s, w_ref[...].astype(BF16)) + b_ref[...]


def _mod_call(cond, w_ada, b_ada):
    rows, d = cond.shape
    cols = w_ada.shape[1]
    tn = 1536
    return pl.pallas_call(
        _mod_kernel,
        grid=(cols // tn,),
        in_specs=[pl.BlockSpec((rows, d), lambda j: (0, 0)),
                  pl.BlockSpec((d, tn), lambda j: (0, j)),
                  pl.BlockSpec((1, tn), lambda j: (0, j))],
        out_specs=pl.BlockSpec((rows, tn), lambda j: (0, j)),
        out_shape=jax.ShapeDtypeStruct((rows, cols), F32),
        compiler_params=_params(),
    )(cond, w_ada, b_ada.reshape(1, cols))


def _fno_w_kernel(w_ref, cc_ref, sc_ref, wp_ref, wg_ref):
    w = w_ref[...]
    cw = jnp.dot(cc_ref[...], w, precision=HIGHEST, preferred_element_type=F32)
    sw = jnp.dot(sc_ref[...], w, precision=HIGHEST, preferred_element_type=F32)
    wp_ref[...] = jnp.concatenate([cw, sw], axis=1).astype(BF16)
    wg_ref[...] = jnp.concatenate([cw, -sw, -sw, -cw], axis=1).astype(BF16)


def _fno_w_call(w_fno, cc, sc):
    g, c, _ = w_fno.shape
    return pl.pallas_call(
        _fno_w_kernel,
        grid=(g,),
        in_specs=[pl.BlockSpec((None, c, c), lambda i: (i, 0, 0)),
                  pl.BlockSpec((c, c), lambda i: (0, 0)),
                  pl.BlockSpec((c, c), lambda i: (0, 0))],
        out_specs=[pl.BlockSpec((None, c, 2 * c), lambda i: (i, 0, 0)),
                   pl.BlockSpec((None, c, 4 * c), lambda i: (i, 0, 0))],
        out_shape=[jax.ShapeDtypeStruct((g, c, 2 * c), BF16),
                   jax.ShapeDtypeStruct((g, c, 4 * c), BF16)],
        compiler_params=_params(),
    )(w_fno, cc, sc)


HALO = 8
CONV_COLS = 2 * HEAD_DIM
CONV_ROWS = 128


def _inproj_kernel(xp_ref, x_ref, xn_ref, mod_ref, g_ref, w_ref, wc_ref, qkv_ref, z_ref, f_ref, ab_ref, *,
                   tm, seq_len):
    t = pl.program_id(1)
    x = jnp.concatenate([xp_ref[...], x_ref[...], xn_ref[...]], axis=0)
    y = x * lax.rsqrt(jnp.mean(x * x, axis=-1, keepdims=True) + RMS_EPS) * g_ref[...]
    shift1 = mod_ref[:, 0:D_MODEL]
    scale1 = mod_ref[:, D_MODEL:2 * D_MODEL]
    h_all = (y * (1.0 + scale1) + shift1).astype(BF16)
    h = h_all[HALO:HALO + tm, :]
    at_start = ((t * tm) & (seq_len - 1)) == 0
    at_end = (((t + 1) * tm) & (seq_len - 1)) == 0
    zero_halo = jnp.zeros((HALO, D_MODEL), BF16)
    h_ext = jnp.concatenate([jnp.where(at_start, zero_halo, h_all[0:HALO, :]), h,
                             jnp.where(at_end, zero_halo, h_all[HALO + tm:, :])], axis=0)

    rows = lax.broadcasted_iota(jnp.int32, (CONV_ROWS, HEAD_DIM), 0)
    n_blocks = tm // CONV_ROWS
    for j in range(QKV_WIDTH // CONV_COLS):
        p_wide = _dot(h_ext, w_ref[:, j * CONV_COLS:(j + 1) * CONV_COLS])
        for hh in range(CONV_COLS // HEAD_DIM):
            head = j * (CONV_COLS // HEAD_DIM) + hh
            cols = slice(head * HEAD_DIM, (head + 1) * HEAD_DIM)
            lanes = slice(hh * HEAD_DIM, (hh + 1) * HEAD_DIM)
            for rb in range(n_blocks):
                lo = HALO + rb * CONV_ROWS
                p = p_wide[lo:lo + CONV_ROWS, lanes]
                starts_seq = rb > 0 and (rb * CONV_ROWS) % seq_len == 0
                ends_seq = rb < n_blocks - 1 and ((rb + 1) * CONV_ROWS) % seq_len == 0
                before = 0.0 if starts_seq else p_wide[lo - 1:lo, lanes]
                after = 0.0 if ends_seq else p_wide[lo + CONV_ROWS:lo + CONV_ROWS + 1, lanes]
                p_prev = jnp.where(rows == 0, before, pltpu.roll(p, 1, 0))
                p_next = jnp.where(rows == CONV_ROWS - 1, after, pltpu.roll(p, CONV_ROWS - 1, 0))
                ch = _silu(wc_ref[0:1, cols] * p_prev + wc_ref[1:2, cols] * p + wc_ref[2:3, cols] * p_next)
                if head < 2 * N_HEADS:
                    ch = ch * lax.rsqrt(jnp.sum(ch * ch, axis=-1, keepdims=True) + 1e-6)
                if head < N_HEADS:
                    ch = ch * (HEAD_DIM ** -0.5)
                qkv_ref[rb * CONV_ROWS:(rb + 1) * CONV_ROWS, cols] = ch.astype(BF16)
    c0, c1, c2 = QKV_WIDTH, QKV_WIDTH + QK_WIDTH, QKV_WIDTH + QK_WIDTH + FOURIER_WIDTH
    z_ref[...] = _dot(h, w_ref[:, c0:c1]).astype(BF16)
    f_ref[...] = _dot(h, w_ref[:, c1:c2]).astype(BF16)
    ab_ref[...] = _dot(h, w_ref[:, c2:W_CAT_COLS])


def _inproj_call(x, mod3, mod_row0, mod_per_batch, g_mix, w_cat, w_conv, tm, seq_len):
    b, n, d = x.shape
    assert seq_len & (seq_len - 1) == 0 and n % tm == 0 and tm % CONV_ROWS == 0 and seq_len % CONV_ROWS == 0
    assert tm % seq_len == 0 or seq_len % tm == 0
    tok = lambda w: pl.BlockSpec((None, tm, w), lambda i, t: (i, t, 0))
    per = tm // HALO
    return pl.pallas_call(
        functools.partial(_inproj_kernel, tm=tm, seq_len=seq_len),
        grid=(b, n // tm),
        in_specs=[pl.BlockSpec((None, HALO, d), lambda i, t: (i, jnp.maximum(t * per - 1, 0), 0)),
                  tok(d),
                  pl.BlockSpec((None, HALO, d), lambda i, t: (i, jnp.minimum((t + 1) * per, n // HALO - 1), 0)),
                  pl.BlockSpec((None, 1, 6 * d), lambda i, t: (mod_row0 + (i if mod_per_batch else 0), 0, 0)),
                  pl.BlockSpec((1, d), lambda i, t: (0, 0)),
                  pl.BlockSpec((d, W_CAT_COLS), lambda i, t: (0, 0)),
                  pl.BlockSpec((3, QKV_WIDTH), lambda i, t: (0, 0))],
        out_specs=[tok(QKV_WIDTH), tok(QK_WIDTH), tok(FOURIER_WIDTH), tok(LANES)],
        out_shape=[jax.ShapeDtypeStruct((b, n, QKV_WIDTH), BF16),
                   jax.ShapeDtypeStruct((b, n, QK_WIDTH), BF16),
                   jax.ShapeDtypeStruct((b, n, FOURIER_WIDTH), BF16),
                   jax.ShapeDtypeStruct((b, n, LANES), F32)],
        compiler_params=_params(),
    )(x, x, x, mod3, g_mix.reshape(1, d), w_cat, w_conv)


LEVEL_BLOCKS = tuple(2 ** e for e in range(int(np.log2(CHUNK))))


def _tri_inverse(a_list, eye, level_masks):
    ds = [eye - jnp.where(level_masks[0], a, 0.0) for a in a_list]
    for mask in level_masks[1:]:
        ls = [jnp.where(mask, a, 0.0).astype(BF16) for a in a_list]
        dbs = [d.astype(BF16) for d in ds]
        ms = [_dot(db, l).astype(BF16) for db, l in zip(dbs, ls)]
        ds = [d - _dot(m, db) for d, m, db in zip(ds, ms, dbs)]
    return ds


def _delta_kernel(*refs, n, hp_n, unroll, has_state):
    if has_state:
        (gp_ref, q_ref, k_ref, v_ref, z_ref, abt_ref, go_ref, trir_ref, sf0_ref, sb0_ref,
         og_ref, sf_ref, sb_ref, lhs_sc, nst_sc, gls, osc) = refs
    else:
        (gp_ref, q_ref, k_ref, v_ref, z_ref, abt_ref, go_ref, trir_ref,
         og_ref, sf_ref, sb_ref, lhs_sc, nst_sc, gls, osc) = refs
    hg = pl.program_id(1)
    nc = n // CHUNK

    ci = lax.broadcasted_iota(jnp.int32, (CHUNK, CHUNK), 0)
    cj = lax.broadcasted_iota(jnp.int32, (CHUNK, CHUNK), 1)
    eye = (ci == cj).astype(F32)
    level_masks = [((ci // (2 * b)) == (cj // (2 * b))) & ((ci // b) != (cj // b))
                   for b in LEVEL_BLOCKS]
    incl = (ci >= cj, ci <= cj)
    strict = (ci > cj, ci < cj)
    n_half = N_DIR * N_HEADS
    gate_rows = lax.broadcasted_iota(jnp.int32, (n_half, CHUNK), 0)

    def head_cols(hp):
        return slice(hp * HEAD_DIM, (hp + 1) * HEAD_DIM)

    def split3_lanes(x):
        hi = x.astype(BF16)
        r1 = x - hi.astype(F32)
        lo = r1.astype(BF16)
        lo2 = (r1 - lo.astype(F32)).astype(BF16)
        return jnp.concatenate([hi, lo, lo2], axis=1)

    def select_col(x, col):
        lanes = lax.broadcasted_iota(jnp.int32, x.shape, 1)
        return jnp.sum(jnp.where(lanes == col, x, 0.0), axis=-1, keepdims=True)

    def chain_front(hp, c, d, q, k, v, kk, qk, cum_rows, gate_cols):
        gate = d * N_HEADS + hg * hp_n + hp
        g_col = jnp.broadcast_to(select_col(gate_cols, d * n_half + gate), (CHUNK, LANES))
        beta = select_col(gate_cols, N_DIR * n_half + gate)
        g_row = jnp.sum(jnp.where(gate_rows == gate, cum_rows[d], 0.0), axis=0, keepdims=True)
        g_tot = g_col[CHUNK - 1:CHUNK, :] if d == 0 else g_col[0:1, :]
        decay = jnp.exp(jnp.where(incl[d], g_col - g_row, -jnp.inf))
        eg = jnp.exp(g_col)
        gls[hp, d, c] = jnp.broadcast_to(jnp.exp(g_tot), (8, LANES))
        a = jnp.where(strict[d], beta * decay * kk, 0.0)
        rhs = jnp.concatenate([beta * v, (beta * eg) * k], axis=1).astype(BF16)
        lhs2 = jnp.concatenate([(k * jnp.exp(g_tot - g_col)).T.astype(BF16), (qk * decay).astype(BF16)], axis=0)
        return a, rhs, lhs2, q * eg

    def prepare(i, carry):
        raws = [abt_ref[i * unroll + j] for j in range(unroll)]
        log_as = [-jnp.exp(gp_ref[0:n_half, :]) * _softplus(raw[0:n_half, :] + gp_ref[n_half:, :]) for raw in raws]
        cum_f_all = _dot(split3_lanes(jnp.concatenate(log_as, axis=0)), trir_ref[...])
        gate_info = []
        for j in range(unroll):
            cum_f = cum_f_all[j * n_half:(j + 1) * n_half, :]
            cum_b = cum_f[:, CHUNK - 1:CHUNK] - cum_f + log_as[j]
            gate_cols = jnp.concatenate([cum_f, cum_b, jax.nn.sigmoid(raws[j][n_half:, :]),
                                         jnp.zeros((CHUNK - 3 * n_half, CHUNK), F32)], axis=0).T
            gate_info.append(((cum_f, cum_b), gate_cols))
        chains = []
        for hp in range(hp_n):
            for j in range(unroll):
                c = i * unroll + j
                r0 = pl.multiple_of(c * CHUNK, CHUNK)
                qb = q_ref[pl.ds(r0, CHUNK), head_cols(hp)]
                kb = k_ref[pl.ds(r0, CHUNK), head_cols(hp)]
                q = qb.astype(F32)
                k = kb.astype(F32)
                v = v_ref[pl.ds(r0, CHUNK), head_cols(hp)].astype(F32)
                kq = lax.dot_general(jnp.concatenate([kb, qb], axis=0), kb,
                                     (((1,), (1,)), ((), ())), preferred_element_type=F32)
                for d in range(N_DIR):
                    chains.append((hp, j, c, d) + chain_front(hp, c, d, q, k, v, kq[0:CHUNK, :], kq[CHUNK:, :],
                                                              *gate_info[j]))
        t_invs = _tri_inverse([ch[4] for ch in chains], eye, level_masks)
        o_local = {}
        for (hp, j, c, d, _, rhs, lhs2, qg), t_inv in zip(chains, t_invs):
            sol = _dot(t_inv.astype(BF16), rhs).astype(BF16)
            x = _dot(lhs2, sol)
            nst_sc[hp, d, c] = x[0:HEAD_DIM, 0:HEAD_DIM]
            lhs_sc[hp, d, c, 0:HEAD_DIM, :] = (-x[0:HEAD_DIM, HEAD_DIM:]).astype(BF16)
            lhs_sc[hp, d, c, HEAD_DIM:, :] = (qg - x[HEAD_DIM:, HEAD_DIM:]).astype(BF16)
            o_local[(hp, j, d)] = x[HEAD_DIM:, 0:HEAD_DIM]
        for hp in range(hp_n):
            for j in range(unroll):
                c = i * unroll + j
                r0 = pl.multiple_of(c * CHUNK, CHUNK)
                osc[hp, pl.ds(r0, CHUNK), :] = o_local[(hp, j, 0)] + o_local[(hp, j, 1)]
        return carry

    lax.fori_loop(0, nc // unroll, prepare, 0)

    def scan_step(hp, c, d, s):
        r0 = pl.multiple_of(c * CHUNK, CHUNK)
        r = _dot(lhs_sc[hp, d, c], s.astype(BF16))
        osc[hp, pl.ds(r0, CHUNK), :] += r[HEAD_DIM:, :]
        return gls[hp, d, c][0:1, :] * s + r[0:HEAD_DIM, :] + nst_sc[hp, d, c]

    def finalize(hp, c):
        r0 = pl.multiple_of(c * CHUNK, CHUNK)
        o = osc[hp, pl.ds(r0, CHUNK), :]
        y = o * lax.rsqrt(jnp.mean(o * o, axis=-1, keepdims=True) + RMS_EPS) * go_ref[...]
        zz = z_ref[pl.ds(r0, CHUNK), head_cols(hp)].astype(F32)
        og_ref[pl.ds(r0, CHUNK), head_cols(hp)] = (y * _silu(zz)).astype(BF16)

    def scan_body(i, carry, finish):
        out = []
        for hp in range(hp_n):
            out.append(scan_step(hp, i, 0, carry[2 * hp]))
            out.append(scan_step(hp, nc - 1 - i, 1, carry[2 * hp + 1]))
        if finish:
            for hp in range(hp_n):
                finalize(hp, i)
                finalize(hp, nc - 1 - i)
        return tuple(out)

    init = []
    for hp in range(hp_n):
        if has_state:
            init += [sf0_ref[hp], sb0_ref[hp]]
        else:
            init += [jnp.zeros((HEAD_DIM, HEAD_DIM), F32), jnp.zeros((HEAD_DIM, HEAD_DIM), F32)]
    half = lax.fori_loop(0, nc // 2, functools.partial(scan_body, finish=False), tuple(init))
    fin = lax.fori_loop(nc // 2, nc, functools.partial(scan_body, finish=True), half)
    for hp in range(hp_n):
        sf_ref[hp] = fin[2 * hp]
        sb_ref[hp] = fin[2 * hp + 1]


def _tri_tables():
    i = np.arange(CHUNK)
    up = (i[:, None] <= i[None, :]).astype(np.float32)
    return jnp.asarray(np.concatenate([up, up, up], axis=0)).astype(BF16)


def _delta_call(qkv, z, abt, a_log, dt_bias, g_o, s0_f, s0_b, *, heads_per_step, unroll):
    b, n, _ = qkv.shape
    nc = n // CHUNK
    assert nc % 2 == 0 and nc % unroll == 0
    hp_n = heads_per_step
    groups = N_HEADS // hp_n
    wide = hp_n * HEAD_DIM
    has_state = s0_f is not None
    trir = _tri_tables()
    col = lambda off: pl.BlockSpec((None, n, wide), lambda i, g: (i, 0, off + g))
    st = pl.BlockSpec((None, None, hp_n, HEAD_DIM, HEAD_DIM), lambda i, g: (i, 0, g, 0, 0))
    n_gate = 2 * N_DIR * N_HEADS
    gate_params = jnp.broadcast_to(jnp.concatenate([a_log.reshape(-1), dt_bias.reshape(-1)])[:, None],
                                   (n_gate, CHUNK))
    in_specs = [pl.BlockSpec((n_gate, CHUNK), lambda i, g: (0, 0)),
                col(0), col(groups), col(2 * groups), col(0),
                pl.BlockSpec((None, nc, n_gate, CHUNK), lambda i, g: (i, 0, 0, 0)),
                pl.BlockSpec((1, HEAD_DIM), lambda i, g: (0, 0)),
                pl.BlockSpec((3 * CHUNK, CHUNK), lambda i, g: (0, 0))]
    args = [gate_params, qkv, qkv, qkv, z, abt, g_o.reshape(1, HEAD_DIM), trir]
    if has_state:
        in_specs += [st, st]
        args += [s0_f, s0_b]
    state_shape = jax.ShapeDtypeStruct((b, 1, N_HEADS, HEAD_DIM, HEAD_DIM), F32)
    return pl.pallas_call(
        functools.partial(_delta_kernel, n=n, hp_n=hp_n, unroll=unroll, has_state=has_state),
        grid=(b, groups),
        in_specs=in_specs,
        out_specs=[col(0), st, st],
        out_shape=[jax.ShapeDtypeStruct((b, n, QK_WIDTH), BF16), state_shape, state_shape],
        scratch_shapes=[pltpu.VMEM((hp_n, N_DIR, nc, HEAD_DIM + CHUNK, HEAD_DIM), BF16),
                        pltpu.VMEM((hp_n, N_DIR, nc, HEAD_DIM, HEAD_DIM), F32),
                        pltpu.VMEM((hp_n, N_DIR, nc, 8, LANES), F32),
                        pltpu.VMEM((hp_n, n, HEAD_DIM), F32)],
        compiler_params=_params(),
    )(*args)


def _dft_tables(n, scale=1.0):
    idx = np.arange(n)
    ang = 2.0 * np.pi * ((idx[:, None] * idx[None, :]) % n) / n
    return (np.cos(ang) * scale).astype(np.float32), (np.sin(ang) * scale).astype(np.float32)


def _fno_prompt_kernel(f_ref, wy_ref, cn_ref, sn_ref, o_ref):
    cn = cn_ref[...]
    sn = sn_ref[...]
    for g in range(N_GROUPS):
        y = _dot(f_ref[:, g * HEAD_DIM:(g + 1) * HEAD_DIM], wy_ref[g])
        o = _dot(cn, y[:, 0:HEAD_DIM].astype(BF16)) + _dot(sn, y[:, HEAD_DIM:].astype(BF16))
        o_ref[:, g * HEAD_DIM:(g + 1) * HEAD_DIM] = o.astype(BF16)


def _fno_prompt_call(f, wy):
    b, n, w = f.shape
    cn, sn = _dft_tables(n, (n * HEAD_DIM) ** -0.5)
    cn = jnp.asarray(cn).astype(BF16)
    sn_neg = jnp.asarray(-sn).astype(BF16)
    return pl.pallas_call(
        _fno_prompt_kernel,
        grid=(b,),
        in_specs=[pl.BlockSpec((None, n, w), lambda i: (i, 0, 0)),
                  pl.BlockSpec((N_GROUPS, HEAD_DIM, 2 * HEAD_DIM), lambda i: (0, 0, 0)),
                  pl.BlockSpec((n, n), lambda i: (0, 0)),
                  pl.BlockSpec((n, n), lambda i: (0, 0))],
        out_specs=pl.BlockSpec((None, n, w), lambda i: (i, 0, 0)),
        out_shape=jax.ShapeDtypeStruct((b, n, w), BF16),
        compiler_params=_params(),
    )(f, wy, cn, sn_neg)


COL_UNROLL = 4
ROW_UNROLL = 8
ROW_PITCH = GRID_W + 8


def _fno_grid_kernel(f_ref, wy_ref, bdc_ref, bds_ref, crs_ref, o_ref, zr_sc, zi_sc, o_sc, *, n):
    tb = bdc_ref.shape[0]
    two = 2 * HEAD_DIM
    rows = n // GRID_W
    rows_per_block = tb // GRID_W

    def col_body(i, carry):
        blocks = [i * COL_UNROLL + j for j in range(COL_UNROLL)]
        ys = [_dot(f_ref[pl.ds(pl.multiple_of(blk * tb, tb), tb), :], wy_ref[...]).astype(BF16)
              for blk in blocks]
        zs = [_dot(bdc_ref[...], y[:, 0:two]) + _dot(bds_ref[...], y[:, two:]) for y in ys]
        for blk, z in zip(blocks, zs):
            for rr in range(rows_per_block):
                dst = pl.ds(pl.multiple_of((blk * rows_per_block + rr) * ROW_PITCH, 8), GRID_W)
                zr_sc[dst, :] = z[rr * GRID_W:(rr + 1) * GRID_W, 0:HEAD_DIM]
                zi_sc[dst, :] = z[rr * GRID_W:(rr + 1) * GRID_W, HEAD_DIM:]
        return carry

    lax.fori_loop(0, n // (tb * COL_UNROLL), col_body, 0)

    def row_body(i, carry):
        cols = [pl.ds(i * ROW_UNROLL + j, rows, stride=ROW_PITCH) for j in range(ROW_UNROLL)]
        zs = [jnp.concatenate([zr_sc[col, :], zi_sc[col, :]], axis=0).astype(BF16) for col in cols]
        outs = [_dot(crs_ref[...], z) for z in zs]
        for col, o in zip(cols, outs):
            o_sc[col, :] = o
        return carry

    lax.fori_loop(0, GRID_W // ROW_UNROLL, row_body, 0)

    def out_body(r, carry):
        src = pl.ds(pl.multiple_of(r * ROW_PITCH, 8), GRID_W)
        o_ref[pl.ds(pl.multiple_of(r * GRID_W, GRID_W), GRID_W), :] = o_sc[src, :].astype(BF16)
        return carry

    lax.fori_loop(0, rows, out_body, 0)


def _fno_grid_call(f, wy4):
    b, n, w = f.shape
    rows = n // GRID_W
    tb = 256
    cw, sw = _dft_tables(GRID_W)
    rep = np.eye(tb // GRID_W, dtype=np.float32)
    bdc = jnp.asarray(np.kron(rep, cw)).astype(BF16)
    bds = jnp.asarray(np.kron(rep, sw)).astype(BF16)
    cr, sr = _dft_tables(rows, (n * HEAD_DIM) ** -0.5)
    crs = jnp.asarray(np.concatenate([cr, sr], axis=1)).astype(BF16)
    const = lambda s: pl.BlockSpec(s, lambda i, g: (0,) * len(s))
    return pl.pallas_call(
        functools.partial(_fno_grid_kernel, n=n),
        grid=(b, N_GROUPS),
        in_specs=[pl.BlockSpec((None, n, HEAD_DIM), lambda i, g: (i, 0, g)),
                  pl.BlockSpec((None, HEAD_DIM, 4 * HEAD_DIM), lambda i, g: (g, 0, 0)),
                  const((tb, tb)), const((tb, tb)), const((rows, 2 * rows))],
        out_specs=pl.BlockSpec((None, n, HEAD_DIM), lambda i, g: (i, 0, g)),
        out_shape=jax.ShapeDtypeStruct((b, n, w), BF16),
        scratch_shapes=[pltpu.VMEM((rows * ROW_PITCH, HEAD_DIM), F32)] * 3,
        compiler_params=_params(),
    )(f, wy4, bdc, bds, crs)


FF_BLOCK = 256
FFN_SUB_ROWS = 512


def _ffn_kernel(x_ref, og_ref, fo_ref, mod_ref, woa_ref, wob_ref, gffn_ref, wg_ref, wu_ref, wd_ref,
                gfin_ref, y_ref, *, n_sub):
    d = D_MODEL
    gate1 = mod_ref[:, 2 * d:3 * d]
    shift2 = mod_ref[:, 3 * d:4 * d]
    scale2 = mod_ref[:, 4 * d:5 * d]
    gate2 = mod_ref[:, 5 * d:6 * d]
    sub = x_ref.shape[0] // n_sub
    blocks = [slice(s * sub, (s + 1) * sub) for s in range(n_sub)]
    x1s, h2s = [], []
    for rs in blocks:
        mo = _dot(og_ref[rs, :], woa_ref[...]) + _dot(fo_ref[rs, :], wob_ref[...])
        x1 = x_ref[rs, :] + gate1 * mo
        hn = x1 * lax.rsqrt(jnp.mean(x1 * x1, axis=-1, keepdims=True) + RMS_EPS) * gffn_ref[...]
        x1s.append(x1)
        h2s.append((hn * (1.0 + scale2) + shift2).astype(BF16))
    accs = [jnp.zeros((sub, d), F32) for _ in blocks]
    for j in range(D_FF // FF_BLOCK):
        sl = slice(j * FF_BLOCK, (j + 1) * FF_BLOCK)
        for s in range(n_sub):
            gt = _dot(h2s[s], wg_ref[:, sl])
            up = _dot(h2s[s], wu_ref[:, sl])
            accs[s] = accs[s] + _dot((_silu(gt) * up).astype(BF16), wd_ref[sl, :])
    for rs, x1, acc in zip(blocks, x1s, accs):
        x2 = x1 + gate2 * acc
        y_ref[rs, :] = x2 * lax.rsqrt(jnp.mean(x2 * x2, axis=-1, keepdims=True) + RMS_EPS) * gfin_ref[...]


def _ffn_call(x, og, fo, mod3, mod_row0, w_out_a, w_out_b, g_ffn, w_g, w_u, w_down, g_final, tm):
    b, n, d = x.shape
    tok = lambda w: pl.BlockSpec((None, tm, w), lambda i, t: (i, t, 0))
    const = lambda s: pl.BlockSpec(s, lambda i, t: (0,) * len(s), pipeline_mode=pl.Buffered(1))
    return pl.pallas_call(
        functools.partial(_ffn_kernel, n_sub=tm // FFN_SUB_ROWS),
        grid=(b, n // tm),
        in_specs=[tok(d), tok(QK_WIDTH), tok(FOURIER_WIDTH),
                  pl.BlockSpec((None, 1, 6 * d), lambda i, t: (mod_row0 + i, 0, 0)),
                  const((QK_WIDTH, d)), const((FOURIER_WIDTH, d)), const((1, d)),
                  const((d, D_FF)), const((d, D_FF)), const((D_FF, d)), const((1, d))],
        out_specs=tok(d),
        out_shape=jax.ShapeDtypeStruct((b, n, d), F32),
        compiler_params=_params(),
    )(x, og, fo, mod3, w_out_a, w_out_b, g_ffn.reshape(1, d), w_g, w_u, w_down, g_final.reshape(1, d))


def _chunk_transposed(ab):
    b, n, _ = ab.shape
    g = ab[:, :, 0:2 * N_DIR * N_HEADS].reshape(b, n // CHUNK, CHUNK, 2 * N_DIR * N_HEADS)
    return jnp.swapaxes(g, 2, 3)


def kernel(x_prompt, x_sample, c, state_dn_fwd, state_dn_bwd, c_ctx, w_ada, b_ada, g_mix, w_in, w_conv,
           a_log, dt_bias, g_o, w_fno, w_out, g_ffn, w_gu, w_down, g_final):
    d = D_MODEL
    bp, np_, _ = x_prompt.shape
    bs, ns, _ = x_sample.shape
    l = 0

    wi = w_in[l]
    n_gate = 2 * N_DIR * N_HEADS
    g0 = QKV_WIDTH + QK_WIDTH
    w_cat = jnp.concatenate([wi[:, 0:g0], wi[:, g0 + n_gate:], wi[:, g0:g0 + n_gate],
                             jnp.zeros((d, LANES - n_gate), F32)], axis=1).astype(BF16)
    w_out_b16 = w_out[l].astype(BF16)
    w_out_a, w_out_b = w_out_b16[0:QK_WIDTH], w_out_b16[QK_WIDTH:]
    w_g = w_gu[l][:, 0:D_FF].astype(BF16)
    w_u = w_gu[l][:, D_FF:].astype(BF16)
    w_dn = w_down[l].astype(BF16)

    cond = jnp.concatenate([c_ctx[None, :], c, jnp.zeros((16 - 1 - bs, d), F32)], axis=0)
    mod = _mod_call(cond, w_ada[l], b_ada[l])
    mod3 = mod.reshape(16, 1, 6 * d)

    cc, sc = _dft_tables(HEAD_DIM)
    wy_p, wy_g = _fno_w_call(w_fno[l], jnp.asarray(cc), jnp.asarray(sc))

    xp = x_prompt.reshape(1, bp * np_, d)
    qkv, z, f, ab = _inproj_call(xp, mod3, 0, False, g_mix[l], w_cat, w_conv[l], 512, np_)
    qkv = qkv.reshape(bp, np_, QKV_WIDTH)
    z = z.reshape(bp, np_, QK_WIDTH)
    f = f.reshape(bp, np_, FOURIER_WIDTH)
    ab = ab.reshape(bp, np_, LANES)
    og, new_f, new_b = _delta_call(qkv, z, _chunk_transposed(ab), a_log[l], dt_bias[l], g_o[l], None, None,
                                   heads_per_step=4, unroll=2)
    fo = _fno_prompt_call(f, wy_p)
    y_prompt = _ffn_call(xp, og.reshape(1, bp * np_, QK_WIDTH), fo.reshape(1, bp * np_, FOURIER_WIDTH),
                         mod3, 0, w_out_a, w_out_b, g_ffn[l], w_g, w_u, w_dn, g_final, 512)
    y_prompt = y_prompt.reshape(bp, np_, d)

    qkv, z, f, ab = _inproj_call(x_sample, mod3, 1, True, g_mix[l], w_cat, w_conv[l], 512, ns)
    og, _, _ = _delta_call(qkv, z, _chunk_transposed(ab), a_log[l], dt_bias[l], g_o[l],
                           state_dn_fwd[:, l:l + 1], state_dn_bwd[:, l:l + 1], heads_per_step=2, unroll=4)
    fo = _fno_grid_call(f, wy_g)
    y_sample = _ffn_call(x_sample, og, fo, mod3, 1, w_out_a, w_out_b, g_ffn[l], w_g, w_u, w_dn, g_final, 1024)

    return (y_prompt, y_sample, new_f, new_b)
```

```python
import functools

import numpy as np
import jax
import jax.numpy as jnp
from jax import lax
from jax.experimental import pallas as pl
from jax.experimental.pallas import tpu as pltpu

D_MODEL = 1024
N_HEADS = 4
HEAD_DIM = 128
QK_WIDTH = N_HEADS * HEAD_DIM
QKV_WIDTH = 3 * QK_WIDTH
N_GROUPS = 4
FOURIER_WIDTH = N_GROUPS * HEAD_DIM
N_DIR = 2
GRID_W = 64
CHUNK = 128
D_FF = 2816
RMS_EPS = 1e-6
LANES = 128
W_CAT_COLS = QKV_WIDTH + QK_WIDTH + FOURIER_WIDTH + LANES
VMEM_LIMIT = 56 * 1024 * 1024

F32 = jnp.float32
BF16 = jnp.bfloat16
HIGHEST = lax.Precision.HIGHEST


def _dot(a, b):
    return jnp.dot(a, b, preferred_element_type=F32)


def _silu(x):
    return x * jax.nn.sigmoid(x)


def _softplus(x):
    return jnp.maximum(x, 0.0) + jnp.log1p(jnp.exp(-jnp.abs(x)))


def _params(**kw):
    return pltpu.CompilerParams(vmem_limit_bytes=VMEM_LIMIT, **kw)


def _mod_kernel(cond_ref, w_ref, b_ref, o_ref):
    s = _silu(cond_ref[...]).astype(BF16)
    o_ref[...] = _dot(s, w_ref[...].astype(BF16)) + b_ref[...]


def _mod_call(cond, w_ada, b_ada):
    rows, d = cond.shape
    cols = w_ada.shape[1]
    tn = 1536
    return pl.pallas_call(
        _mod_kernel,
        grid=(cols // tn,),
        in_specs=[pl.BlockSpec((rows, d), lambda j: (0, 0)),
                  pl.BlockSpec((d, tn), lambda j: (0, j)),
                  pl.BlockSpec((1, tn), lambda j: (0, j))],
        out_specs=pl.BlockSpec((rows, tn), lambda j: (0, j)),
        out_shape=jax.ShapeDtypeStruct((rows, cols), F32),
        compiler_params=_params(),
    )(cond, w_ada, b_ada.reshape(1, cols))


def _fno_w_kernel(w_ref, cc_ref, sc_ref, wp_ref, wg_ref):
    w = w_ref[...]
    cw = jnp.dot(cc_ref[...], w, precision=HIGHEST, preferred_element_type=F32)
    sw = jnp.dot(sc_ref[...], w, precision=HIGHEST, preferred_element_type=F32)
    wp_ref[...] = jnp.concatenate([cw, sw], axis=1).astype(BF16)
    wg_ref[...] = jnp.concatenate([cw, -sw, -sw, -cw], axis=1).astype(BF16)


def _fno_w_call(w_fno, cc, sc):
    g, c, _ = w_fno.shape
    return pl.pallas_call(
        _fno_w_kernel,
        grid=(g,),
        in_specs=[pl.BlockSpec((None, c, c), lambda i: (i, 0, 0)),
                  pl.BlockSpec((c, c), lambda i: (0, 0)),
                  pl.BlockSpec((c, c), lambda i: (0, 0))],
        out_specs=[pl.BlockSpec((None, c, 2 * c), lambda i: (i, 0, 0)),
                   pl.BlockSpec((None, c, 4 * c), lambda i: (i, 0, 0))],
        out_shape=[jax.ShapeDtypeStruct((g, c, 2 * c), BF16),
                   jax.ShapeDtypeStruct((g, c, 4 * c), BF16)],
        compiler_params=_params(),
    )(w_fno, cc, sc)


HALO = 8
CONV_COLS = 2 * HEAD_DIM
CONV_ROWS = 128


def _inproj_kernel(xp_ref, x_ref, xn_ref, mod_ref, g_ref, w_ref, wc_ref, qkv_ref, z_ref, f_ref, ab_ref, *,
                   tm, seq_len):
    t = pl.program_id(1)
    x = jnp.concatenate([xp_ref[...], x_ref[...], xn_ref[...]], axis=0)
    y = x * lax.rsqrt(jnp.mean(x * x, axis=-1, keepdims=True) + RMS_EPS) * g_ref[...]
    shift1 = mod_ref[:, 0:D_MODEL]
    scale1 = mod_ref[:, D_MODEL:2 * D_MODEL]
    h_all = (y * (1.0 + scale1) + shift1).astype(BF16)
    h = h_all[HALO:HALO + tm, :]
    at_start = ((t * tm) & (seq_len - 1)) == 0
    at_end = (((t + 1) * tm) & (seq_len - 1)) == 0
    zero_halo = jnp.zeros((HALO, D_MODEL), BF16)
    h_ext = jnp.concatenate([jnp.where(at_start, zero_halo, h_all[0:HALO, :]), h,
                             jnp.where(at_end, zero_halo, h_all[HALO + tm:, :])], axis=0)

    rows = lax.broadcasted_iota(jnp.int32, (CONV_ROWS, HEAD_DIM), 0)
    n_blocks = tm // CONV_ROWS
    for j in range(QKV_WIDTH // CONV_COLS):
        p_wide = _dot(h_ext, w_ref[:, j * CONV_COLS:(j + 1) * CONV_COLS])
        for hh in range(CONV_COLS // HEAD_DIM):
            head = j * (CONV_COLS // HEAD_DIM) + hh
            cols = slice(head * HEAD_DIM, (head + 1) * HEAD_DIM)
            lanes = slice(hh * HEAD_DIM, (hh + 1) * HEAD_DIM)
            for rb in range(n_blocks):
                lo = HALO + rb * CONV_ROWS
                p = p_wide[lo:lo + CONV_ROWS, lanes]
                starts_seq = rb > 0 and (rb * CONV_ROWS) % seq_len == 0
                ends_seq = rb < n_blocks - 1 and ((rb + 1) * CONV_ROWS) % seq_len == 0
                before = 0.0 if starts_seq else p_wide[lo - 1:lo, lanes]
                after = 0.0 if ends_seq else p_wide[lo + CONV_ROWS:lo + CONV_ROWS + 1, lanes]
                p_prev = jnp.where(rows == 0, before, pltpu.roll(p, 1, 0))
                p_next = jnp.where(rows == CONV_ROWS - 1, after, pltpu.roll(p, CONV_ROWS - 1, 0))
                ch = _silu(wc_ref[0:1, cols] * p_prev + wc_ref[1:2, cols] * p + wc_ref[2:3, cols] * p_next)
                if head < 2 * N_HEADS:
                    ch = ch * lax.rsqrt(jnp.sum(ch * ch, axis=-1, keepdims=True) + 1e-6)
                if head < N_HEADS:
                    ch = ch * (HEAD_DIM ** -0.5)
                qkv_ref[rb * CONV_ROWS:(rb + 1) * CONV_ROWS, cols] = ch.astype(BF16)
    c0, c1, c2 = QKV_WIDTH, QKV_WIDTH + QK_WIDTH, QKV_WIDTH + QK_WIDTH + FOURIER_WIDTH
    z_ref[...] = _dot(h, w_ref[:, c0:c1]).astype(BF16)
    f_ref[...] = _dot(h, w_ref[:, c1:c2]).astype(BF16)
    ab_ref[...] = _dot(h, w_ref[:, c2:W_CAT_COLS])


def _inproj_call(x, mod3, mod_row0, mod_per_batch, g_mix, w_cat, w_conv, tm, seq_len):
    b, n, d = x.shape
    assert seq_len & (seq_len - 1) == 0 and n % tm == 0 and tm % CONV_ROWS == 0 and seq_len % CONV_ROWS == 0
    assert tm % seq_len == 0 or seq_len % tm == 0
    tok = lambda w: pl.BlockSpec((None, tm, w), lambda i, t: (i, t, 0))
    per = tm // HALO
    return pl.pallas_call(
        functools.partial(_inproj_kernel, tm=tm, seq_len=seq_len),
        grid=(b, n // tm),
        in_specs=[pl.BlockSpec((None, HALO, d), lambda i, t: (i, jnp.maximum(t * per - 1, 0), 0)),
                  tok(d),
                  pl.BlockSpec((None, HALO, d), lambda i, t: (i, jnp.minimum((t + 1) * per, n // HALO - 1), 0)),
                  pl.BlockSpec((None, 1, 6 * d), lambda i, t: (mod_row0 + (i if mod_per_batch else 0), 0, 0)),
                  pl.BlockSpec((1, d), lambda i, t: (0, 0)),
                  pl.BlockSpec((d, W_CAT_COLS), lambda i, t: (0, 0)),
                  pl.BlockSpec((3, QKV_WIDTH), lambda i, t: (0, 0))],
        out_specs=[tok(QKV_WIDTH), tok(QK_WIDTH), tok(FOURIER_WIDTH), tok(LANES)],
        out_shape=[jax.ShapeDtypeStruct((b, n, QKV_WIDTH), BF16),
                   jax.ShapeDtypeStruct((b, n, QK_WIDTH), BF16),
                   jax.ShapeDtypeStruct((b, n, FOURIER_WIDTH), BF16),
                   jax.ShapeDtypeStruct((b, n, LANES), F32)],
        compiler_params=_params(),
    )(x, x, x, mod3, g_mix.reshape(1, d), w_cat, w_conv)


LEVEL_BLOCKS = tuple(2 ** e for e in range(int(np.log2(CHUNK))))


def _tri_inverse(a_list, fold_eye, fold_masks, top_mask, lane_lo):
    half = CHUNK // 2

    def unfold(f):
        return jnp.concatenate([jnp.where(lane_lo, f, 0.0), jnp.where(lane_lo, 0.0, f)], axis=0)

    a_folds = [jnp.where(lane_lo, a[0:half, :], a[half:, :]) for a in a_list]
    fs = [fold_eye - jnp.where(fold_masks[0], af, 0.0) for af in a_folds]
    for mask in fold_masks[1:]:
        ls = [unfold(jnp.where(mask, af, 0.0)).astype(BF16) for af in a_folds]
        ms = [_dot(f.astype(BF16), l).astype(BF16) for f, l in zip(fs, ls)]
        fs = [f - _dot(m, unfold(f).astype(BF16)) for f, m in zip(fs, ms)]
    ds = [unfold(f) for f in fs]
    ls = [jnp.where(top_mask, a, 0.0).astype(BF16) for a in a_list]
    dbs = [d.astype(BF16) for d in ds]
    ms = [_dot(db, l).astype(BF16) for db, l in zip(dbs, ls)]
    return [d - _dot(m, db) for d, m, db in zip(ds, ms, dbs)]


def _delta_kernel(*refs, n, hp_n, unroll, has_state):
    if has_state:
        (gp_ref, q_ref, k_ref, v_ref, z_ref, abt_ref, go_ref, trir_ref, sf0_ref, sb0_ref,
         og_ref, sf_ref, sb_ref, lhs_sc, nst_sc, gls, osc) = refs
    else:
        (gp_ref, q_ref, k_ref, v_ref, z_ref, abt_ref, go_ref, trir_ref,
         og_ref, sf_ref, sb_ref, lhs_sc, nst_sc, gls, osc) = refs
    hg = pl.program_id(1)
    nc = n // CHUNK

    ci = lax.broadcasted_iota(jnp.int32, (CHUNK, CHUNK), 0)
    cj = lax.broadcasted_iota(jnp.int32, (CHUNK, CHUNK), 1)
    top_mask = (ci // (CHUNK // 2)) != (cj // (CHUNK // 2))
    fi = lax.broadcasted_iota(jnp.int32, (CHUNK // 2, CHUNK), 0)
    fl = lax.broadcasted_iota(jnp.int32, (CHUNK // 2, CHUNK), 1)
    fj = fl & (CHUNK // 2 - 1)
    lane_lo = fl < CHUNK // 2
    fold_eye = (fi == fj).astype(F32)
    fold_masks = [((fi // (2 * b)) == (fj // (2 * b))) & ((fi // b) != (fj // b)) for b in LEVEL_BLOCKS[:-1]]
    incl = (ci >= cj, ci <= cj)
    strict = (ci > cj, ci < cj)
    n_half = N_DIR * N_HEADS
    gate_rows = lax.broadcasted_iota(jnp.int32, (n_half, CHUNK), 0)

    def head_cols(hp):
        return slice(hp * HEAD_DIM, (hp + 1) * HEAD_DIM)

    def split3_lanes(x):
        hi = x.astype(BF16)
        r1 = x - hi.astype(F32)
        lo = r1.astype(BF16)
        lo2 = (r1 - lo.astype(F32)).astype(BF16)
        return jnp.concatenate([hi, lo, lo2], axis=1)

    def select_col(x, col):
        lanes = lax.broadcasted_iota(jnp.int32, x.shape, 1)
        return jnp.sum(jnp.where(lanes == col, x, 0.0), axis=-1, keepdims=True)

    def chain_front(hp, c, d, q, k, v, kk, qk, cum_rows, gate_cols):
        gate = d * N_HEADS + hg * hp_n + hp
        g_col = jnp.broadcast_to(select_col(gate_cols, d * n_half + gate), (CHUNK, LANES))
        beta = select_col(gate_cols, N_DIR * n_half + gate)
        g_row = jnp.sum(jnp.where(gate_rows == gate, cum_rows[d], 0.0), axis=0, keepdims=True)
        g_tot = g_col[CHUNK - 1:CHUNK, :] if d == 0 else g_col[0:1, :]
        decay = jnp.exp(jnp.where(incl[d], g_col - g_row, -jnp.inf))
        eg = jnp.exp(g_col)
        gls[hp, d, c] = jnp.broadcast_to(jnp.exp(g_tot), (8, LANES))
        a = jnp.where(strict[d], beta * decay * kk, 0.0)
        rhs = jnp.concatenate([beta * v, (beta * eg) * k], axis=1).astype(BF16)
        lhs2 = jnp.concatenate([(k * jnp.exp(g_tot - g_col)).T.astype(BF16), (qk * decay).astype(BF16)], axis=0)
        return a, rhs, lhs2, q * eg

    def prepare(i, carry):
        raws = [abt_ref[i * unroll + j] for j in range(unroll)]
        log_as = [-jnp.exp(gp_ref[0:n_half, :]) * _softplus(raw[0:n_half, :] + gp_ref[n_half:, :]) for raw in raws]
        cum_f_all = _dot(split3_lanes(jnp.concatenate(log_as, axis=0)), trir_ref[...])
        gate_info = []
        for j in range(unroll):
            cum_f = cum_f_all[j * n_half:(j + 1) * n_half, :]
            cum_b = cum_f[:, CHUNK - 1:CHUNK] - cum_f + log_as[j]
            gate_cols = jnp.concatenate([cum_f, cum_b, jax.nn.sigmoid(raws[j][n_half:, :]),
                                         jnp.zeros((CHUNK - 3 * n_half, CHUNK), F32)], axis=0).T
            gate_info.append(((cum_f, cum_b), gate_cols))
        chains = []
        for hp in range(hp_n):
            for j in range(unroll):
                c = i * unroll + j
                r0 = pl.multiple_of(c * CHUNK, CHUNK)
                qb = q_ref[pl.ds(r0, CHUNK), head_cols(hp)]
                kb = k_ref[pl.ds(r0, CHUNK), head_cols(hp)]
                q = qb.astype(F32)
                k = kb.astype(F32)
                v = v_ref[pl.ds(r0, CHUNK), head_cols(hp)].astype(F32)
                kq = lax.dot_general(jnp.concatenate([kb, qb], axis=0), kb,
                                     (((1,), (1,)), ((), ())), preferred_element_type=F32)
                for d in range(N_DIR):
                    chains.append((hp, j, c, d) + chain_front(hp, c, d, q, k, v, kq[0:CHUNK, :], kq[CHUNK:, :],
                                                              *gate_info[j]))
        t_invs = _tri_inverse([ch[4] for ch in chains], fold_eye, fold_masks, top_mask, lane_lo)
        o_local = {}
        for (hp, j, c, d, _, rhs, lhs2, qg), t_inv in zip(chains, t_invs):
            sol = _dot(t_inv.astype(BF16), rhs).astype(BF16)
            x = _dot(lhs2, sol)
            nst_sc[hp, d, c] = x[0:HEAD_DIM, 0:HEAD_DIM]
            lhs_sc[hp, d, c, 0:HEAD_DIM, :] = (-x[0:HEAD_DIM, HEAD_DIM:]).astype(BF16)
            lhs_sc[hp, d, c, HEAD_DIM:, :] = (qg - x[HEAD_DIM:, HEAD_DIM:]).astype(BF16)
            o_local[(hp, j, d)] = x[HEAD_DIM:, 0:HEAD_DIM]
        for hp in range(hp_n):
            for j in range(unroll):
                c = i * unroll + j
                r0 = pl.multiple_of(c * CHUNK, CHUNK)
                osc[hp, pl.ds(r0, CHUNK), :] = o_local[(hp, j, 0)] + o_local[(hp, j, 1)]
        return carry

    lax.fori_loop(0, nc // unroll, prepare, 0)

    def scan_step(hp, c, d, s):
        r0 = pl.multiple_of(c * CHUNK, CHUNK)
        r = _dot(lhs_sc[hp, d, c], s.astype(BF16))
        osc[hp, pl.ds(r0, CHUNK), :] += r[HEAD_DIM:, :]
        return gls[hp, d, c][0:1, :] * s + r[0:HEAD_DIM, :] + nst_sc[hp, d, c]

    def finalize(hp, c):
        r0 = pl.multiple_of(c * CHUNK, CHUNK)
        o = osc[hp, pl.ds(r0, CHUNK), :]
        y = o * lax.rsqrt(jnp.mean(o * o, axis=-1, keepdims=True) + RMS_EPS) * go_ref[...]
        zz = z_ref[pl.ds(r0, CHUNK), head_cols(hp)].astype(F32)
        og_ref[pl.ds(r0, CHUNK), head_cols(hp)] = (y * _silu(zz)).astype(BF16)

    def scan_body(i, carry, finish):
        out = []
        for hp in range(hp_n):
            out.append(scan_step(hp, i, 0, carry[2 * hp]))
            out.append(scan_step(hp, nc - 1 - i, 1, carry[2 * hp + 1]))
        if finish:
            for hp in range(hp_n):
                finalize(hp, i)
                finalize(hp, nc - 1 - i)
        return tuple(out)

    init = []
    for hp in range(hp_n):
        if has_state:
            init += [sf0_ref[hp], sb0_ref[hp]]
        else:
            init += [jnp.zeros((HEAD_DIM, HEAD_DIM), F32), jnp.zeros((HEAD_DIM, HEAD_DIM), F32)]
    half = lax.fori_loop(0, nc // 2, functools.partial(scan_body, finish=False), tuple(init))
    fin = lax.fori_loop(nc // 2, nc, functools.partial(scan_body, finish=True), half)
    for hp in range(hp_n):
        sf_ref[hp] = fin[2 * hp]
        sb_ref[hp] = fin[2 * hp + 1]


def _tri_tables():
    i = np.arange(CHUNK)
    up = (i[:, None] <= i[None, :]).astype(np.float32)
    return jnp.asarray(np.concatenate([up, up, up], axis=0)).astype(BF16)


def _delta_call(qkv, z, abt, a_log, dt_bias, g_o, s0_f, s0_b, *, heads_per_step, unroll):
    b, n, _ = qkv.shape
    nc = n // CHUNK
    assert nc % 2 == 0 and nc % unroll == 0
    hp_n = heads_per_step
    groups = N_HEADS // hp_n
    wide = hp_n * HEAD_DIM
    has_state = s0_f is not None
    trir = _tri_tables()
    col = lambda off: pl.BlockSpec((None, n, wide), lambda i, g: (i, 0, off + g))
    st = pl.BlockSpec((None, None, hp_n, HEAD_DIM, HEAD_DIM), lambda i, g: (i, 0, g, 0, 0))
    n_gate = 2 * N_DIR * N_HEADS
    gate_params = jnp.broadcast_to(jnp.concatenate([a_log.reshape(-1), dt_bias.reshape(-1)])[:, None],
                                   (n_gate, CHUNK))
    in_specs = [pl.BlockSpec((n_gate, CHUNK), lambda i, g: (0, 0)),
                col(0), col(groups), col(2 * groups), col(0),
                pl.BlockSpec((None, nc, n_gate, CHUNK), lambda i, g: (i, 0, 0, 0)),
                pl.BlockSpec((1, HEAD_DIM), lambda i, g: (0, 0)),
                pl.BlockSpec((3 * CHUNK, CHUNK), lambda i, g: (0, 0))]
    args = [gate_params, qkv, qkv, qkv, z, abt, g_o.reshape(1, HEAD_DIM), trir]
    if has_state:
        in_specs += [st, st]
        args += [s0_f, s0_b]
    state_shape = jax.ShapeDtypeStruct((b, 1, N_HEADS, HEAD_DIM, HEAD_DIM), F32)
    return pl.pallas_call(
        functools.partial(_delta_kernel, n=n, hp_n=hp_n, unroll=unroll, has_state=has_state),
        grid=(b, groups),
        in_specs=in_specs,
        out_specs=[col(0), st, st],
        out_shape=[jax.ShapeDtypeStruct((b, n, QK_WIDTH), BF16), state_shape, state_shape],
        scratch_shapes=[pltpu.VMEM((hp_n, N_DIR, nc, HEAD_DIM + CHUNK, HEAD_DIM), BF16),
                        pltpu.VMEM((hp_n, N_DIR, nc, HEAD_DIM, HEAD_DIM), F32),
                        pltpu.VMEM((hp_n, N_DIR, nc, 8, LANES), F32),
                        pltpu.VMEM((hp_n, n, HEAD_DIM), F32)],
        compiler_params=_params(),
    )(*args)


def _dft_tables(n, scale=1.0):
    idx = np.arange(n)
    ang = 2.0 * np.pi * ((idx[:, None] * idx[None, :]) % n) / n
    return (np.cos(ang) * scale).astype(np.float32), (np.sin(ang) * scale).astype(np.float32)


def _fno_prompt_kernel(f_ref, wy_ref, cn_ref, sn_ref, o_ref):
    cn = cn_ref[...]
    sn = sn_ref[...]
    for g in range(N_GROUPS):
        y = _dot(f_ref[:, g * HEAD_DIM:(g + 1) * HEAD_DIM], wy_ref[g])
        o = _dot(cn, y[:, 0:HEAD_DIM].astype(BF16)) + _dot(sn, y[:, HEAD_DIM:].astype(BF16))
        o_ref[:, g * HEAD_DIM:(g + 1) * HEAD_DIM] = o.astype(BF16)


def _fno_prompt_call(f, wy):
    b, n, w = f.shape
    cn, sn = _dft_tables(n, (n * HEAD_DIM) ** -0.5)
    cn = jnp.asarray(cn).astype(BF16)
    sn_neg = jnp.asarray(-sn).astype(BF16)
    return pl.pallas_call(
        _fno_prompt_kernel,
        grid=(b,),
        in_specs=[pl.BlockSpec((None, n, w), lambda i: (i, 0, 0)),
                  pl.BlockSpec((N_GROUPS, HEAD_DIM, 2 * HEAD_DIM), lambda i: (0, 0, 0)),
                  pl.BlockSpec((n, n), lambda i: (0, 0)),
                  pl.BlockSpec((n, n), lambda i: (0, 0))],
        out_specs=pl.BlockSpec((None, n, w), lambda i: (i, 0, 0)),
        out_shape=jax.ShapeDtypeStruct((b, n, w), BF16),
        compiler_params=_params(),
    )(f, wy, cn, sn_neg)


COL_UNROLL = 4
ROW_UNROLL = 8
ROW_PITCH = GRID_W + 8


def _fno_grid_kernel(f_ref, wy_ref, bdc_ref, bds_ref, crs_ref, o_ref, zr_sc, zi_sc, o_sc, *, n):
    tb = bdc_ref.shape[0]
    two = 2 * HEAD_DIM
    rows = n // GRID_W
    rows_per_block = tb // GRID_W

    def col_body(i, carry):
        blocks = [i * COL_UNROLL + j for j in range(COL_UNROLL)]
        ys = [_dot(f_ref[pl.ds(pl.multiple_of(blk * tb, tb), tb), :], wy_ref[...]).astype(BF16)
              for blk in blocks]
        zs = [_dot(bdc_ref[...], y[:, 0:two]) + _dot(bds_ref[...], y[:, two:]) for y in ys]
        for blk, z in zip(blocks, zs):
            for rr in range(rows_per_block):
                dst = pl.ds(pl.multiple_of((blk * rows_per_block + rr) * ROW_PITCH, 8), GRID_W)
                zr_sc[dst, :] = z[rr * GRID_W:(rr + 1) * GRID_W, 0:HEAD_DIM]
                zi_sc[dst, :] = z[rr * GRID_W:(rr + 1) * GRID_W, HEAD_DIM:]
        return carry

    lax.fori_loop(0, n // (tb * COL_UNROLL), col_body, 0)

    def row_body(i, carry):
        cols = [pl.ds(i * ROW_UNROLL + j, rows, stride=ROW_PITCH) for j in range(ROW_UNROLL)]
        zs = [jnp.concatenate([zr_sc[col, :], zi_sc[col, :]], axis=0).astype(BF16) for col in cols]
        outs = [_dot(crs_ref[...], z) for z in zs]
        for col, o in zip(cols, outs):
            o_sc[col, :] = o
        return carry

    lax.fori_loop(0, GRID_W // ROW_UNROLL, row_body, 0)

    def out_body(r, carry):
        src = pl.ds(pl.multiple_of(r * ROW_PITCH, 8), GRID_W)
        o_ref[pl.ds(pl.multiple_of(r * GRID_W, GRID_W), GRID_W), :] = o_sc[src, :].astype(BF16)
        return carry

    lax.fori_loop(0, rows, out_body, 0)


def _fno_grid_call(f, wy4):
    b, n, w = f.shape
    rows = n // GRID_W
    tb = 256
    cw, sw = _dft_tables(GRID_W)
    rep = np.eye(tb // GRID_W, dtype=np.float32)
    bdc = jnp.asarray(np.kron(rep, cw)).astype(BF16)
    bds = jnp.asarray(np.kron(rep, sw)).astype(BF16)
    cr, sr = _dft_tables(rows, (n * HEAD_DIM) ** -0.5)
    crs = jnp.asarray(np.concatenate([cr, sr], axis=1)).astype(BF16)
    const = lambda s: pl.BlockSpec(s, lambda i, g: (0,) * len(s))
    return pl.pallas_call(
        functools.partial(_fno_grid_kernel, n=n),
        grid=(b, N_GROUPS),
        in_specs=[pl.BlockSpec((None, n, HEAD_DIM), lambda i, g: (i, 0, g)),
                  pl.BlockSpec((None, HEAD_DIM, 4 * HEAD_DIM), lambda i, g: (g, 0, 0)),
                  const((tb, tb)), const((tb, tb)), const((rows, 2 * rows))],
        out_specs=pl.BlockSpec((None, n, HEAD_DIM), lambda i, g: (i, 0, g)),
        out_shape=jax.ShapeDtypeStruct((b, n, w), BF16),
        scratch_shapes=[pltpu.VMEM((rows * ROW_PITCH, HEAD_DIM), F32)] * 3,
        compiler_params=_params(),
    )(f, wy4, bdc, bds, crs)


FF_BLOCK = 256
FFN_SUB_ROWS = 512


def _ffn_kernel(x_ref, og_ref, fo_ref, mod_ref, woa_ref, wob_ref, gffn_ref, wg_ref, wu_ref, wd_ref,
                gfin_ref, y_ref, *, n_sub):
    d = D_MODEL
    gate1 = mod_ref[:, 2 * d:3 * d]
    shift2 = mod_ref[:, 3 * d:4 * d]
    scale2 = mod_ref[:, 4 * d:5 * d]
    gate2 = mod_ref[:, 5 * d:6 * d]
    sub = x_ref.shape[0] // n_sub
    blocks = [slice(s * sub, (s + 1) * sub) for s in range(n_sub)]
    x1s, h2s = [], []
    for rs in blocks:
        mo = _dot(og_ref[rs, :], woa_ref[...]) + _dot(fo_ref[rs, :], wob_ref[...])
        x1 = x_ref[rs, :] + gate1 * mo
        hn = x1 * lax.rsqrt(jnp.mean(x1 * x1, axis=-1, keepdims=True) + RMS_EPS) * gffn_ref[...]
        x1s.append(x1)
        h2s.append((hn * (1.0 + scale2) + shift2).astype(BF16))
    accs = [jnp.zeros((sub, d), F32) for _ in blocks]
    for j in range(D_FF // FF_BLOCK):
        sl = slice(j * FF_BLOCK, (j + 1) * FF_BLOCK)
        for s in range(n_sub):
            gt = _dot(h2s[s], wg_ref[:, sl])
            up = _dot(h2s[s], wu_ref[:, sl])
            accs[s] = accs[s] + _dot((_silu(gt) * up).astype(BF16), wd_ref[sl, :])
    for rs, x1, acc in zip(blocks, x1s, accs):
        x2 = x1 + gate2 * acc
        y_ref[rs, :] = x2 * lax.rsqrt(jnp.mean(x2 * x2, axis=-1, keepdims=True) + RMS_EPS) * gfin_ref[...]


def _ffn_call(x, og, fo, mod3, mod_row0, w_out_a, w_out_b, g_ffn, w_g, w_u, w_down, g_final, tm):
    b, n, d = x.shape
    tok = lambda w: pl.BlockSpec((None, tm, w), lambda i, t: (i, t, 0))
    const = lambda s: pl.BlockSpec(s, lambda i, t: (0,) * len(s), pipeline_mode=pl.Buffered(1))
    return pl.pallas_call(
        functools.partial(_ffn_kernel, n_sub=tm // FFN_SUB_ROWS),
        grid=(b, n // tm),
        in_specs=[tok(d), tok(QK_WIDTH), tok(FOURIER_WIDTH),
                  pl.BlockSpec((None, 1, 6 * d), lambda i, t: (mod_row0 + i, 0, 0)),
                  const((QK_WIDTH, d)), const((FOURIER_WIDTH, d)), const((1, d)),
                  const((d, D_FF)), const((d, D_FF)), const((D_FF, d)), const((1, d))],
        out_specs=tok(d),
        out_shape=jax.ShapeDtypeStruct((b, n, d), F32),
        compiler_params=_params(),
    )(x, og, fo, mod3, w_out_a, w_out_b, g_ffn.reshape(1, d), w_g, w_u, w_down, g_final.reshape(1, d))


def _chunk_transposed(ab):
    b, n, _ = ab.shape
    g = ab[:, :, 0:2 * N_DIR * N_HEADS].reshape(b, n // CHUNK, CHUNK, 2 * N_DIR * N_HEADS)
    return jnp.swapaxes(g, 2, 3)


def kernel(x_prompt, x_sample, c, state_dn_fwd, state_dn_bwd, c_ctx, w_ada, b_ada, g_mix, w_in, w_conv,
           a_log, dt_bias, g_o, w_fno, w_out, g_ffn, w_gu, w_down, g_final):
    d = D_MODEL
    bp, np_, _ = x_prompt.shape
    bs, ns, _ = x_sample.shape
    l = 0

    wi = w_in[l]
    n_gate = 2 * N_DIR * N_HEADS
    g0 = QKV_WIDTH + QK_WIDTH
    w_cat = jnp.concatenate([wi[:, 0:g0], wi[:, g0 + n_gate:], wi[:, g0:g0 + n_gate],
                             jnp.zeros((d, LANES - n_gate), F32)], axis=1).astype(BF16)
    w_out_b16 = w_out[l].astype(BF16)
    w_out_a, w_out_b = w_out_b16[0:QK_WIDTH], w_out_b16[QK_WIDTH:]
    w_g = w_gu[l][:, 0:D_FF].astype(BF16)
    w_u = w_gu[l][:, D_FF:].astype(BF16)
    w_dn = w_down[l].astype(BF16)

    cond = jnp.concatenate([c_ctx[None, :], c, jnp.zeros((16 - 1 - bs, d), F32)], axis=0)
    mod = _mod_call(cond, w_ada[l], b_ada[l])
    mod3 = mod.reshape(16, 1, 6 * d)

    cc, sc = _dft_tables(HEAD_DIM)
    wy_p, wy_g = _fno_w_call(w_fno[l], jnp.asarray(cc), jnp.asarray(sc))

    xp = x_prompt.reshape(1, bp * np_, d)
    qkv, z, f, ab = _inproj_call(xp, mod3, 0, False, g_mix[l], w_cat, w_conv[l], 512, np_)
    qkv = qkv.reshape(bp, np_, QKV_WIDTH)
    z = z.reshape(bp, np_, QK_WIDTH)
    f = f.reshape(bp, np_, FOURIER_WIDTH)
    ab = ab.reshape(bp, np_, LANES)
    og, new_f, new_b = _delta_call(qkv, z, _chunk_transposed(ab), a_log[l], dt_bias[l], g_o[l], None, None,
                                   heads_per_step=4, unroll=2)
    fo = _fno_prompt_call(f, wy_p)
    y_prompt = _ffn_call(xp, og.reshape(1, bp * np_, QK_WIDTH), fo.reshape(1, bp * np_, FOURIER_WIDTH),
                         mod3, 0, w_out_a, w_out_b, g_ffn[l], w_g, w_u, w_dn, g_final, 512)
    y_prompt = y_prompt.reshape(bp, np_, d)

    qkv, z, f, ab = _inproj_call(x_sample, mod3, 1, True, g_mix[l], w_cat, w_conv[l], 512, ns)
    og, _, _ = _delta_call(qkv, z, _chunk_transposed(ab), a_log[l], dt_bias[l], g_o[l],
                           state_dn_fwd[:, l:l + 1], state_dn_bwd[:, l:l + 1], heads_per_step=2, unroll=4)
    fo = _fno_grid_call(f, wy_g)
    y_sample = _ffn_call(x_sample, og, fo, mod3, 1, w_out_a, w_out_b, g_ffn[l], w_g, w_u, w_dn, g_final, 1024)

    return (y_prompt, y_sample, new_f, new_b)
```

```python
import functools

import numpy as np
import jax
import jax.numpy as jnp
from jax import lax
from jax.experimental import pallas as pl
from jax.experimental.pallas import tpu as pltpu

D_MODEL = 1024
N_HEADS = 4
HEAD_DIM = 128
QK_WIDTH = N_HEADS * HEAD_DIM
QKV_WIDTH = 3 * QK_WIDTH
N_GROUPS = 4
FOURIER_WIDTH = N_GROUPS * HEAD_DIM
N_DIR = 2
GRID_W = 64
CHUNK = 128
D_FF = 2816
RMS_EPS = 1e-6
LANES = 128
W_CAT_COLS = QKV_WIDTH + QK_WIDTH + FOURIER_WIDTH + LANES
VMEM_LIMIT = 56 * 1024 * 1024

F32 = jnp.float32
BF16 = jnp.bfloat16
HIGHEST = lax.Precision.HIGHEST


def _dot(a, b):
    return jnp.dot(a, b, preferred_element_type=F32)


def _silu(x):
    return x * jax.nn.sigmoid(x)


def _softplus(x):
    return jnp.maximum(x, 0.0) + jnp.log1p(jnp.exp(-jnp.abs(x)))


def _params(**kw):
    return pltpu.CompilerParams(vmem_limit_bytes=VMEM_LIMIT, **kw)


def _mod_kernel(cond_ref, w_ref, b_ref, o_ref):
    s = _silu(cond_ref[...]).astype(BF16)
    o_ref[...] = _dot(s, w_ref[...].astype(BF16)) + b_ref[...]


def _mod_call(cond, w_ada, b_ada):
    rows, d = cond.shape
    cols = w_ada.shape[1]
    tn = 1536
    return pl.pallas_call(
        _mod_kernel,
        grid=(cols // tn,),
        in_specs=[pl.BlockSpec((rows, d), lambda j: (0, 0)),
                  pl.BlockSpec((d, tn), lambda j: (0, j)),
                  pl.BlockSpec((1, tn), lambda j: (0, j))],
        out_specs=pl.BlockSpec((rows, tn), lambda j: (0, j)),
        out_shape=jax.ShapeDtypeStruct((rows, cols), F32),
        compiler_params=_params(),
    )(cond, w_ada, b_ada.reshape(1, cols))


def _fno_w_kernel(w_ref, cc_ref, sc_ref, wp_ref, wg_ref):
    w = w_ref[...]
    cw = jnp.dot(cc_ref[...], w, precision=HIGHEST, preferred_element_type=F32)
    sw = jnp.dot(sc_ref[...], w, precision=HIGHEST, preferred_element_type=F32)
    wp_ref[...] = jnp.concatenate([cw, sw], axis=1).astype(BF16)
    wg_ref[...] = jnp.concatenate([cw, -sw, -sw, -cw], axis=1).astype(BF16)


def _fno_w_call(w_fno, cc, sc):
    g, c, _ = w_fno.shape
    return pl.pallas_call(
        _fno_w_kernel,
        grid=(g,),
        in_specs=[pl.BlockSpec((None, c, c), lambda i: (i, 0, 0)),
                  pl.BlockSpec((c, c), lambda i: (0, 0)),
                  pl.BlockSpec((c, c), lambda i: (0, 0))],
        out_specs=[pl.BlockSpec((None, c, 2 * c), lambda i: (i, 0, 0)),
                   pl.BlockSpec((None, c, 4 * c), lambda i: (i, 0, 0))],
        out_shape=[jax.ShapeDtypeStruct((g, c, 2 * c), BF16),
                   jax.ShapeDtypeStruct((g, c, 4 * c), BF16)],
        compiler_params=_params(),
    )(w_fno, cc, sc)


HALO = 8
CONV_COLS = 2 * HEAD_DIM
CONV_ROWS = 128


def _inproj_kernel(xp_ref, x_ref, xn_ref, mod_ref, g_ref, w_ref, wc_ref, qkv_ref, z_ref, f_ref, ab_ref, *,
                   tm, seq_len):
    t = pl.program_id(1)
    x = jnp.concatenate([xp_ref[...], x_ref[...], xn_ref[...]], axis=0)
    y = x * lax.rsqrt(jnp.mean(x * x, axis=-1, keepdims=True) + RMS_EPS) * g_ref[...]
    shift1 = mod_ref[:, 0:D_MODEL]
    scale1 = mod_ref[:, D_MODEL:2 * D_MODEL]
    h_all = (y * (1.0 + scale1) + shift1).astype(BF16)
    h = h_all[HALO:HALO + tm, :]
    at_start = ((t * tm) & (seq_len - 1)) == 0
    at_end = (((t + 1) * tm) & (seq_len - 1)) == 0
    zero_halo = jnp.zeros((HALO, D_MODEL), BF16)
    h_ext = jnp.concatenate([jnp.where(at_start, zero_halo, h_all[0:HALO, :]), h,
                             jnp.where(at_end, zero_halo, h_all[HALO + tm:, :])], axis=0)

    rows = lax.broadcasted_iota(jnp.int32, (CONV_ROWS, HEAD_DIM), 0)
    n_blocks = tm // CONV_ROWS
    for j in range(QKV_WIDTH // CONV_COLS):
        p_wide = _dot(h_ext, w_ref[:, j * CONV_COLS:(j + 1) * CONV_COLS])
        for hh in range(CONV_COLS // HEAD_DIM):
            head = j * (CONV_COLS // HEAD_DIM) + hh
            cols = slice(head * HEAD_DIM, (head + 1) * HEAD_DIM)
            lanes = slice(hh * HEAD_DIM, (hh + 1) * HEAD_DIM)
            for rb in range(n_blocks):
                lo = HALO + rb * CONV_ROWS
                p = p_wide[lo:lo + CONV_ROWS, lanes]
                starts_seq = rb > 0 and (rb * CONV_ROWS) % seq_len == 0
                ends_seq = rb < n_blocks - 1 and ((rb + 1) * CONV_ROWS) % seq_len == 0
                before = 0.0 if starts_seq else p_wide[lo - 1:lo, lanes]
                after = 0.0 if ends_seq else p_wide[lo + CONV_ROWS:lo + CONV_ROWS + 1, lanes]
                p_prev = jnp.where(rows == 0, before, pltpu.roll(p, 1, 0))
                p_next = jnp.where(rows == CONV_ROWS - 1, after, pltpu.roll(p, CONV_ROWS - 1, 0))
                ch = _silu(wc_ref[0:1, cols] * p_prev + wc_ref[1:2, cols] * p + wc_ref[2:3, cols] * p_next)
                if head < 2 * N_HEADS:
                    ch = ch * lax.rsqrt(jnp.sum(ch * ch, axis=-1, keepdims=True) + 1e-6)
                if head < N_HEADS:
                    ch = ch * (HEAD_DIM ** -0.5)
                qkv_ref[rb * CONV_ROWS:(rb + 1) * CONV_ROWS, cols] = ch.astype(BF16)
    c0, c1, c2 = QKV_WIDTH, QKV_WIDTH + QK_WIDTH, QKV_WIDTH + QK_WIDTH + FOURIER_WIDTH
    z_ref[...] = _dot(h, w_ref[:, c0:c1]).astype(BF16)
    f_ref[...] = _dot(h, w_ref[:, c1:c2]).astype(BF16)
    ab_ref[...] = _dot(h, w_ref[:, c2:W_CAT_COLS])


def _inproj_call(x, mod3, mod_row0, mod_per_batch, g_mix, w_cat, w_conv, tm, seq_len):
    b, n, d = x.shape
    assert seq_len & (seq_len - 1) == 0 and n % tm == 0 and tm % CONV_ROWS == 0 and seq_len % CONV_ROWS == 0
    assert tm % seq_len == 0 or seq_len % tm == 0
    tok = lambda w: pl.BlockSpec((None, tm, w), lambda i, t: (i, t, 0))
    per = tm // HALO
    return pl.pallas_call(
        functools.partial(_inproj_kernel, tm=tm, seq_len=seq_len),
        grid=(b, n // tm),
        in_specs=[pl.BlockSpec((None, HALO, d), lambda i, t: (i, jnp.maximum(t * per - 1, 0), 0)),
                  tok(d),
                  pl.BlockSpec((None, HALO, d), lambda i, t: (i, jnp.minimum((t + 1) * per, n // HALO - 1), 0)),
                  pl.BlockSpec((None, 1, 6 * d), lambda i, t: (mod_row0 + (i if mod_per_batch else 0), 0, 0)),
                  pl.BlockSpec((1, d), lambda i, t: (0, 0)),
                  pl.BlockSpec((d, W_CAT_COLS), lambda i, t: (0, 0)),
                  pl.BlockSpec((3, QKV_WIDTH), lambda i, t: (0, 0))],
        out_specs=[tok(QKV_WIDTH), tok(QK_WIDTH), tok(FOURIER_WIDTH), tok(LANES)],
        out_shape=[jax.ShapeDtypeStruct((b, n, QKV_WIDTH), BF16),
                   jax.ShapeDtypeStruct((b, n, QK_WIDTH), BF16),
                   jax.ShapeDtypeStruct((b, n, FOURIER_WIDTH), BF16),
                   jax.ShapeDtypeStruct((b, n, LANES), F32)],
        compiler_params=_params(),
    )(x, x, x, mod3, g_mix.reshape(1, d), w_cat, w_conv)


LEVEL_BLOCKS = tuple(2 ** e for e in range(int(np.log2(CHUNK))))


def _tri_inverse(a_list, fold_eye, fold_masks, top_mask, lane_lo):
    half = CHUNK // 2

    def unfold(f):
        return jnp.concatenate([jnp.where(lane_lo, f, 0.0), jnp.where(lane_lo, 0.0, f)], axis=0)

    a_folds = [jnp.where(lane_lo, a[0:half, :], a[half:, :]) for a in a_list]
    fs = [fold_eye - jnp.where(fold_masks[0], af, 0.0) for af in a_folds]
    for mask in fold_masks[1:]:
        ls = [unfold(jnp.where(mask, af, 0.0)).astype(BF16) for af in a_folds]
        ms = [_dot(f.astype(BF16), l).astype(BF16) for f, l in zip(fs, ls)]
        fs = [f - _dot(m, unfold(f).astype(BF16)) for f, m in zip(fs, ms)]
    ds = [unfold(f) for f in fs]
    ls = [jnp.where(top_mask, a, 0.0).astype(BF16) for a in a_list]
    dbs = [d.astype(BF16) for d in ds]
    ms = [_dot(db, l).astype(BF16) for db, l in zip(dbs, ls)]
    return [d - _dot(m, db) for d, m, db in zip(ds, ms, dbs)]


def _delta_kernel(*refs, n, hp_n, unroll, has_state):
    if has_state:
        (gp_ref, q_ref, k_ref, v_ref, z_ref, abt_ref, go_ref, trir_ref, sf0_ref, sb0_ref,
         og_ref, sf_ref, sb_ref, lhs_sc, nst_sc, gls, osc) = refs
    else:
        (gp_ref, q_ref, k_ref, v_ref, z_ref, abt_ref, go_ref, trir_ref,
         og_ref, sf_ref, sb_ref, lhs_sc, nst_sc, gls, osc) = refs
    hg = pl.program_id(1)
    nc = n // CHUNK

    ci = lax.broadcasted_iota(jnp.int32, (CHUNK, CHUNK), 0)
    cj = lax.broadcasted_iota(jnp.int32, (CHUNK, CHUNK), 1)
    top_mask = (ci // (CHUNK // 2)) != (cj // (CHUNK // 2))
    fi = lax.broadcasted_iota(jnp.int32, (CHUNK // 2, CHUNK), 0)
    fl = lax.broadcasted_iota(jnp.int32, (CHUNK // 2, CHUNK), 1)
    fj = fl & (CHUNK // 2 - 1)
    lane_lo = fl < CHUNK // 2
    fold_eye = (fi == fj).astype(F32)
    fold_masks = [((fi // (2 * b)) == (fj // (2 * b))) & ((fi // b) != (fj // b)) for b in LEVEL_BLOCKS[:-1]]
    incl = (ci >= cj, ci <= cj)
    strict = (ci > cj, ci < cj)
    n_half = N_DIR * N_HEADS
    gate_rows = lax.broadcasted_iota(jnp.int32, (n_half, CHUNK), 0)

    def head_cols(hp):
        return slice(hp * HEAD_DIM, (hp + 1) * HEAD_DIM)

    def split3_lanes(x):
        hi = x.astype(BF16)
        r1 = x - hi.astype(F32)
        lo = r1.astype(BF16)
        lo2 = (r1 - lo.astype(F32)).astype(BF16)
        return jnp.concatenate([hi, lo, lo2], axis=1)

    def select_col(x, col):
        lanes = lax.broadcasted_iota(jnp.int32, x.shape, 1)
        return jnp.sum(jnp.where(lanes == col, x, 0.0), axis=-1, keepdims=True)

    def chain_front(hp, c, d, q, k, v, kk, qk, cum_rows, gate_cols):
        gate = d * N_HEADS + hg * hp_n + hp
        g_col = jnp.broadcast_to(select_col(gate_cols, d * n_half + gate), (CHUNK, LANES))
        beta = select_col(gate_cols, N_DIR * n_half + gate)
        g_row = jnp.sum(jnp.where(gate_rows == gate, cum_rows[d], 0.0), axis=0, keepdims=True)
        g_tot = g_col[CHUNK - 1:CHUNK, :] if d == 0 else g_col[0:1, :]
        decay = jnp.exp(jnp.where(incl[d], g_col - g_row, -jnp.inf))
        eg = jnp.exp(g_col)
        gls[hp, d, c] = jnp.broadcast_to(jnp.exp(g_tot), (8, LANES))
        a = jnp.where(strict[d], beta * decay * kk, 0.0)
        rhs = jnp.concatenate([beta * v, (beta * eg) * k], axis=1).astype(BF16)
        lhs2 = jnp.concatenate([(k * jnp.exp(g_tot - g_col)).T.astype(BF16), (qk * decay).astype(BF16)], axis=0)
        return a, rhs, lhs2, q * eg

    def prepare(i, carry):
        raws = [abt_ref[i * unroll + j] for j in range(unroll)]
        log_as = [-jnp.exp(gp_ref[0:n_half, :]) * _softplus(raw[0:n_half, :] + gp_ref[n_half:, :]) for raw in raws]
        cum_f_all = _dot(split3_lanes(jnp.concatenate(log_as, axis=0)), trir_ref[...])
        gate_info = []
        for j in range(unroll):
            cum_f = cum_f_all[j * n_half:(j + 1) * n_half, :]
            cum_b = cum_f[:, CHUNK - 1:CHUNK] - cum_f + log_as[j]
            gate_cols = jnp.concatenate([cum_f, cum_b, jax.nn.sigmoid(raws[j][n_half:, :]),
                                         jnp.zeros((CHUNK - 3 * n_half, CHUNK), F32)], axis=0).T
            gate_info.append(((cum_f, cum_b), gate_cols))
        chains = []
        for hp in range(hp_n):
            for j in range(unroll):
                c = i * unroll + j
                r0 = pl.multiple_of(c * CHUNK, CHUNK)
                qb = q_ref[pl.ds(r0, CHUNK), head_cols(hp)]
                kb = k_ref[pl.ds(r0, CHUNK), head_cols(hp)]
                q = qb.astype(F32)
                k = kb.astype(F32)
                v = v_ref[pl.ds(r0, CHUNK), head_cols(hp)].astype(F32)
                kq = lax.dot_general(jnp.concatenate([kb, qb], axis=0), kb,
                                     (((1,), (1,)), ((), ())), preferred_element_type=F32)
                for d in range(N_DIR):
                    chains.append((hp, j, c, d) + chain_front(hp, c, d, q, k, v, kq[0:CHUNK, :], kq[CHUNK:, :],
                                                              *gate_info[j]))
        t_invs = _tri_inverse([ch[4] for ch in chains], fold_eye, fold_masks, top_mask, lane_lo)
        o_local = {}
        for (hp, j, c, d, _, rhs, lhs2, qg), t_inv in zip(chains, t_invs):
            sol = _dot(t_inv.astype(BF16), rhs).astype(BF16)
            x = _dot(lhs2, sol)
            nst_sc[hp, d, c] = x[0:HEAD_DIM, 0:HEAD_DIM]
            lhs_sc[hp, d, c, 0:HEAD_DIM, :] = (-x[0:HEAD_DIM, HEAD_DIM:]).astype(BF16)
            lhs_sc[hp, d, c, HEAD_DIM:, :] = (qg - x[HEAD_DIM:, HEAD_DIM:]).astype(BF16)
            o_local[(hp, j, d)] = x[HEAD_DIM:, 0:HEAD_DIM]
        for hp in range(hp_n):
            for j in range(unroll):
                c = i * unroll + j
                r0 = pl.multiple_of(c * CHUNK, CHUNK)
                osc[hp, pl.ds(r0, CHUNK), :] = o_local[(hp, j, 0)] + o_local[(hp, j, 1)]
        return carry

    lax.fori_loop(0, nc // unroll, prepare, 0)

    def scan_step(hp, c, d, s):
        r0 = pl.multiple_of(c * CHUNK, CHUNK)
        r = _dot(lhs_sc[hp, d, c], s.astype(BF16))
        osc[hp, pl.ds(r0, CHUNK), :] += r[HEAD_DIM:, :]
        return gls[hp, d, c][0:1, :] * s + r[0:HEAD_DIM, :] + nst_sc[hp, d, c]

    def finalize(hp, c):
        r0 = pl.multiple_of(c * CHUNK, CHUNK)
        o = osc[hp, pl.ds(r0, CHUNK), :]
        y = o * lax.rsqrt(jnp.mean(o * o, axis=-1, keepdims=True) + RMS_EPS) * go_ref[...]
        zz = z_ref[pl.ds(r0, CHUNK), head_cols(hp)].astype(F32)
        og_ref[pl.ds(r0, CHUNK), head_cols(hp)] = (y * _silu(zz)).astype(BF16)

    def scan_body(i, carry, finish):
        out = []
        for hp in range(hp_n):
            out.append(scan_step(hp, i, 0, carry[2 * hp]))
            out.append(scan_step(hp, nc - 1 - i, 1, carry[2 * hp + 1]))
        if finish:
            for hp in range(hp_n):
                finalize(hp, i)
                finalize(hp, nc - 1 - i)
        return tuple(out)

    init = []
    for hp in range(hp_n):
        if has_state:
            init += [sf0_ref[hp], sb0_ref[hp]]
        else:
            init += [jnp.zeros((HEAD_DIM, HEAD_DIM), F32), jnp.zeros((HEAD_DIM, HEAD_DIM), F32)]
    half = lax.fori_loop(0, nc // 2, functools.partial(scan_body, finish=False), tuple(init))
    fin = lax.fori_loop(nc // 2, nc, functools.partial(scan_body, finish=True), half)
    for hp in range(hp_n):
        sf_ref[hp] = fin[2 * hp]
        sb_ref[hp] = fin[2 * hp + 1]


def _tri_tables():
    i = np.arange(CHUNK)
    up = (i[:, None] <= i[None, :]).astype(np.float32)
    return jnp.asarray(np.concatenate([up, up, up], axis=0)).astype(BF16)


def _delta_call(qkv, z, abt, a_log, dt_bias, g_o, s0_f, s0_b, *, heads_per_step, unroll):
    b, n, _ = qkv.shape
    nc = n // CHUNK
    assert nc % 2 == 0 and nc % unroll == 0
    hp_n = heads_per_step
    groups = N_HEADS // hp_n
    wide = hp_n * HEAD_DIM
    has_state = s0_f is not None
    trir = _tri_tables()
    col = lambda off: pl.BlockSpec((None, n, wide), lambda i, g: (i, 0, off + g))
    st = pl.BlockSpec((None, None, hp_n, HEAD_DIM, HEAD_DIM), lambda i, g: (i, 0, g, 0, 0))
    n_gate = 2 * N_DIR * N_HEADS
    gate_params = jnp.broadcast_to(jnp.concatenate([a_log.reshape(-1), dt_bias.reshape(-1)])[:, None],
                                   (n_gate, CHUNK))
    in_specs = [pl.BlockSpec((n_gate, CHUNK), lambda i, g: (0, 0)),
                col(0), col(groups), col(2 * groups), col(0),
                pl.BlockSpec((None, nc, n_gate, CHUNK), lambda i, g: (i, 0, 0, 0)),
                pl.BlockSpec((1, HEAD_DIM), lambda i, g: (0, 0)),
                pl.BlockSpec((3 * CHUNK, CHUNK), lambda i, g: (0, 0))]
    args = [gate_params, qkv, qkv, qkv, z, abt, g_o.reshape(1, HEAD_DIM), trir]
    if has_state:
        in_specs += [st, st]
        args += [s0_f, s0_b]
    state_shape = jax.ShapeDtypeStruct((b, 1, N_HEADS, HEAD_DIM, HEAD_DIM), F32)
    return pl.pallas_call(
        functools.partial(_delta_kernel, n=n, hp_n=hp_n, unroll=unroll, has_state=has_state),
        grid=(b, groups),
        in_specs=in_specs,
        out_specs=[col(0), st, st],
        out_shape=[jax.ShapeDtypeStruct((b, n, QK_WIDTH), BF16), state_shape, state_shape],
        scratch_shapes=[pltpu.VMEM((hp_n, N_DIR, nc, HEAD_DIM + CHUNK, HEAD_DIM), BF16),
                        pltpu.VMEM((hp_n, N_DIR, nc, HEAD_DIM, HEAD_DIM), F32),
                        pltpu.VMEM((hp_n, N_DIR, nc, 8, LANES), F32),
                        pltpu.VMEM((hp_n, n, HEAD_DIM), F32)],
        compiler_params=_params(),
    )(*args)


def _dft_tables(n, scale=1.0):
    idx = np.arange(n)
    ang = 2.0 * np.pi * ((idx[:, None] * idx[None, :]) % n) / n
    return (np.cos(ang) * scale).astype(np.float32), (np.sin(ang) * scale).astype(np.float32)


def _fno_prompt_kernel(f_ref, wy_ref, cn_ref, sn_ref, o_ref):
    cn = cn_ref[...]
    sn = sn_ref[...]
    for g in range(N_GROUPS):
        y = _dot(f_ref[:, g * HEAD_DIM:(g + 1) * HEAD_DIM], wy_ref[g])
        o = _dot(cn, y[:, 0:HEAD_DIM].astype(BF16)) + _dot(sn, y[:, HEAD_DIM:].astype(BF16))
        o_ref[:, g * HEAD_DIM:(g + 1) * HEAD_DIM] = o.astype(BF16)


def _fno_prompt_call(f, wy):
    b, n, w = f.shape
    cn, sn = _dft_tables(n, (n * HEAD_DIM) ** -0.5)
    cn = jnp.asarray(cn).astype(BF16)
    sn_neg = jnp.asarray(-sn).astype(BF16)
    return pl.pallas_call(
        _fno_prompt_kernel,
        grid=(b,),
        in_specs=[pl.BlockSpec((None, n, w), lambda i: (i, 0, 0)),
                  pl.BlockSpec((N_GROUPS, HEAD_DIM, 2 * HEAD_DIM), lambda i: (0, 0, 0)),
                  pl.BlockSpec((n, n), lambda i: (0, 0)),
                  pl.BlockSpec((n, n), lambda i: (0, 0))],
        out_specs=pl.BlockSpec((None, n, w), lambda i: (i, 0, 0)),
        out_shape=jax.ShapeDtypeStruct((b, n, w), BF16),
        compiler_params=_params(),
    )(f, wy, cn, sn_neg)


COL_UNROLL = 4
ROW_UNROLL = 8
ROW_PITCH = GRID_W + 8


def _fno_grid_kernel(f_ref, wy_ref, bdc_ref, bds_ref, crs_ref, o_ref, zr_sc, zi_sc, o_sc, *, n):
    tb = bdc_ref.shape[0]
    two = 2 * HEAD_DIM
    rows = n // GRID_W
    rows_per_block = tb // GRID_W

    def col_body(i, carry):
        blocks = [i * COL_UNROLL + j for j in range(COL_UNROLL)]
        ys = [_dot(f_ref[pl.ds(pl.multiple_of(blk * tb, tb), tb), :], wy_ref[...]).astype(BF16)
              for blk in blocks]
        zs = [_dot(bdc_ref[...], y[:, 0:two]) + _dot(bds_ref[...], y[:, two:]) for y in ys]
        for blk, z in zip(blocks, zs):
            for rr in range(rows_per_block):
                dst = pl.ds(pl.multiple_of((blk * rows_per_block + rr) * ROW_PITCH, 8), GRID_W)
                zr_sc[dst, :] = z[rr * GRID_W:(rr + 1) * GRID_W, 0:HEAD_DIM]
                zi_sc[dst, :] = z[rr * GRID_W:(rr + 1) * GRID_W, HEAD_DIM:]
        return carry

    lax.fori_loop(0, n // (tb * COL_UNROLL), col_body, 0)

    def row_body(i, carry):
        cols = [pl.ds(i * ROW_UNROLL + j, rows, stride=ROW_PITCH) for j in range(ROW_UNROLL)]
        zs = [jnp.concatenate([zr_sc[col, :], zi_sc[col, :]], axis=0).astype(BF16) for col in cols]
        outs = [_dot(crs_ref[...], z) for z in zs]
        for col, o in zip(cols, outs):
            o_sc[col, :] = o
        return carry

    lax.fori_loop(0, GRID_W // ROW_UNROLL, row_body, 0)

    def out_body(r, carry):
        src = pl.ds(pl.multiple_of(r * ROW_PITCH, 8), GRID_W)
        o_ref[pl.ds(pl.multiple_of(r * GRID_W, GRID_W), GRID_W), :] = o_sc[src, :].astype(BF16)
        return carry

    lax.fori_loop(0, rows, out_body, 0)


def _fno_grid_call(f, wy4):
    b, n, w = f.shape
    rows = n // GRID_W
    tb = 256
    cw, sw = _dft_tables(GRID_W)
    rep = np.eye(tb // GRID_W, dtype=np.float32)
    bdc = jnp.asarray(np.kron(rep, cw)).astype(BF16)
    bds = jnp.asarray(np.kron(rep, sw)).astype(BF16)
    cr, sr = _dft_tables(rows, (n * HEAD_DIM) ** -0.5)
    crs = jnp.asarray(np.concatenate([cr, sr], axis=1)).astype(BF16)
    const = lambda s: pl.BlockSpec(s, lambda i, g: (0,) * len(s))
    return pl.pallas_call(
        functools.partial(_fno_grid_kernel, n=n),
        grid=(b, N_GROUPS),
        in_specs=[pl.BlockSpec((None, n, HEAD_DIM), lambda i, g: (i, 0, g)),
                  pl.BlockSpec((None, HEAD_DIM, 4 * HEAD_DIM), lambda i, g: (g, 0, 0)),
                  const((tb, tb)), const((tb, tb)), const((rows, 2 * rows))],
        out_specs=pl.BlockSpec((None, n, HEAD_DIM), lambda i, g: (i, 0, g)),
        out_shape=jax.ShapeDtypeStruct((b, n, w), BF16),
        scratch_shapes=[pltpu.VMEM((rows * ROW_PITCH, HEAD_DIM), F32)] * 3,
        compiler_params=_params(),
    )(f, wy4, bdc, bds, crs)


FF_BLOCK = 256
FFN_SUB_ROWS = 512


def _ffn_kernel(x_ref, og_ref, fo_ref, mod_ref, woa_ref, wob_ref, gffn_ref, wg_ref, wu_ref, wd_ref,
                gfin_ref, y_ref, *, n_sub):
    d = D_MODEL
    gate1 = mod_ref[:, 2 * d:3 * d]
    shift2 = mod_ref[:, 3 * d:4 * d]
    scale2 = mod_ref[:, 4 * d:5 * d]
    gate2 = mod_ref[:, 5 * d:6 * d]
    sub = x_ref.shape[0] // n_sub
    blocks = [slice(s * sub, (s + 1) * sub) for s in range(n_sub)]
    x1s, h2s = [], []
    for rs in blocks:
        mo = _dot(og_ref[rs, :], woa_ref[...]) + _dot(fo_ref[rs, :], wob_ref[...])
        x1 = x_ref[rs, :] + gate1 * mo
        hn = x1 * lax.rsqrt(jnp.mean(x1 * x1, axis=-1, keepdims=True) + RMS_EPS) * gffn_ref[...]
        x1s.append(x1)
        h2s.append((hn * (1.0 + scale2) + shift2).astype(BF16))
    accs = [jnp.zeros((sub, d), F32) for _ in blocks]
    for j in range(D_FF // FF_BLOCK):
        sl = slice(j * FF_BLOCK, (j + 1) * FF_BLOCK)
        for s in range(n_sub):
            gt = _dot(h2s[s], wg_ref[:, sl])
            up = _dot(h2s[s], wu_ref[:, sl])
            accs[s] = accs[s] + _dot((_silu(gt) * up).astype(BF16), wd_ref[sl, :])
    for rs, x1, acc in zip(blocks, x1s, accs):
        x2 = x1 + gate2 * acc
        y_ref[rs, :] = x2 * lax.rsqrt(jnp.mean(x2 * x2, axis=-1, keepdims=True) + RMS_EPS) * gfin_ref[...]


def _ffn_call(x, og, fo, mod3, mod_row0, w_out_a, w_out_b, g_ffn, w_g, w_u, w_down, g_final, tm):
    b, n, d = x.shape
    tok = lambda w: pl.BlockSpec((None, tm, w), lambda i, t: (i, t, 0))
    const = lambda s: pl.BlockSpec(s, lambda i, t: (0,) * len(s), pipeline_mode=pl.Buffered(1))
    return pl.pallas_call(
        functools.partial(_ffn_kernel, n_sub=tm // FFN_SUB_ROWS),
        grid=(b, n // tm),
        in_specs=[tok(d), tok(QK_WIDTH), tok(FOURIER_WIDTH),
                  pl.BlockSpec((None, 1, 6 * d), lambda i, t: (mod_row0 + i, 0, 0)),
                  const((QK_WIDTH, d)), const((FOURIER_WIDTH, d)), const((1, d)),
                  const((d, D_FF)), const((d, D_FF)), const((D_FF, d)), const((1, d))],
        out_specs=tok(d),
        out_shape=jax.ShapeDtypeStruct((b, n, d), F32),
        compiler_params=_params(),
    )(x, og, fo, mod3, w_out_a, w_out_b, g_ffn.reshape(1, d), w_g, w_u, w_down, g_final.reshape(1, d))


def _chunk_transposed(ab):
    b, n, _ = ab.shape
    g = ab[:, :, 0:2 * N_DIR * N_HEADS].reshape(b, n // CHUNK, CHUNK, 2 * N_DIR * N_HEADS)
    return jnp.swapaxes(g, 2, 3)


def kernel(x_prompt, x_sample, c, state_dn_fwd, state_dn_bwd, c_ctx, w_ada, b_ada, g_mix, w_in, w_conv,
           a_log, dt_bias, g_o, w_fno, w_out, g_ffn, w_gu, w_down, g_final):
    d = D_MODEL
    bp, np_, _ = x_prompt.shape
    bs, ns, _ = x_sample.shape
    l = 0

    wi = w_in[l]
    n_gate = 2 * N_DIR * N_HEADS
    g0 = QKV_WIDTH + QK_WIDTH
    w_cat = jnp.concatenate([wi[:, 0:g0], wi[:, g0 + n_gate:], wi[:, g0:g0 + n_gate],
                             jnp.zeros((d, LANES - n_gate), F32)], axis=1).astype(BF16)
    w_out_b16 = w_out[l].astype(BF16)
    w_out_a, w_out_b = w_out_b16[0:QK_WIDTH], w_out_b16[QK_WIDTH:]
    w_g = w_gu[l][:, 0:D_FF].astype(BF16)
    w_u = w_gu[l][:, D_FF:].astype(BF16)
    w_dn = w_down[l].astype(BF16)

    cond = jnp.concatenate([c_ctx[None, :], c, jnp.zeros((16 - 1 - bs, d), F32)], axis=0)
    mod = _mod_call(cond, w_ada[l], b_ada[l])
    mod3 = mod.reshape(16, 1, 6 * d)

    cc, sc = _dft_tables(HEAD_DIM)
    wy_p, wy_g = _fno_w_call(w_fno[l], jnp.asarray(cc), jnp.asarray(sc))

    xp = x_prompt.reshape(1, bp * np_, d)
    qkv, z, f, ab = _inproj_call(xp, mod3, 0, False, g_mix[l], w_cat, w_conv[l], 512, np_)
    qkv = qkv.reshape(bp, np_, QKV_WIDTH)
    z = z.reshape(bp, np_, QK_WIDTH)
    f = f.reshape(bp, np_, FOURIER_WIDTH)
    ab = ab.reshape(bp, np_, LANES)
    og, new_f, new_b = _delta_call(qkv, z, _chunk_transposed(ab), a_log[l], dt_bias[l], g_o[l], None, None,
                                   heads_per_step=4, unroll=2)
    fo = _fno_prompt_call(f, wy_p)
    y_prompt = _ffn_call(xp, og.reshape(1, bp * np_, QK_WIDTH), fo.reshape(1, bp * np_, FOURIER_WIDTH),
                         mod3, 0, w_out_a, w_out_b, g_ffn[l], w_g, w_u, w_dn, g_final, 512)
    y_prompt = y_prompt.reshape(bp, np_, d)

    qkv, z, f, ab = _inproj_call(x_sample, mod3, 1, True, g_mix[l], w_cat, w_conv[l], 512, ns)
    og, _, _ = _delta_call(qkv, z, _chunk_transposed(ab), a_log[l], dt_bias[l], g_o[l],
                           state_dn_fwd[:, l:l + 1], state_dn_bwd[:, l:l + 1], heads_per_step=2, unroll=8)
    fo = _fno_grid_call(f, wy_g)
    y_sample = _ffn_call(x_sample, og, fo, mod3, 1, w_out_a, w_out_b, g_ffn[l], w_g, w_u, w_dn, g_final, 1024)

    return (y_prompt, y_sample, new_f, new_b)
```

```python
import functools

import numpy as np
import jax
import jax.numpy as jnp
from jax import lax
from jax.experimental import pallas as pl
from jax.experimental.pallas import tpu as pltpu

D_MODEL = 1024
N_HEADS = 4
HEAD_DIM = 128
QK_WIDTH = N_HEADS * HEAD_DIM
QKV_WIDTH = 3 * QK_WIDTH
N_GROUPS = 4
FOURIER_WIDTH = N_GROUPS * HEAD_DIM
N_DIR = 2
GRID_W = 64
CHUNK = 128
D_FF = 2816
RMS_EPS = 1e-6
LANES = 128
W_CAT_COLS = QKV_WIDTH + QK_WIDTH + FOURIER_WIDTH + LANES
VMEM_LIMIT = 56 * 1024 * 1024

F32 = jnp.float32
BF16 = jnp.bfloat16
HIGHEST = lax.Precision.HIGHEST


def _dot(a, b):
    return jnp.dot(a, b, preferred_element_type=F32)


def _silu(x):
    return x * jax.nn.sigmoid(x)


def _softplus(x):
    return jnp.maximum(x, 0.0) + jnp.log1p(jnp.exp(-jnp.abs(x)))


def _params(**kw):
    return pltpu.CompilerParams(vmem_limit_bytes=VMEM_LIMIT, **kw)


def _mod_kernel(cond_ref, w_ref, b_ref, o_ref):
    s = _silu(cond_ref[...]).astype(BF16)
    o_ref[...] = _dot(s, w_ref[...].astype(BF16)) + b_ref[...]


def _mod_call(cond, w_ada, b_ada):
    rows, d = cond.shape
    cols = w_ada.shape[1]
    tn = 1536
    return pl.pallas_call(
        _mod_kernel,
        grid=(cols // tn,),
        in_specs=[pl.BlockSpec((rows, d), lambda j: (0, 0)),
                  pl.BlockSpec((d, tn), lambda j: (0, j)),
                  pl.BlockSpec((1, tn), lambda j: (0, j))],
        out_specs=pl.BlockSpec((rows, tn), lambda j: (0, j)),
        out_shape=jax.ShapeDtypeStruct((rows, cols), F32),
        compiler_params=_params(),
    )(cond, w_ada, b_ada.reshape(1, cols))


def _fno_w_kernel(w_ref, cc_ref, sc_ref, wp_ref, wg_ref):
    w = w_ref[...]
    cw = jnp.dot(cc_ref[...], w, precision=HIGHEST, preferred_element_type=F32)
    sw = jnp.dot(sc_ref[...], w, precision=HIGHEST, preferred_element_type=F32)
    wp_ref[...] = jnp.concatenate([cw, sw], axis=1).astype(BF16)
    wg_ref[...] = jnp.concatenate([cw, -sw, -sw, -cw], axis=1).astype(BF16)


def _fno_w_call(w_fno, cc, sc):
    g, c, _ = w_fno.shape
    return pl.pallas_call(
        _fno_w_kernel,
        grid=(g,),
        in_specs=[pl.BlockSpec((None, c, c), lambda i: (i, 0, 0)),
                  pl.BlockSpec((c, c), lambda i: (0, 0)),
                  pl.BlockSpec((c, c), lambda i: (0, 0))],
        out_specs=[pl.BlockSpec((None, c, 2 * c), lambda i: (i, 0, 0)),
                   pl.BlockSpec((None, c, 4 * c), lambda i: (i, 0, 0))],
        out_shape=[jax.ShapeDtypeStruct((g, c, 2 * c), BF16),
                   jax.ShapeDtypeStruct((g, c, 4 * c), BF16)],
        compiler_params=_params(),
    )(w_fno, cc, sc)


HALO = 8
CONV_COLS = 2 * HEAD_DIM
CONV_ROWS = 128


def _inproj_kernel(xp_ref, x_ref, xn_ref, mod_ref, g_ref, w_ref, wc_ref, qkv_ref, z_ref, f_ref, ab_ref, *,
                   tm, seq_len):
    t = pl.program_id(1)
    x = jnp.concatenate([xp_ref[...], x_ref[...], xn_ref[...]], axis=0)
    y = x * lax.rsqrt(jnp.mean(x * x, axis=-1, keepdims=True) + RMS_EPS) * g_ref[...]
    shift1 = mod_ref[:, 0:D_MODEL]
    scale1 = mod_ref[:, D_MODEL:2 * D_MODEL]
    h_all = (y * (1.0 + scale1) + shift1).astype(BF16)
    h = h_all[HALO:HALO + tm, :]
    at_start = ((t * tm) & (seq_len - 1)) == 0
    at_end = (((t + 1) * tm) & (seq_len - 1)) == 0
    zero_halo = jnp.zeros((HALO, D_MODEL), BF16)
    h_ext = jnp.concatenate([jnp.where(at_start, zero_halo, h_all[0:HALO, :]), h,
                             jnp.where(at_end, zero_halo, h_all[HALO + tm:, :])], axis=0)

    rows = lax.broadcasted_iota(jnp.int32, (CONV_ROWS, HEAD_DIM), 0)
    n_blocks = tm // CONV_ROWS
    for j in range(QKV_WIDTH // CONV_COLS):
        p_wide = _dot(h_ext, w_ref[:, j * CONV_COLS:(j + 1) * CONV_COLS])
        for hh in range(CONV_COLS // HEAD_DIM):
            head = j * (CONV_COLS // HEAD_DIM) + hh
            cols = slice(head * HEAD_DIM, (head + 1) * HEAD_DIM)
            lanes = slice(hh * HEAD_DIM, (hh + 1) * HEAD_DIM)
            for rb in range(n_blocks):
                lo = HALO + rb * CONV_ROWS
                p = p_wide[lo:lo + CONV_ROWS, lanes]
                starts_seq = rb > 0 and (rb * CONV_ROWS) % seq_len == 0
                ends_seq = rb < n_blocks - 1 and ((rb + 1) * CONV_ROWS) % seq_len == 0
                before = 0.0 if starts_seq else p_wide[lo - 1:lo, lanes]
                after = 0.0 if ends_seq else p_wide[lo + CONV_ROWS:lo + CONV_ROWS + 1, lanes]
                p_prev = jnp.where(rows == 0, before, pltpu.roll(p, 1, 0))
                p_next = jnp.where(rows == CONV_ROWS - 1, after, pltpu.roll(p, CONV_ROWS - 1, 0))
                ch = _silu(wc_ref[0:1, cols] * p_prev + wc_ref[1:2, cols] * p + wc_ref[2:3, cols] * p_next)
                if head < 2 * N_HEADS:
                    ch = ch * lax.rsqrt(jnp.sum(ch * ch, axis=-1, keepdims=True) + 1e-6)
                if head < N_HEADS:
                    ch = ch * (HEAD_DIM ** -0.5)
                qkv_ref[rb * CONV_ROWS:(rb + 1) * CONV_ROWS, cols] = ch.astype(BF16)
    c0, c1, c2 = QKV_WIDTH, QKV_WIDTH + QK_WIDTH, QKV_WIDTH + QK_WIDTH + FOURIER_WIDTH
    z_ref[...] = _dot(h, w_ref[:, c0:c1]).astype(BF16)
    f_ref[...] = _dot(h, w_ref[:, c1:c2]).astype(BF16)
    ab_ref[...] = _dot(h, w_ref[:, c2:W_CAT_COLS])


def _inproj_call(x, mod3, mod_row0, mod_per_batch, g_mix, w_cat, w_conv, tm, seq_len):
    b, n, d = x.shape
    assert seq_len & (seq_len - 1) == 0 and n % tm == 0 and tm % CONV_ROWS == 0 and seq_len % CONV_ROWS == 0
    assert tm % seq_len == 0 or seq_len % tm == 0
    tok = lambda w: pl.BlockSpec((None, tm, w), lambda i, t: (i, t, 0))
    per = tm // HALO
    return pl.pallas_call(
        functools.partial(_inproj_kernel, tm=tm, seq_len=seq_len),
        grid=(b, n // tm),
        in_specs=[pl.BlockSpec((None, HALO, d), lambda i, t: (i, jnp.maximum(t * per - 1, 0), 0)),
                  tok(d),
                  pl.BlockSpec((None, HALO, d), lambda i, t: (i, jnp.minimum((t + 1) * per, n // HALO - 1), 0)),
                  pl.BlockSpec((None, 1, 6 * d), lambda i, t: (mod_row0 + (i if mod_per_batch else 0), 0, 0)),
                  pl.BlockSpec((1, d), lambda i, t: (0, 0)),
                  pl.BlockSpec((d, W_CAT_COLS), lambda i, t: (0, 0)),
                  pl.BlockSpec((3, QKV_WIDTH), lambda i, t: (0, 0))],
        out_specs=[tok(QKV_WIDTH), tok(QK_WIDTH), tok(FOURIER_WIDTH), tok(LANES)],
        out_shape=[jax.ShapeDtypeStruct((b, n, QKV_WIDTH), BF16),
                   jax.ShapeDtypeStruct((b, n, QK_WIDTH), BF16),
                   jax.ShapeDtypeStruct((b, n, FOURIER_WIDTH), BF16),
                   jax.ShapeDtypeStruct((b, n, LANES), F32)],
        compiler_params=_params(),
    )(x, x, x, mod3, g_mix.reshape(1, d), w_cat, w_conv)


LEVEL_BLOCKS = tuple(2 ** e for e in range(int(np.log2(CHUNK))))


def _tri_inverse(a_list, fold_eye, fold_masks, top_mask, lane_lo):
    half = CHUNK // 2

    def unfold(f):
        return jnp.concatenate([jnp.where(lane_lo, f, 0.0), jnp.where(lane_lo, 0.0, f)], axis=0)

    a_folds = [jnp.where(lane_lo, a[0:half, :], a[half:, :]) for a in a_list]
    fs = [fold_eye - jnp.where(fold_masks[0], af, 0.0) for af in a_folds]
    for mask in fold_masks[1:]:
        ls = [unfold(jnp.where(mask, af, 0.0)).astype(BF16) for af in a_folds]
        ms = [_dot(f.astype(BF16), l).astype(BF16) for f, l in zip(fs, ls)]
        fs = [f - _dot(m, unfold(f).astype(BF16)) for f, m in zip(fs, ms)]
    ds = [unfold(f) for f in fs]
    ls = [jnp.where(top_mask, a, 0.0).astype(BF16) for a in a_list]
    dbs = [d.astype(BF16) for d in ds]
    ms = [_dot(db, l).astype(BF16) for db, l in zip(dbs, ls)]
    return [d - _dot(m, db) for d, m, db in zip(ds, ms, dbs)]


def _delta_kernel(*refs, n, hp_n, unroll, has_state):
    if has_state:
        (gp_ref, q_ref, k_ref, v_ref, z_ref, abt_ref, go_ref, trir_ref, sf0_ref, sb0_ref,
         og_ref, sf_ref, sb_ref, lhs_sc, nst_sc, gls, osc) = refs
    else:
        (gp_ref, q_ref, k_ref, v_ref, z_ref, abt_ref, go_ref, trir_ref,
         og_ref, sf_ref, sb_ref, lhs_sc, nst_sc, gls, osc) = refs
    hg = pl.program_id(1)
    nc = n // CHUNK

    ci = lax.broadcasted_iota(jnp.int32, (CHUNK, CHUNK), 0)
    cj = lax.broadcasted_iota(jnp.int32, (CHUNK, CHUNK), 1)
    top_mask = (ci // (CHUNK // 2)) != (cj // (CHUNK // 2))
    fi = lax.broadcasted_iota(jnp.int32, (CHUNK // 2, CHUNK), 0)
    fl = lax.broadcasted_iota(jnp.int32, (CHUNK // 2, CHUNK), 1)
    fj = fl & (CHUNK // 2 - 1)
    lane_lo = fl < CHUNK // 2
    fold_eye = (fi == fj).astype(F32)
    fold_masks = [((fi // (2 * b)) == (fj // (2 * b))) & ((fi // b) != (fj // b)) for b in LEVEL_BLOCKS[:-1]]
    incl = (ci >= cj, ci <= cj)
    strict = (ci > cj, ci < cj)
    n_half = N_DIR * N_HEADS
    gate_rows = lax.broadcasted_iota(jnp.int32, (n_half, CHUNK), 0)

    def head_cols(hp):
        return slice(hp * HEAD_DIM, (hp + 1) * HEAD_DIM)

    def split3_lanes(x):
        hi = x.astype(BF16)
        r1 = x - hi.astype(F32)
        lo = r1.astype(BF16)
        lo2 = (r1 - lo.astype(F32)).astype(BF16)
        return jnp.concatenate([hi, lo, lo2], axis=1)

    def select_col(x, col):
        lanes = lax.broadcasted_iota(jnp.int32, x.shape, 1)
        return jnp.sum(jnp.where(lanes == col, x, 0.0), axis=-1, keepdims=True)

    def chain_front(hp, c, d, q, k, v, kk, qk, cum_rows, gate_cols):
        gate = d * N_HEADS + hg * hp_n + hp
        g_col = jnp.broadcast_to(select_col(gate_cols, d * n_half + gate), (CHUNK, LANES))
        beta = select_col(gate_cols, N_DIR * n_half + gate)
        g_row = jnp.sum(jnp.where(gate_rows == gate, cum_rows[d], 0.0), axis=0, keepdims=True)
        g_tot = g_col[CHUNK - 1:CHUNK, :] if d == 0 else g_col[0:1, :]
        decay = jnp.exp(jnp.where(incl[d], g_col - g_row, -jnp.inf))
        eg = jnp.exp(g_col)
        gls[hp, d, c] = jnp.broadcast_to(jnp.exp(g_tot), (8, LANES))
        a = jnp.where(strict[d], beta * decay * kk, 0.0)
        rhs = jnp.concatenate([beta * v, (beta * eg) * k], axis=1).astype(BF16)
        lhs2 = jnp.concatenate([(k * jnp.exp(g_tot - g_col)).T.astype(BF16), (qk * decay).astype(BF16)], axis=0)
        return a, rhs, lhs2, q * eg

    def prepare(i, carry):
        raws = [abt_ref[i * unroll + j] for j in range(unroll)]
        log_as = [-jnp.exp(gp_ref[0:n_half, :]) * _softplus(raw[0:n_half, :] + gp_ref[n_half:, :]) for raw in raws]
        cum_f_all = _dot(split3_lanes(jnp.concatenate(log_as, axis=0)), trir_ref[...])
        gate_info = []
        for j in range(unroll):
            cum_f = cum_f_all[j * n_half:(j + 1) * n_half, :]
            cum_b = cum_f[:, CHUNK - 1:CHUNK] - cum_f + log_as[j]
            gate_cols = jnp.concatenate([cum_f, cum_b, jax.nn.sigmoid(raws[j][n_half:, :]),
                                         jnp.zeros((CHUNK - 3 * n_half, CHUNK), F32)], axis=0).T
            gate_info.append(((cum_f, cum_b), gate_cols))
        chains = []
        for hp in range(hp_n):
            for j in range(unroll):
                c = i * unroll + j
                r0 = pl.multiple_of(c * CHUNK, CHUNK)
                qb = q_ref[pl.ds(r0, CHUNK), head_cols(hp)]
                kb = k_ref[pl.ds(r0, CHUNK), head_cols(hp)]
                q = qb.astype(F32)
                k = kb.astype(F32)
                v = v_ref[pl.ds(r0, CHUNK), head_cols(hp)].astype(F32)
                kq = lax.dot_general(jnp.concatenate([kb, qb], axis=0), kb,
                                     (((1,), (1,)), ((), ())), preferred_element_type=F32)
                for d in range(N_DIR):
                    chains.append((hp, j, c, d) + chain_front(hp, c, d, q, k, v, kq[0:CHUNK, :], kq[CHUNK:, :],
                                                              *gate_info[j]))
        t_invs = _tri_inverse([ch[4] for ch in chains], fold_eye, fold_masks, top_mask, lane_lo)
        o_local = {}
        for (hp, j, c, d, _, rhs, lhs2, qg), t_inv in zip(chains, t_invs):
            sol = _dot(t_inv.astype(BF16), rhs).astype(BF16)
            x = _dot(lhs2, sol)
            nst_sc[hp, d, c] = x[0:HEAD_DIM, 0:HEAD_DIM]
            lhs_sc[hp, d, c, 0:HEAD_DIM, :] = (-x[0:HEAD_DIM, HEAD_DIM:]).astype(BF16)
            lhs_sc[hp, d, c, HEAD_DIM:, :] = (qg - x[HEAD_DIM:, HEAD_DIM:]).astype(BF16)
            o_local[(hp, j, d)] = x[HEAD_DIM:, 0:HEAD_DIM]
        for hp in range(hp_n):
            for j in range(unroll):
                c = i * unroll + j
                r0 = pl.multiple_of(c * CHUNK, CHUNK)
                osc[hp, pl.ds(r0, CHUNK), :] = o_local[(hp, j, 0)] + o_local[(hp, j, 1)]
        return carry

    lax.fori_loop(0, nc // unroll, prepare, 0)

    def scan_step(hp, c, d, s):
        r0 = pl.multiple_of(c * CHUNK, CHUNK)
        r = _dot(lhs_sc[hp, d, c], s.astype(BF16))
        osc[hp, pl.ds(r0, CHUNK), :] += r[HEAD_DIM:, :]
        return gls[hp, d, c][0:1, :] * s + r[0:HEAD_DIM, :] + nst_sc[hp, d, c]

    def finalize(hp, c):
        r0 = pl.multiple_of(c * CHUNK, CHUNK)
        o = osc[hp, pl.ds(r0, CHUNK), :]
        y = o * lax.rsqrt(jnp.mean(o * o, axis=-1, keepdims=True) + RMS_EPS) * go_ref[...]
        zz = z_ref[pl.ds(r0, CHUNK), head_cols(hp)].astype(F32)
        og_ref[pl.ds(r0, CHUNK), head_cols(hp)] = (y * _silu(zz)).astype(BF16)

    def scan_body(i, carry, finish):
        out = []
        for hp in range(hp_n):
            out.append(scan_step(hp, i, 0, carry[2 * hp]))
            out.append(scan_step(hp, nc - 1 - i, 1, carry[2 * hp + 1]))
        if finish:
            for hp in range(hp_n):
                finalize(hp, i)
                finalize(hp, nc - 1 - i)
        return tuple(out)

    init = []
    for hp in range(hp_n):
        if has_state:
            init += [sf0_ref[hp], sb0_ref[hp]]
        else:
            init += [jnp.zeros((HEAD_DIM, HEAD_DIM), F32), jnp.zeros((HEAD_DIM, HEAD_DIM), F32)]
    half = lax.fori_loop(0, nc // 2, functools.partial(scan_body, finish=False), tuple(init))
    fin = lax.fori_loop(nc // 2, nc, functools.partial(scan_body, finish=True), half)
    for hp in range(hp_n):
        sf_ref[hp] = fin[2 * hp]
        sb_ref[hp] = fin[2 * hp + 1]


def _tri_tables():
    i = np.arange(CHUNK)
    up = (i[:, None] <= i[None, :]).astype(np.float32)
    return jnp.asarray(np.concatenate([up, up, up], axis=0)).astype(BF16)


def _delta_call(qkv, z, abt, a_log, dt_bias, g_o, s0_f, s0_b, *, heads_per_step, unroll):
    b, n, _ = qkv.shape
    nc = n // CHUNK
    assert nc % 2 == 0 and nc % unroll == 0
    hp_n = heads_per_step
    groups = N_HEADS // hp_n
    wide = hp_n * HEAD_DIM
    has_state = s0_f is not None
    trir = _tri_tables()
    col = lambda off: pl.BlockSpec((None, n, wide), lambda i, g: (i, 0, off + g))
    st = pl.BlockSpec((None, None, hp_n, HEAD_DIM, HEAD_DIM), lambda i, g: (i, 0, g, 0, 0))
    n_gate = 2 * N_DIR * N_HEADS
    gate_params = jnp.broadcast_to(jnp.concatenate([a_log.reshape(-1), dt_bias.reshape(-1)])[:, None],
                                   (n_gate, CHUNK))
    in_specs = [pl.BlockSpec((n_gate, CHUNK), lambda i, g: (0, 0)),
                col(0), col(groups), col(2 * groups), col(0),
                pl.BlockSpec((None, nc, n_gate, CHUNK), lambda i, g: (i, 0, 0, 0)),
                pl.BlockSpec((1, HEAD_DIM), lambda i, g: (0, 0)),
                pl.BlockSpec((3 * CHUNK, CHUNK), lambda i, g: (0, 0))]
    args = [gate_params, qkv, qkv, qkv, z, abt, g_o.reshape(1, HEAD_DIM), trir]
    if has_state:
        in_specs += [st, st]
        args += [s0_f, s0_b]
    state_shape = jax.ShapeDtypeStruct((b, 1, N_HEADS, HEAD_DIM, HEAD_DIM), F32)
    return pl.pallas_call(
        functools.partial(_delta_kernel, n=n, hp_n=hp_n, unroll=unroll, has_state=has_state),
        grid=(b, groups),
        in_specs=in_specs,
        out_specs=[col(0), st, st],
        out_shape=[jax.ShapeDtypeStruct((b, n, QK_WIDTH), BF16), state_shape, state_shape],
        scratch_shapes=[pltpu.VMEM((hp_n, N_DIR, nc, HEAD_DIM + CHUNK, HEAD_DIM), BF16),
                        pltpu.VMEM((hp_n, N_DIR, nc, HEAD_DIM, HEAD_DIM), F32),
                        pltpu.VMEM((hp_n, N_DIR, nc, 8, LANES), F32),
                        pltpu.VMEM((hp_n, n, HEAD_DIM), F32)],
        compiler_params=_params(),
    )(*args)


def _dft_tables(n, scale=1.0):
    idx = np.arange(n)
    ang = 2.0 * np.pi * ((idx[:, None] * idx[None, :]) % n) / n
    return (np.cos(ang) * scale).astype(np.float32), (np.sin(ang) * scale).astype(np.float32)


def _fno_prompt_kernel(f_ref, wy_ref, cn_ref, sn_ref, o_ref):
    cn = cn_ref[...]
    sn = sn_ref[...]
    for g in range(N_GROUPS):
        y = _dot(f_ref[:, g * HEAD_DIM:(g + 1) * HEAD_DIM], wy_ref[g])
        o = _dot(cn, y[:, 0:HEAD_DIM].astype(BF16)) + _dot(sn, y[:, HEAD_DIM:].astype(BF16))
        o_ref[:, g * HEAD_DIM:(g + 1) * HEAD_DIM] = o.astype(BF16)


def _fno_prompt_call(f, wy):
    b, n, w = f.shape
    cn, sn = _dft_tables(n, (n * HEAD_DIM) ** -0.5)
    cn = jnp.asarray(cn).astype(BF16)
    sn_neg = jnp.asarray(-sn).astype(BF16)
    return pl.pallas_call(
        _fno_prompt_kernel,
        grid=(b,),
        in_specs=[pl.BlockSpec((None, n, w), lambda i: (i, 0, 0)),
                  pl.BlockSpec((N_GROUPS, HEAD_DIM, 2 * HEAD_DIM), lambda i: (0, 0, 0)),
                  pl.BlockSpec((n, n), lambda i: (0, 0)),
                  pl.BlockSpec((n, n), lambda i: (0, 0))],
        out_specs=pl.BlockSpec((None, n, w), lambda i: (i, 0, 0)),
        out_shape=jax.ShapeDtypeStruct((b, n, w), BF16),
        compiler_params=_params(),
    )(f, wy, cn, sn_neg)


COL_UNROLL = 4
ROW_UNROLL = 16
ROW_PITCH = GRID_W + 8


def _fno_grid_kernel(f_ref, wy_ref, bdc_ref, bds_ref, crs_ref, o_ref, zr_sc, zi_sc, o_sc, *, n):
    tb = bdc_ref.shape[0]
    two = 2 * HEAD_DIM
    rows = n // GRID_W
    rows_per_block = tb // GRID_W

    def col_body(i, carry):
        blocks = [i * COL_UNROLL + j for j in range(COL_UNROLL)]
        ys = [_dot(f_ref[pl.ds(pl.multiple_of(blk * tb, tb), tb), :], wy_ref[...]).astype(BF16)
              for blk in blocks]
        zs = [_dot(bdc_ref[...], y[:, 0:two]) + _dot(bds_ref[...], y[:, two:]) for y in ys]
        for blk, z in zip(blocks, zs):
            for rr in range(rows_per_block):
                dst = pl.ds(pl.multiple_of((blk * rows_per_block + rr) * ROW_PITCH, 8), GRID_W)
                zr_sc[dst, :] = z[rr * GRID_W:(rr + 1) * GRID_W, 0:HEAD_DIM]
                zi_sc[dst, :] = z[rr * GRID_W:(rr + 1) * GRID_W, HEAD_DIM:]
        return carry

    lax.fori_loop(0, n // (tb * COL_UNROLL), col_body, 0)

    def row_body(i, carry):
        cols = [pl.ds(i * ROW_UNROLL + j, rows, stride=ROW_PITCH) for j in range(ROW_UNROLL)]
        zs = [jnp.concatenate([zr_sc[col, :], zi_sc[col, :]], axis=0).astype(BF16) for col in cols]
        outs = [_dot(crs_ref[...], jnp.concatenate([zs[j], zs[j + 1]], axis=1)) for j in range(0, ROW_UNROLL, 2)]
        for j in range(0, ROW_UNROLL, 2):
            o_sc[cols[j], :] = outs[j // 2][:, 0:HEAD_DIM]
            o_sc[cols[j + 1], :] = outs[j // 2][:, HEAD_DIM:]
        return carry

    lax.fori_loop(0, GRID_W // ROW_UNROLL, row_body, 0)

    def out_body(r, carry):
        src = pl.ds(pl.multiple_of(r * ROW_PITCH, 8), GRID_W)
        o_ref[pl.ds(pl.multiple_of(r * GRID_W, GRID_W), GRID_W), :] = o_sc[src, :].astype(BF16)
        return carry

    lax.fori_loop(0, rows, out_body, 0)


def _fno_grid_call(f, wy4):
    b, n, w = f.shape
    rows = n // GRID_W
    tb = 256
    cw, sw = _dft_tables(GRID_W)
    rep = np.eye(tb // GRID_W, dtype=np.float32)
    bdc = jnp.asarray(np.kron(rep, cw)).astype(BF16)
    bds = jnp.asarray(np.kron(rep, sw)).astype(BF16)
    cr, sr = _dft_tables(rows, (n * HEAD_DIM) ** -0.5)
    crs = jnp.asarray(np.concatenate([cr, sr], axis=1)).astype(BF16)
    const = lambda s: pl.BlockSpec(s, lambda i, g: (0,) * len(s))
    return pl.pallas_call(
        functools.partial(_fno_grid_kernel, n=n),
        grid=(b, N_GROUPS),
        in_specs=[pl.BlockSpec((None, n, HEAD_DIM), lambda i, g: (i, 0, g)),
                  pl.BlockSpec((None, HEAD_DIM, 4 * HEAD_DIM), lambda i, g: (g, 0, 0)),
                  const((tb, tb)), const((tb, tb)), const((rows, 2 * rows))],
        out_specs=pl.BlockSpec((None, n, HEAD_DIM), lambda i, g: (i, 0, g)),
        out_shape=jax.ShapeDtypeStruct((b, n, w), BF16),
        scratch_shapes=[pltpu.VMEM((rows * ROW_PITCH, HEAD_DIM), F32)] * 3,
        compiler_params=_params(),
    )(f, wy4, bdc, bds, crs)


FF_BLOCK = 256
FFN_SUB_ROWS = 512


def _ffn_kernel(x_ref, og_ref, fo_ref, mod_ref, woa_ref, wob_ref, gffn_ref, wg_ref, wu_ref, wd_ref,
                gfin_ref, y_ref, *, n_sub):
    d = D_MODEL
    gate1 = mod_ref[:, 2 * d:3 * d]
    shift2 = mod_ref[:, 3 * d:4 * d]
    scale2 = mod_ref[:, 4 * d:5 * d]
    gate2 = mod_ref[:, 5 * d:6 * d]
    sub = x_ref.shape[0] // n_sub
    blocks = [slice(s * sub, (s + 1) * sub) for s in range(n_sub)]
    x1s, h2s = [], []
    for rs in blocks:
        mo = _dot(og_ref[rs, :], woa_ref[...]) + _dot(fo_ref[rs, :], wob_ref[...])
        x1 = x_ref[rs, :] + gate1 * mo
        hn = x1 * lax.rsqrt(jnp.mean(x1 * x1, axis=-1, keepdims=True) + RMS_EPS) * gffn_ref[...]
        x1s.append(x1)
        h2s.append((hn * (1.0 + scale2) + shift2).astype(BF16))
    accs = [jnp.zeros((sub, d), F32) for _ in blocks]
    for j in range(D_FF // FF_BLOCK):
        sl = slice(j * FF_BLOCK, (j + 1) * FF_BLOCK)
        for s in range(n_sub):
            gt = _dot(h2s[s], wg_ref[:, sl])
            up = _dot(h2s[s], wu_ref[:, sl])
            accs[s] = accs[s] + _dot((_silu(gt) * up).astype(BF16), wd_ref[sl, :])
    for rs, x1, acc in zip(blocks, x1s, accs):
        x2 = x1 + gate2 * acc
        y_ref[rs, :] = x2 * lax.rsqrt(jnp.mean(x2 * x2, axis=-1, keepdims=True) + RMS_EPS) * gfin_ref[...]


def _ffn_call(x, og, fo, mod3, mod_row0, w_out_a, w_out_b, g_ffn, w_g, w_u, w_down, g_final, tm):
    b, n, d = x.shape
    tok = lambda w: pl.BlockSpec((None, tm, w), lambda i, t: (i, t, 0))
    const = lambda s: pl.BlockSpec(s, lambda i, t: (0,) * len(s), pipeline_mode=pl.Buffered(1))
    return pl.pallas_call(
        functools.partial(_ffn_kernel, n_sub=tm // FFN_SUB_ROWS),
        grid=(b, n // tm),
        in_specs=[tok(d), tok(QK_WIDTH), tok(FOURIER_WIDTH),
                  pl.BlockSpec((None, 1, 6 * d), lambda i, t: (mod_row0 + i, 0, 0)),
                  const((QK_WIDTH, d)), const((FOURIER_WIDTH, d)), const((1, d)),
                  const((d, D_FF)), const((d, D_FF)), const((D_FF, d)), const((1, d))],
        out_specs=tok(d),
        out_shape=jax.ShapeDtypeStruct((b, n, d), F32),
        compiler_params=_params(),
    )(x, og, fo, mod3, w_out_a, w_out_b, g_ffn.reshape(1, d), w_g, w_u, w_down, g_final.reshape(1, d))


def _chunk_transposed(ab):
    b, n, _ = ab.shape
    g = ab[:, :, 0:2 * N_DIR * N_HEADS].reshape(b, n // CHUNK, CHUNK, 2 * N_DIR * N_HEADS)
    return jnp.swapaxes(g, 2, 3)


def kernel(x_prompt, x_sample, c, state_dn_fwd, state_dn_bwd, c_ctx, w_ada, b_ada, g_mix, w_in, w_conv,
           a_log, dt_bias, g_o, w_fno, w_out, g_ffn, w_gu, w_down, g_final):
    d = D_MODEL
    bp, np_, _ = x_prompt.shape
    bs, ns, _ = x_sample.shape
    l = 0

    wi = w_in[l]
    n_gate = 2 * N_DIR * N_HEADS
    g0 = QKV_WIDTH + QK_WIDTH
    w_cat = jnp.concatenate([wi[:, 0:g0], wi[:, g0 + n_gate:], wi[:, g0:g0 + n_gate],
                             jnp.zeros((d, LANES - n_gate), F32)], axis=1).astype(BF16)
    w_out_b16 = w_out[l].astype(BF16)
    w_out_a, w_out_b = w_out_b16[0:QK_WIDTH], w_out_b16[QK_WIDTH:]
    w_g = w_gu[l][:, 0:D_FF].astype(BF16)
    w_u = w_gu[l][:, D_FF:].astype(BF16)
    w_dn = w_down[l].astype(BF16)

    cond = jnp.concatenate([c_ctx[None, :], c, jnp.zeros((16 - 1 - bs, d), F32)], axis=0)
    mod = _mod_call(cond, w_ada[l], b_ada[l])
    mod3 = mod.reshape(16, 1, 6 * d)

    cc, sc = _dft_tables(HEAD_DIM)
    wy_p, wy_g = _fno_w_call(w_fno[l], jnp.asarray(cc), jnp.asarray(sc))

    xp = x_prompt.reshape(1, bp * np_, d)
    qkv, z, f, ab = _inproj_call(xp, mod3, 0, False, g_mix[l], w_cat, w_conv[l], 512, np_)
    qkv = qkv.reshape(bp, np_, QKV_WIDTH)
    z = z.reshape(bp, np_, QK_WIDTH)
    f = f.reshape(bp, np_, FOURIER_WIDTH)
    ab = ab.reshape(bp, np_, LANES)
    og, new_f, new_b = _delta_call(qkv, z, _chunk_transposed(ab), a_log[l], dt_bias[l], g_o[l], None, None,
                                   heads_per_step=4, unroll=2)
    fo = _fno_prompt_call(f, wy_p)
    y_prompt = _ffn_call(xp, og.reshape(1, bp * np_, QK_WIDTH), fo.reshape(1, bp * np_, FOURIER_WIDTH),
                         mod3, 0, w_out_a, w_out_b, g_ffn[l], w_g, w_u, w_dn, g_final, 512)
    y_prompt = y_prompt.reshape(bp, np_, d)

    qkv, z, f, ab = _inproj_call(x_sample, mod3, 1, True, g_mix[l], w_cat, w_conv[l], 512, ns)
    og, _, _ = _delta_call(qkv, z, _chunk_transposed(ab), a_log[l], dt_bias[l], g_o[l],
                           state_dn_fwd[:, l:l + 1], state_dn_bwd[:, l:l + 1], heads_per_step=2, unroll=4)
    fo = _fno_grid_call(f, wy_g)
    y_sample = _ffn_call(x_sample, og, fo, mod3, 1, w_out_a, w_out_b, g_ffn[l], w_g, w_u, w_dn, g_final, 1024)

    return (y_prompt, y_sample, new_f, new_b)
```
